```python
import math
import jax, jax.numpy as jnp
from jax import lax
import numpy as np

D_MODEL = 1024
BATCH = 4
SEQ = 4096
DEPTH = 4
DEC_BATCH = 128
DEC_SEQ = 4
PAST_LEN = 2048
PAGE_SIZE = 128

A_HEADS = 4
A_DK = 128
A_DV = 128
A_CONV = 4
A_CHUNK = 64
B_GROUPS = ((128, 1), (512, 4), (2048, 16))
B_HEADS = 4
B_HD = 64
B_ROT = B_HD // 4
ROPE_THETA = 500000.0
C_WIDTH = 512
C_BLOCKS = 8
C_BW = C_WIDTH // C_BLOCKS
C_CONV = 4
C_POW = 8.0
D_FF = ((8 * D_MODEL // 3 + 255) // 256) * 256
F_CONV = 3
DN_ALPHA = (2 * DEPTH) ** 0.25
DN_BETA = (8 * DEPTH) ** -0.25
LN_EPS = 1e-5
NORM_EPS = 1e-6

A_QK = A_HEADS * A_DK
A_VW = A_HEADS * A_DV
A_QKV = 2 * A_QK + A_VW
B_W = len(B_GROUPS) * B_HEADS * B_HD
B_OUT = B_HEADS * B_HD
IN_SIZES = (A_QKV, A_HEADS, A_HEADS, A_VW, 3 * B_W, C_WIDTH, C_WIDTH, 3 * D_MODEL)
IN_SPLITS = tuple(int(s) for s in np.cumsum(IN_SIZES)[:-1])
N_IN = sum(IN_SIZES)

kernel_name = 'hybrid_gdn_dilated_rglru_deepnorm_step'

F32 = jnp.float32


def layer_norm(x, g, b):
    xf = x.astype(F32)
    mu = jnp.mean(xf, -1, keepdims=True)
    var = jnp.mean(jnp.square(xf - mu), -1, keepdims=True)
    return ((xf - mu) * lax.rsqrt(var + LN_EPS) * g + b).astype(x.dtype)


def l2norm(x):
    return x * lax.rsqrt(jnp.sum(x * x, -1, keepdims=True) + NORM_EPS)


def causal_dwconv(x, buf, w, b=None):
    K, L = w.shape[0], x.shape[1]
    xx = jnp.concatenate([buf.astype(x.dtype), x], axis=1)
    y = xx[:, 0:L] * w[0]
    for j in range(1, K):
        y = y + xx[:, j:j + L] * w[j]
    if b is not None:
        y = y + b
    return y, xx[:, L:]


def rope_partial(x, pos):
    half = B_ROT // 2
    inv = ROPE_THETA ** (-jnp.arange(half, dtype=F32) / half)
    ang = pos.astype(F32)[:, None] * inv
    shp = (1, pos.shape[0]) + (1,) * (x.ndim - 3) + (half,)
    cos, sin = jnp.cos(ang).reshape(shp), jnp.sin(ang).reshape(shp)
    x1 = x[..., :half].astype(F32)
    x2 = x[..., half:B_ROT].astype(F32)
    rot = jnp.concatenate([x1 * cos - x2 * sin, x2 * cos + x1 * sin], -1).astype(x.dtype)
    return jnp.concatenate([rot, x[..., B_ROT:]], -1)


def chunk_gated_delta(q, k, v, g, beta, S0):
    Bn, L, H, _ = q.shape
    Dv = v.shape[-1]
    C = A_CHUNK
    n = -(-L // C)
    P = n * C - L

    def chunks(t):
        t = jnp.pad(t, ((0, 0), (0, P)) + ((0, 0),) * (t.ndim - 2))
        t = t.reshape((Bn, n, C) + t.shape[2:])
        return jnp.moveaxis(t, 3, 1)

    q, k, v, g, beta = (chunks(t) for t in (q, k, v, g, beta))
    g = jnp.cumsum(g, axis=-1)
    tril = jnp.tril(jnp.ones((C, C), bool))
    strict = jnp.tril(jnp.ones((C, C), bool), -1)
    decay = jnp.where(tril, jnp.exp(jnp.where(tril, g[..., :, None] - g[..., None, :], 0.0)), 0.0)
    kb = k * beta[..., None]
    A = jnp.where(strict, jnp.einsum('bhncd,bhnjd->bhncj', kb, k) * decay, 0.0)
    eye = jnp.eye(C, dtype=F32)
    T = lax.linalg.triangular_solve(eye + A, jnp.broadcast_to(eye, A.shape),
                                    left_side=True, lower=True, unit_diagonal=True)
    u = T @ (v * beta[..., None])
    w = T @ (kb * jnp.exp(g)[..., None])
    qk = jnp.where(tril, jnp.einsum('bhncd,bhnjd->bhncj', q, k) * decay, 0.0)
    gl = g[..., -1]
    k_dec = k * jnp.exp(gl[..., None] - g)[..., None]
    q_dec = q * jnp.exp(g)[..., None]

    def step(S, xs):
        q_i, k_i, u_i, w_i, a_i, gl_i = xs
        v_new = u_i - w_i @ S
        o = q_i @ S + a_i @ v_new
        S = S * jnp.exp(gl_i)[..., None, None] + jnp.swapaxes(k_i, -1, -2) @ v_new
        return S, o

    xs = tuple(jnp.moveaxis(t, 2, 0) for t in (q_dec, k_dec, u, w, qk, gl))
    S, o = lax.scan(step, S0, xs)
    o = jnp.moveaxis(jnp.moveaxis(o, 0, 2), 1, 3).reshape(Bn, n * C, H, Dv)[:, :L]
    return o, S


def gated_delta_branch(qkv, b_logit, a_logit, z, conv_buf, S0, conv_w, A_log, dt_bias, norm_w):
    Bn, L, _ = qkv.shape
    y, new_buf = causal_dwconv(qkv, conv_buf, conv_w)
    y = jax.nn.silu(y.astype(F32))
    q, k, v = jnp.split(y, [A_QK, 2 * A_QK], axis=-1)
    q = l2norm(q.reshape(Bn, L, A_HEADS, A_DK)) * (A_DK ** -0.5)
    k = l2norm(k.reshape(Bn, L, A_HEADS, A_DK))
    v = v.reshape(Bn, L, A_HEADS, A_DV)
    beta = jax.nn.sigmoid(b_logit.astype(F32))
    g = -jnp.exp(A_log.astype(F32)) * jax.nn.softplus(a_logit.astype(F32) + dt_bias.astype(F32))
    o, S = chunk_gated_delta(q, k, v, g, beta, S0.astype(F32))
    o = o * lax.rsqrt(jnp.mean(o * o, -1, keepdims=True) + NORM_EPS) * norm_w.astype(F32)
    o = o * jax.nn.silu(z.astype(F32).reshape(Bn, L, A_HEADS, A_DV))
    return o.reshape(Bn, L, A_VW).astype(qkv.dtype), new_buf, S.astype(qkv.dtype)


def dilated_prompt(q, k, v, window, dil):
    Bn, S, H, D = q.shape
    nk = window // dil
    n = S // dil
    nblk = -(-n // nk)
    npad = nblk * nk - n

    def strided(t):
        t = t.reshape(Bn, n, dil, H, D).transpose(0, 2, 1, 3, 4).reshape(Bn * dil, n, H, D)
        t = jnp.pad(t.astype(F32), ((0, 0), (0, npad), (0, 0), (0, 0)))
        return t.reshape(Bn * dil, nblk, nk, H, D)

    qs, ks, vs = strided(q), strided(k), strided(v)
    prev = lambda t: jnp.pad(t, ((0, 0), (1, 0), (0, 0), (0, 0), (0, 0)))[:, :-1]
    kk = jnp.concatenate([prev(ks), ks], axis=2)
    vv = jnp.concatenate([prev(vs), vs], axis=2)
    s = jnp.einsum('bnqhd,bnkhd->bnhqk', qs, kk) * (D ** -0.5)
    qi = jnp.arange(nk)[:, None]
    kj = jnp.arange(2 * nk)[None, :]
    band = (kj >= qi) & (kj <= qi + nk)
    first = (jnp.arange(nblk) == 0)[:, None, None]
    valid = band[None] & ~(first & (kj < nk)[None])
    s = jnp.where(valid[None, :, None], s, -jnp.inf)
    lse = jax.nn.logsumexp(s, axis=-1)
    p = jnp.exp(s - lse[..., None])
    o = jnp.einsum('bnhqk,bnkhd->bnqhd', p, vv).reshape(Bn * dil, nblk * nk, H, D)
    lse = jnp.swapaxes(lse, 2, 3).reshape(Bn * dil, nblk * nk, H)

    def unstride(t):
        t = t[:, :n]
        t = t.reshape((Bn, dil, n) + t.shape[2:])
        return jnp.swapaxes(t, 1, 2).reshape((Bn, S) + t.shape[3:])

    return unstride(o), unstride(lse)


def dilated_sample(q, kv_new, buf, window, dil):
    Lb, T = buf.shape[1], q.shape[1]
    kv = jnp.concatenate([buf.astype(kv_new.dtype), kv_new], axis=1)
    m = jnp.arange(window // dil + 1)
    idx = Lb + jnp.arange(T)[:, None] - dil * m[None, :]
    valid = idx >= 0
    kvg = kv[:, jnp.maximum(idx, 0)].astype(F32)
    s = jnp.einsum('bthd,btmhd->bthm', q.astype(F32), kvg[:, :, :, 0]) * (B_HD ** -0.5)
    s = jnp.where(valid[None, :, None, :], s, -jnp.inf)
    lse = jax.nn.logsumexp(s, axis=-1)
    p = jnp.exp(s - lse[..., None])
    o = jnp.einsum('bthm,btmhd->bthd', p, kvg[:, :, :, 1])
    return o, lse


def dilated_branch(qkv_b, pos, bufs):
    Bn, L, _ = qkv_b.shape
    G = len(B_GROUPS)
    qb, kb, vb = jnp.split(qkv_b, 3, axis=-1)
    shp = (Bn, L, G, B_HEADS, B_HD)
    q = rope_partial(qb.reshape(shp), pos)
    k = rope_partial(kb.reshape(shp), pos)
    v = vb.reshape(shp)
    outs, lses, rows = [], [], []
    for gi, (win, dil) in enumerate(B_GROUPS):
        kv = jnp.stack([k[:, :, gi], v[:, :, gi]], axis=2)
        if bufs is None:
            o, lse = dilated_prompt(q[:, :, gi], k[:, :, gi], v[:, :, gi], win, dil)
            rows.append(kv[:, L - min(win, L):])
        else:
            o, lse = dilated_sample(q[:, :, gi], kv, bufs[gi], win, dil)
            rows.append(kv)
        outs.append(o)
        lses.append(lse)
    wts = jax.nn.softmax(jnp.stack(lses, 0), axis=0)
    o = jnp.sum(wts[..., None] * jnp.stack(outs, 0), axis=0)
    return o.reshape(Bn, L, B_OUT).astype(qkv_b.dtype), rows


def _lin_comb(l, r):
    return (l[0] * r[0], r[0] * l[1] + r[1])


def rglru_branch(xc, gc, conv_buf, h0, conv_w, conv_b, w_r, b_r, w_i, b_i, lam):
    Bn, L, _ = xc.shape
    y, new_buf = causal_dwconv(xc, conv_buf, conv_w, conv_b)
    yf = y.astype(F32)
    yb = yf.reshape(Bn, L, C_BLOCKS, C_BW)
    r = jax.nn.sigmoid(jnp.einsum('blnc,ncd->blnd', yb, w_r.astype(F32)).reshape(Bn, L, C_WIDTH) + b_r)
    i = jax.nn.sigmoid(jnp.einsum('blnc,ncd->blnd', yb, w_i.astype(F32)).reshape(Bn, L, C_WIDTH) + b_i)
    log_a = -C_POW * r * jax.nn.softplus(-lam.astype(F32))
    a = jnp.exp(log_a)
    bx = jnp.sqrt(-jnp.expm1(2.0 * log_a)) * (i * yf)
    a_cum, b_cum = lax.associative_scan(_lin_comb, (a, bx), axis=1)
    h = a_cum * h0.astype(F32)[:, None] + b_cum
    out = h * jax.nn.gelu(gc.astype(F32))
    return out.astype(xc.dtype), new_buf, h[:, -1].astype(xc.dtype)


def conv_ffn(x, buf, w_up, conv_w, conv_b, w_down):
    gate, up = jnp.split(x @ w_up, 2, axis=-1)
    gate, new_buf = causal_dwconv(gate, buf, conv_w, conv_b)
    return (jax.nn.silu(gate) * up) @ w_down, new_buf


def trunk_layer(x, pos, a_conv, a_S, b_bufs, c_conv, c_h, f_conv, p):
    proj = x @ p['w_in']
    qkv_a, b_a, a_a, z_a, qkv_b, x_c, g_c, gates = jnp.split(proj, IN_SPLITS, axis=-1)
    o_a, a_conv_n, a_S_n = gated_delta_branch(qkv_a, b_a, a_a, z_a, a_conv, a_S, p['a_conv_w'],
                                              p['a_A_log'], p['a_dt_bias'], p['a_norm_w'])
    o_b, b_rows = dilated_branch(qkv_b, pos, b_bufs)
    o_c, c_conv_n, c_h_n = rglru_branch(x_c, g_c, c_conv, c_h, p['c_conv_w'], p['c_conv_b'], p['c_w_r'],
                                        p['c_b_r'], p['c_w_i'], p['c_b_i'], p['c_lam'])
    s_a, s_b, s_c = jnp.split(jax.nn.sigmoid(gates.astype(F32)), 3, axis=-1)
    merged = s_a * (o_a @ p['w_pa']) + s_b * (o_b @ p['w_pb']) + s_c * (o_c @ p['w_pc'])
    mix = merged.astype(x.dtype) @ p['w_o']
    x = layer_norm(DN_ALPHA * x + mix, p['ln1_g'], p['ln1_b'])
    f, f_conv_n = conv_ffn(x, f_conv, p['f_up'], p['f_conv_w'], p['f_conv_b'], p['f_down'])
    x = layer_norm(DN_ALPHA * x + f, p['ln2_g'], p['ln2_b'])
    return x, (a_conv_n, a_S_n, b_rows[0], b_rows[1], b_rows[2], c_conv_n, c_h_n, f_conv_n)


def setup_inputs(seed: int = 0) -> dict:
    key = jax.random.key(seed)
    keys = jax.random.split(key, 40)
    cnt = [0]

    def nxt():
        cnt[0] += 1
        return keys[cnt[0] - 1]

    def nrm(shape, scale=1.0):
        return jax.random.normal(nxt(), shape, F32) * scale

    def unif(shape, lo, hi):
        return jax.random.uniform(nxt(), shape, F32, lo, hi)

    L = DEPTH
    lb = [min(w, PAST_LEN) for w, _ in B_GROUPS]
    dt = jnp.exp(unif((L, A_HEADS), math.log(1e-3), math.log(1e-1)))
    u = unif((L, C_WIDTH), 0.9, 0.999) ** (1.0 / C_POW)
    a_A = unif((L, A_HEADS), 1.0, 16.0)
    return {
        'x_prompt': nrm((BATCH, SEQ, D_MODEL)),
        'x_sample': nrm((DEC_BATCH, DEC_SEQ, D_MODEL)),
        'state_a_conv': nrm((L, DEC_BATCH, A_CONV - 1, A_QKV)),
        'state_a_rec': nrm((L, DEC_BATCH, A_HEADS, A_DK, A_DV), A_DK ** -0.5),
        'cache_b_w128': nrm((L, DEC_BATCH, lb[0], 2, B_HEADS, B_HD)),
        'cache_b_w512': nrm((L, DEC_BATCH, lb[1], 2, B_HEADS, B_HD)),
        'cache_b_w2048': nrm((L, DEC_BATCH, lb[2], 2, B_HEADS, B_HD)),
        'state_c_conv': nrm((L, DEC_BATCH, C_CONV - 1, C_WIDTH)),
        'state_c_h': nrm((L, DEC_BATCH, C_WIDTH), 0.5),
        'state_f_conv': nrm((L, DEC_BATCH, F_CONV - 1, D_FF)),
        'ln_in_g': 1.0 + nrm((D_MODEL,), 0.02),
        'ln_in_b': nrm((D_MODEL,), 0.02),
        'w_in': nrm((L, D_MODEL, N_IN), D_MODEL ** -0.5),
        'a_conv_w': nrm((L, A_CONV, A_QKV), A_CONV ** -0.5),
        'a_A_log': jnp.log(a_A),
        'a_dt_bias': dt + jnp.log(-jnp.expm1(-dt)),
        'a_norm_w': 1.0 + nrm((L, A_DV), 0.02),
        'c_conv_w': nrm((L, C_CONV, C_WIDTH), C_CONV ** -0.5),
        'c_conv_b': nrm((L, C_WIDTH), 0.02),
        'c_w_r': nrm((L, C_BLOCKS, C_BW, C_BW), C_BW ** -0.5),
        'c_b_r': nrm((L, C_WIDTH), 0.02),
        'c_w_i': nrm((L, C_BLOCKS, C_BW, C_BW), C_BW ** -0.5),
        'c_b_i': nrm((L, C_WIDTH), 0.02),
        'c_lam': jnp.log(u) - jnp.log1p(-u),
        'w_pa': nrm((L, A_VW, D_MODEL), A_VW ** -0.5),
        'w_pb': nrm((L, B_OUT, D_MODEL), B_OUT ** -0.5),
        'w_pc': nrm((L, C_WIDTH, D_MODEL), C_WIDTH ** -0.5),
        'w_o': nrm((L, D_MODEL, D_MODEL), D_MODEL ** -0.5 * DN_BETA),
        'ln1_g': 1.0 + nrm((L, D_MODEL), 0.02),
        'ln1_b': nrm((L, D_MODEL), 0.02),
        'f_up': nrm((L, D_MODEL, 2 * D_FF), D_MODEL ** -0.5),
        'f_conv_w': nrm((L, F_CONV, D_FF), F_CONV ** -0.5),
        'f_conv_b': nrm((L, D_FF), 0.02),
        'f_down': nrm((L, D_FF, D_MODEL), D_FF ** -0.5 * DN_BETA),
        'ln2_g': 1.0 + nrm((L, D_MODEL), 0.02),
        'ln2_b': nrm((L, D_MODEL), 0.02),
    }


def reference(x_prompt, x_sample, state_a_conv, state_a_rec, cache_b_w128, cache_b_w512, cache_b_w2048,
              state_c_conv, state_c_h, state_f_conv, ln_in_g, ln_in_b, w_in, a_conv_w, a_A_log, a_dt_bias,
              a_norm_w, c_conv_w, c_conv_b, c_w_r, c_b_r, c_w_i, c_b_i, c_lam, w_pa, w_pb, w_pc, w_o,
              ln1_g, ln1_b, f_up, f_conv_w, f_conv_b, f_down, ln2_g, ln2_b):
    Bp, Sp = x_prompt.shape[0], x_prompt.shape[1]
    dt = x_prompt.dtype
    pos_p = jnp.arange(Sp)
    pos_s = PAST_LEN + jnp.arange(x_sample.shape[1])
    zero_a_conv = jnp.zeros((Bp, A_CONV - 1, A_QKV), dt)
    zero_a_S = jnp.zeros((Bp, A_HEADS, A_DK, A_DV), dt)
    zero_c_conv = jnp.zeros((Bp, C_CONV - 1, C_WIDTH), dt)
    zero_c_h = jnp.zeros((Bp, C_WIDTH), dt)
    zero_f_conv = jnp.zeros((Bp, F_CONV - 1, D_FF), dt)

    hp = layer_norm(x_prompt, ln_in_g, ln_in_b)
    hs = layer_norm(x_sample, ln_in_g, ln_in_b)
    new_p, new_s = [], []
    for l in range(DEPTH):
        p = {'w_in': w_in[l], 'a_conv_w': a_conv_w[l], 'a_A_log': a_A_log[l], 'a_dt_bias': a_dt_bias[l],
             'a_norm_w': a_norm_w[l], 'c_conv_w': c_conv_w[l], 'c_conv_b': c_conv_b[l], 'c_w_r': c_w_r[l],
             'c_b_r': c_b_r[l], 'c_w_i': c_w_i[l], 'c_b_i': c_b_i[l], 'c_lam': c_lam[l], 'w_pa': w_pa[l],
             'w_pb': w_pb[l], 'w_pc': w_pc[l], 'w_o': w_o[l], 'ln1_g': ln1_g[l], 'ln1_b': ln1_b[l],
             'f_up': f_up[l], 'f_conv_w': f_conv_w[l], 'f_conv_b': f_conv_b[l], 'f_down': f_down[l],
             'ln2_g': ln2_g[l], 'ln2_b': ln2_b[l]}
        hp, st_p = trunk_layer(hp, pos_p, zero_a_conv, zero_a_S, None, zero_c_conv, zero_c_h, zero_f_conv, p)
        hs, st_s = trunk_layer(hs, pos_s, state_a_conv[l], state_a_rec[l],
                               (cache_b_w128[l], cache_b_w512[l], cache_b_w2048[l]),
                               state_c_conv[l], state_c_h[l], state_f_conv[l], p)
        new_p.append(st_p)
        new_s.append(st_s)

    (a_conv_p, a_rec_p, b128_p, b512_p, b2048_p, c_conv_p, c_h_p, f_conv_p) = [jnp.stack(s, 0) for s in zip(*new_p)]
    (a_conv_s, a_rec_s, b128_s, b512_s, b2048_s, c_conv_s, c_h_s, f_conv_s) = [jnp.stack(s, 0) for s in zip(*new_s)]
    return (hp, hs,
            a_conv_p, a_rec_p, b128_p, b512_p, b2048_p, c_conv_p, c_h_p, f_conv_p,
            a_conv_s, a_rec_s, b128_s, b512_s, b2048_s, c_conv_s, c_h_s, f_conv_s)
```

```python
import functools
import math

import numpy as np
import jax
import jax.numpy as jnp
from jax import lax
from jax.experimental import pallas as pl
from jax.experimental.pallas import tpu as pltpu

F32 = jnp.float32
BF16 = jnp.bfloat16

D_MODEL = 1024
DEPTH = 4
PAST_LEN = 2048
A_HEADS, A_DK, A_DV, A_CONV, A_CHUNK = 4, 128, 128, 4, 64
B_GROUPS = ((128, 1), (512, 4), (2048, 16))
B_HEADS, B_HD = 4, 64
B_ROT = B_HD // 4
ROPE_THETA = 500000.0
B_NK = 128
C_WIDTH, C_BLOCKS, C_CONV, C_POW = 512, 8, 4, 8.0
C_BW = C_WIDTH // C_BLOCKS
D_FF = ((8 * D_MODEL // 3 + 255) // 256) * 256
F_CONV = 3
FF_CHUNK = 1408
DN_ALPHA = (2 * DEPTH) ** 0.25
LN_EPS = 1e-5
NORM_EPS = 1e-6

A_QK = A_HEADS * A_DK
A_VW = A_HEADS * A_DV
A_QKV = 2 * A_QK + A_VW
B_GW = B_HEADS * B_HD
B_W = len(B_GROUPS) * B_GW

OFF_GATES = 0
OFF_XC = 3072
OFF_GC = 3584
OFF_Z = 4096
OFF_QKVA = 4608
OFF_QB = 6144
OFF_BA = 8448
N_PROJ = 8704
PROJ_TN = 512

VMEM_LIMIT = 56 * 1024 * 1024


def _cparams(sem):
    return pltpu.CompilerParams(dimension_semantics=sem, vmem_limit_bytes=VMEM_LIMIT)


def _sigmoid(x):
    return 1.0 / (1.0 + jnp.exp(-x))


def _silu(x):
    return x * _sigmoid(x)


def _softplus(x):
    return jnp.maximum(x, 0.0) + jnp.log1p(jnp.exp(-jnp.abs(x)))


def _layer_norm(x, g, b):
    mu = jnp.mean(x, axis=-1, keepdims=True)
    xc = x - mu
    var = jnp.mean(xc * xc, axis=-1, keepdims=True)
    return xc * lax.rsqrt(var + LN_EPS) * g + b


def _dot(a, b):
    return jnp.dot(a.astype(BF16), b.astype(BF16), preferred_element_type=F32)


def _dot_nt(a, b):
    return lax.dot_general(a.astype(BF16), b.astype(BF16), (((1,), (1,)), ((), ())),
                           preferred_element_type=F32)


def _dot_tn(a, b):
    return lax.dot_general(a.astype(BF16), b.astype(BF16), (((0,), (0,)), ((), ())),
                           preferred_element_type=F32)


def _split2(a):
    hi = a.astype(BF16)
    lo = (a - hi.astype(F32)).astype(BF16)
    return hi, lo


def _dot3(a, b):
    ah, al = _split2(a)
    bh, bl = _split2(b)
    d = lambda x, y: jnp.dot(x, y, preferred_element_type=F32)
    return d(ah, bh) + (d(ah, bl) + d(al, bh))


def _ln_kernel(x_ref, g_ref, b_ref, o_ref):
    o_ref[...] = _layer_norm(x_ref[...], g_ref[...], b_ref[...])


def _ln_call(x, g, b, tm):
    rows = x.shape[0]
    return pl.pallas_call(
        _ln_kernel,
        grid=(rows // tm,),
        in_specs=[pl.BlockSpec((tm, D_MODEL), lambda i: (i, 0)),
                  pl.BlockSpec((1, D_MODEL), lambda i: (0, 0)),
                  pl.BlockSpec((1, D_MODEL), lambda i: (0, 0))],
        out_specs=pl.BlockSpec((tm, D_MODEL), lambda i: (i, 0)),
        out_shape=jax.ShapeDtypeStruct((rows, D_MODEL), F32),
        compiler_params=_cparams(("parallel",)),
        name="entry_ln",
    )(x, g, b)


def _proj_kernel(x_ref, w_ref, o_ref, xb_ref):
    @pl.when(pl.program_id(1) == 0)
    def _():
        xb_ref[...] = x_ref[...].astype(BF16)

    o_ref[...] = jnp.dot(xb_ref[...], w_ref[...], preferred_element_type=F32)


def _proj_call(x, w, tm):
    rows = x.shape[0]
    return pl.pallas_call(
        _proj_kernel,
        grid=(rows // tm, N_PROJ // PROJ_TN),
        in_specs=[pl.BlockSpec((tm, D_MODEL), lambda i, j: (i, 0)),
                  pl.BlockSpec((D_MODEL, PROJ_TN), lambda i, j: (0, j))],
        out_specs=pl.BlockSpec((tm, PROJ_TN), lambda i, j: (i, j)),
        out_shape=jax.ShapeDtypeStruct((rows, N_PROJ), F32),
        scratch_shapes=[pltpu.VMEM((tm, D_MODEL), BF16)],
        compiler_params=_cparams(("parallel", "arbitrary")),
        name="in_proj",
    )(x, w)


def _gdn_prompt_kernel(x_ref, z_ref, ba_ref, cw_ref, gp_ref, nw_ref,
                       o_ref, sfin_ref, xx_ref, s_ref):
    c = pl.program_id(1)
    C = A_CHUNK

    @pl.when(c == 0)
    def _():
        xx_ref[0:8, :] = jnp.zeros((8, A_QKV), F32)
        s_ref[...] = jnp.zeros(s_ref.shape, F32)

    x = x_ref[0]
    xx_ref[8:8 + C, :] = x
    y = xx_ref[5:5 + C, :] * cw_ref[0:1, :]
    for j in range(1, A_CONV):
        y = y + xx_ref[5 + j:5 + j + C, :] * cw_ref[j:j + 1, :]
    xx_ref[0:8, :] = x[C - 8:C, :]
    y = _silu(y)

    ba = ba_ref[0]
    beta_all = _sigmoid(ba)
    g_all = -jnp.exp(gp_ref[0:1, :]) * _softplus(ba + gp_ref[1:2, :])
    ri = lax.broadcasted_iota(jnp.int32, (C, C), 0)
    ci = lax.broadcasted_iota(jnp.int32, (C, C), 1)
    tril = ri >= ci
    strict = ri > ci
    eye = (ri == ci).astype(F32)
    gc = _dot3(tril.astype(F32), g_all)
    gct = jnp.concatenate([gc, jnp.zeros((128 - C, 128), F32)], axis=0).T

    z = z_ref[0]
    nw = nw_ref[...]
    for h in range(A_HEADS):
        q = y[:, h * A_DK:(h + 1) * A_DK]
        k = y[:, A_QK + h * A_DK:A_QK + (h + 1) * A_DK]
        v = y[:, 2 * A_QK + h * A_DV:2 * A_QK + (h + 1) * A_DV]
        q = q * lax.rsqrt(jnp.sum(q * q, axis=-1, keepdims=True) + NORM_EPS) * (A_DK ** -0.5)
        k = k * lax.rsqrt(jnp.sum(k * k, axis=-1, keepdims=True) + NORM_EPS)
        beta = beta_all[:, h:h + 1]
        gcol = gc[:, 4 + h:5 + h]
        grow = gct[4 + h:5 + h, 0:C]
        decay = jnp.where(tril, jnp.exp(jnp.where(tril, gcol - grow, 0.0)), 0.0)
        kb = k * beta
        a = jnp.where(strict, _dot_nt(kb, k) * decay, 0.0)
        m = -a
        t = eye + m
        for _ in range(5):
            m = _dot3(m, m)
            t = t + _dot3(t, m)
        eg = jnp.exp(gcol)
        u = _dot(t, v * beta)
        w = _dot(t, kb * eg)
        qk = jnp.where(tril, _dot_nt(q, k) * decay, 0.0)
        glast = gcol[C - 1:C, :]
        k_dec = k * jnp.exp(glast - gcol)
        q_dec = q * eg
        s = s_ref[h]
        v_new = u - _dot(w, s)
        o = _dot(q_dec, s) + _dot(qk, v_new)
        s_new = s * jnp.exp(glast) + _dot_tn(k_dec, v_new)
        s_ref[h] = s_new
        sfin_ref[0, h] = s_new
        o = o * lax.rsqrt(jnp.mean(o * o, axis=-1, keepdims=True) + NORM_EPS) * nw
        o = o * _silu(z[:, h * A_DV:(h + 1) * A_DV])
        o_ref[0, :, h * A_DV:(h + 1) * A_DV] = o


def _gdn_prompt_call(proj3, cw, gp, nw):
    B, S, _ = proj3.shape
    C = A_CHUNK
    return pl.pallas_call(
        _gdn_prompt_kernel,
        grid=(B, S // C),
        in_specs=[pl.BlockSpec((1, C, A_QKV), lambda b, c: (b, c, OFF_QKVA // A_QKV)),
                  pl.BlockSpec((1, C, A_VW), lambda b, c: (b, c, OFF_Z // A_VW)),
                  pl.BlockSpec((1, C, 128), lambda b, c: (b, c, OFF_BA // 128)),
                  pl.BlockSpec((A_CONV, A_QKV), lambda b, c: (0, 0)),
                  pl.BlockSpec((2, 128), lambda b, c: (0, 0)),
                  pl.BlockSpec((1, A_DV), lambda b, c: (0, 0))],
        out_specs=[pl.BlockSpec((1, C, A_VW), lambda b, c: (b, c, 0)),
                   pl.BlockSpec((1, A_HEADS, A_DK, A_DV), lambda b, c: (b, 0, 0, 0))],
        out_shape=[jax.ShapeDtypeStruct((B, S, A_VW), F32),
                   jax.ShapeDtypeStruct((B, A_HEADS, A_DK, A_DV), F32)],
        scratch_shapes=[pltpu.VMEM((8 + C, A_QKV), F32),
                        pltpu.VMEM((A_HEADS, A_DK, A_DV), F32)],
        compiler_params=_cparams(("parallel", "arbitrary")),
        name="gdn_prompt",
    )(proj3, proj3, proj3, cw, gp, nw)


GS_BB = 8


def _gdn_sample_kernel(x_ref, z_ref, ba_ref, st_ref, s0_ref, cw_ref, gp_ref, nw_ref,
                       o_ref, snew_ref, oscr_ref):
    T = x_ref.shape[0]
    bb = GS_BB
    xx = [st_ref[j] for j in range(A_CONV - 1)] + [x_ref[t] for t in range(T)]
    vb = [[None] * A_HEADS for _ in range(T)]
    rows = [[] for _ in range(A_HEADS)]
    kinds = [[[None] * T for _ in range(4)] for _ in range(A_HEADS)]
    for t in range(T):
        y = xx[t] * cw_ref[0:1, :]
        for j in range(1, A_CONV):
            y = y + xx[t + j] * cw_ref[j:j + 1, :]
        y = _silu(y)
        ba = ba_ref[t]
        beta_all = _sigmoid(ba)
        eg_all = jnp.exp(-jnp.exp(gp_ref[0:1, :]) * _softplus(ba + gp_ref[1:2, :]))
        for h in range(A_HEADS):
            q = y[:, h * A_DK:(h + 1) * A_DK]
            k = y[:, A_QK + h * A_DK:A_QK + (h + 1) * A_DK]
            v = y[:, 2 * A_QK + h * A_DV:2 * A_QK + (h + 1) * A_DV]
            q = q * lax.rsqrt(jnp.sum(q * q, axis=-1, keepdims=True) + NORM_EPS) * (A_DK ** -0.5)
            k = k * lax.rsqrt(jnp.sum(k * k, axis=-1, keepdims=True) + NORM_EPS)
            beta = beta_all[:, h:h + 1]
            eg = eg_all[:, 4 + h:5 + h]
            kinds[h][0][t] = q
            kinds[h][1][t] = k
            kinds[h][2][t] = k * (beta * eg)
            kinds[h][3][t] = jnp.broadcast_to(eg, (bb, A_DK))
            vb[t][h] = v * beta
    for h in range(A_HEADS):
        xm = jnp.concatenate([kinds[h][kd][t] for kd in range(4) for t in range(T)], axis=0)
        xt = xm.T
        for b in range(bb):
            s = s0_ref[b, h]
            for t in range(T):
                col = lambda kd: xt[:, kd * T * bb + t * bb + b:kd * T * bb + t * bb + b + 1]
                ks = jnp.sum(s * col(2), axis=0, keepdims=True)
                r = vb[t][h][b:b + 1, :] - ks
                s = s * col(3) + col(1) * r
                o = jnp.sum(s * col(0), axis=0, keepdims=True)
                oscr_ref[t, b:b + 1, h * A_DV:(h + 1) * A_DV] = o
            snew_ref[b, h] = s
    nw = nw_ref[...]
    for t in range(T):
        z = z_ref[t]
        for h in range(A_HEADS):
            o = oscr_ref[t, :, h * A_DV:(h + 1) * A_DV]
            o = o * lax.rsqrt(jnp.mean(o * o, axis=-1, keepdims=True) + NORM_EPS) * nw
            o_ref[t, :, h * A_DV:(h + 1) * A_DV] = o * _silu(z[:, h * A_DV:(h + 1) * A_DV])


def _gdn_sample_call(proj3, st, s0, cw, gp, nw):
    T, B, _ = proj3.shape
    bb = GS_BB
    return pl.pallas_call(
        _gdn_sample_kernel,
        grid=(B // bb,),
        in_specs=[pl.BlockSpec((T, bb, A_QKV), lambda i: (0, i, OFF_QKVA // A_QKV)),
                  pl.BlockSpec((T, bb, A_VW), lambda i: (0, i, OFF_Z // A_VW)),
                  pl.BlockSpec((T, bb, 128), lambda i: (0, i, OFF_BA // 128)),
                  pl.BlockSpec((A_CONV - 1, bb, A_QKV), lambda i: (0, i, 0)),
                  pl.BlockSpec((bb, A_HEADS, A_DK, A_DV), lambda i: (i, 0, 0, 0)),
                  pl.BlockSpec((A_CONV, A_QKV), lambda i: (0, 0)),
                  pl.BlockSpec((2, 128), lambda i: (0, 0)),
                  pl.BlockSpec((1, A_DV), lambda i: (0, 0))],
        out_specs=[pl.BlockSpec((T, bb, A_VW), lambda i: (0, i, 0)),
                   pl.BlockSpec((bb, A_HEADS, A_DK, A_DV), lambda i: (i, 0, 0, 0))],
        out_shape=[jax.ShapeDtypeStruct((T, B, A_VW), F32),
                   jax.ShapeDtypeStruct(s0.shape, F32)],
        scratch_shapes=[pltpu.VMEM((T, bb, A_VW), F32)],
        compiler_params=_cparams(("parallel",)),
        name="gdn_sample",
    )(proj3, proj3, proj3, st, s0, cw, gp, nw)


def _rope_kernel(q_ref, k_ref, v_ref, c_ref, s1_ref, s2_ref, qo_ref, kv0_ref, kv1_ref, kv2_ref):
    reps = B_W // 128
    c = jnp.concatenate([c_ref[0]] * reps, axis=1)
    s1 = jnp.concatenate([s1_ref[0]] * reps, axis=1)
    s2 = jnp.concatenate([s2_ref[0]] * reps, axis=1)

    def rot(x):
        return x * c + pltpu.roll(x, B_W - B_ROT // 2, 1) * s1 + pltpu.roll(x, B_ROT // 2, 1) * s2

    qo_ref[0] = (rot(q_ref[0]) * (B_HD ** -0.5)).astype(BF16)
    k = rot(k_ref[0])
    v = v_ref[0]
    for g, kv_ref in enumerate((kv0_ref, kv1_ref, kv2_ref)):
        kv_ref[0, :, 0:B_GW] = k[:, g * B_GW:(g + 1) * B_GW]
        kv_ref[0, :, B_GW:2 * B_GW] = v[:, g * B_GW:(g + 1) * B_GW]


def _rope_call(proj3, tabs, tm):
    nseq, rows, _ = proj3.shape
    qb = OFF_QB // B_W
    tab_spec = pl.BlockSpec((1, tm, 128), lambda n, i: (0, i, 0))
    kv_shape = jax.ShapeDtypeStruct((nseq, rows, 2 * B_GW), F32)
    kv_spec = pl.BlockSpec((1, tm, 2 * B_GW), lambda n, i: (n, i, 0))
    return pl.pallas_call(
        _rope_kernel,
        grid=(nseq, rows // tm),
        in_specs=[pl.BlockSpec((1, tm, B_W), lambda n, i: (n, i, qb)),
                  pl.BlockSpec((1, tm, B_W), lambda n, i: (n, i, qb + 1)),
                  pl.BlockSpec((1, tm, B_W), lambda n, i: (n, i, qb + 2)),
                  tab_spec, tab_spec, tab_spec],
        out_specs=[pl.BlockSpec((1, tm, B_W), lambda n, i: (n, i, 0)), kv_spec, kv_spec, kv_spec],
        out_shape=[jax.ShapeDtypeStruct((nseq, rows, B_W), BF16), kv_shape, kv_shape, kv_shape],
        compiler_params=_cparams(("parallel", "parallel")),
        name="rope_regroup",
    )(proj3, proj3, proj3, *tabs)


def _rope_tables(pos):
    half = B_ROT // 2
    inv = ROPE_THETA ** (-jnp.arange(half, dtype=F32) / half)
    ang = pos.astype(F32)[:, None] * inv
    cos, sin = jnp.cos(ang), jnp.sin(ang)
    rows = pos.shape[0]
    one = jnp.ones((rows, B_HD - B_ROT), F32)
    zero = jnp.zeros((rows, half), F32)
    zrest = jnp.zeros((rows, B_HD - B_ROT), F32)
    c = jnp.concatenate([cos, cos, one], axis=1)
    s1 = jnp.concatenate([-sin, zero, zrest], axis=1)
    s2 = jnp.concatenate([zero, sin, zrest], axis=1)
    tile = lambda t: jnp.concatenate([t, t], axis=1)[None]
    return tile(c), tile(s1), tile(s2)


def _head_lane_mask(rows):
    lane = lax.broadcasted_iota(jnp.int32, (rows, B_GW), 1)
    return [(lane >= h * B_HD) & (lane < (h + 1) * B_HD) for h in range(B_HEADS)]


def _attn_prompt_kernel(q_ref, kvp_ref, kvc_ref, o_ref, l_ref):
    i = pl.program_id(2)
    nk = B_NK
    q = q_ref[0]
    kk = jnp.concatenate([kvp_ref[0, :, 0:B_GW], kvc_ref[0, :, 0:B_GW]], axis=0).astype(BF16)
    vv = jnp.concatenate([kvp_ref[0, :, B_GW:], kvc_ref[0, :, B_GW:]], axis=0).astype(BF16)
    qi = lax.broadcasted_iota(jnp.int32, (nk, 2 * nk), 0)
    kj = lax.broadcasted_iota(jnp.int32, (nk, 2 * nk), 1)
    valid = (kj >= qi) & (kj <= qi + nk) & ((kj >= nk) | (i > 0))
    masks = _head_lane_mask(nk)
    o_acc = jnp.zeros((nk, B_GW), F32)
    l_acc = jnp.zeros((nk, B_GW), F32)
    zq = jnp.zeros_like(q)
    for h in range(B_HEADS):
        s = _dot_nt(jnp.where(masks[h], q, zq), kk)
        s = jnp.where(valid, s, -jnp.inf)
        m = jnp.max(s, axis=-1, keepdims=True)
        p = jnp.exp(s - m)
        den = jnp.sum(p, axis=-1, keepdims=True)
        lse = m + jnp.log(den)
        oh = jnp.dot((p / den).astype(BF16), vv, preferred_element_type=F32)
        o_acc = jnp.where(masks[h], oh, o_acc)
        l_acc = jnp.where(masks[h], lse, l_acc)
    o_ref[0] = o_acc
    l_ref[0] = l_acc


def _attn_prompt_call(q, kv, g, dil):
    B, S, _ = q.shape
    n = S // dil
    nblk = n // B_NK
    qv = q.reshape(B, n, dil * B_W)
    kvv = kv.reshape(B, n, dil * 2 * B_GW)
    ngrp = B_W // B_GW
    out = jax.ShapeDtypeStruct((B, n, dil * B_GW), F32)
    o, l = pl.pallas_call(
        _attn_prompt_kernel,
        grid=(B, dil, nblk),
        in_specs=[pl.BlockSpec((1, B_NK, B_GW), lambda b, r, i: (b, i, r * ngrp + g)),
                  pl.BlockSpec((1, B_NK, 2 * B_GW), lambda b, r, i: (b, jnp.maximum(i - 1, 0), r)),
                  pl.BlockSpec((1, B_NK, 2 * B_GW), lambda b, r, i: (b, i, r))],
        out_specs=[pl.BlockSpec((1, B_NK, B_GW), lambda b, r, i: (b, i, r)),
                   pl.BlockSpec((1, B_NK, B_GW), lambda b, r, i: (b, i, r))],
        out_shape=[out, out],
        compiler_params=_cparams(("parallel", "parallel", "arbitrary")),
        name="attn_prompt_d%d" % dil,
    )(qv, kvv, kvv)
    return o.reshape(B * S, B_GW), l.reshape(B * S, B_GW)


AS_BB = 8


def _attn_sample_kernel(q_ref, kvn_ref, cache_ref, o_ref, l_ref, *, dil):
    T = q_ref.shape[0]
    bb = AS_BB
    nk = B_NK
    R = 8
    row = lax.broadcasted_iota(jnp.int32, (R, B_GW), 0)
    lane = lax.broadcasted_iota(jnp.int32, (R, B_GW), 1)
    hmask = (lane >= row * B_HD) & (lane < (row + 1) * B_HD)
    hmask_t = jnp.concatenate([hmask] * T, axis=0)
    kj = lax.broadcasted_iota(jnp.int32, (R, 2 * nk), 1)
    kj_t = lax.broadcasted_iota(jnp.int32, (R * T, 2 * nk), 1)
    qt_t = lax.broadcasted_iota(jnp.int32, (R * T, 2 * nk), 0) // R
    qs = [q_ref[t].astype(F32) for t in range(T)]
    kvn = [kvn_ref[t] for t in range(T)]

    def attend(qm, kext, vext, valid):
        s = jnp.where(valid, _dot_nt(qm, kext), -jnp.inf)
        m = jnp.max(s, axis=-1, keepdims=True)
        p = jnp.exp(s - m)
        den = jnp.sum(p, axis=-1, keepdims=True)
        return _dot(p / den, vext), m + jnp.log(den)

    for b in range(bb):
        new = jnp.concatenate([kvn[t][b:b + 1, :] for t in range(T)] + [jnp.zeros((nk - T, 2 * B_GW), F32)], axis=0)
        if dil == 1:
            kext = jnp.concatenate([cache_ref[b, :, 0:B_GW], new[:, 0:B_GW]], axis=0)
            vext = jnp.concatenate([cache_ref[b, :, B_GW:2 * B_GW], new[:, B_GW:]], axis=0)
            qm = jnp.concatenate([jnp.where(hmask, jnp.broadcast_to(qs[t][b:b + 1, :], (R, B_GW)), 0.0)
                                  for t in range(T)], axis=0)
            valid = ((kj_t < nk) & (kj_t >= qt_t)) | ((kj_t >= nk) & (kj_t - nk <= qt_t))
            o, lse = attend(qm, kext, vext, valid)
            om = jnp.where(hmask_t, o, 0.0)
            lm = jnp.where(hmask_t, lse, 0.0)
            for t in range(T):
                o_ref[t, b:b + 1, :] = jnp.sum(om[t * R:(t + 1) * R], axis=0, keepdims=True)
                l_ref[t, b:b + 1, :] = jnp.sum(lm[t * R:(t + 1) * R], axis=0, keepdims=True)
        else:
            for t in range(T):
                base = t * 2 * B_GW
                kext = jnp.concatenate([cache_ref[b, :, base:base + B_GW], new[:, 0:B_GW]], axis=0)
                vext = jnp.concatenate([cache_ref[b, :, base + B_GW:base + 2 * B_GW], new[:, B_GW:]], axis=0)
                qm = jnp.where(hmask, jnp.broadcast_to(qs[t][b:b + 1, :], (R, B_GW)), 0.0)
                valid = (kj < nk) | (kj - nk == t)
                o, lse = attend(qm, kext, vext, valid)
                o_ref[t, b:b + 1, :] = jnp.sum(jnp.where(hmask, o, 0.0), axis=0, keepdims=True)
                l_ref[t, b:b + 1, :] = jnp.sum(jnp.where(hmask, lse, 0.0), axis=0, keepdims=True)


def _attn_sample_call(q, kvn, cache, g, dil):
    T, B, _ = q.shape
    bb = AS_BB
    Lb = cache.shape[1]
    cv = cache.reshape(B, Lb // dil, dil * 2 * B_GW)
    cw = min(dil, T) * 2 * B_GW
    out = jax.ShapeDtypeStruct((T, B, B_GW), F32)
    return pl.pallas_call(
        functools.partial(_attn_sample_kernel, dil=dil),
        grid=(B // bb,),
        in_specs=[pl.BlockSpec((T, bb, B_GW), lambda i: (0, i, g)),
                  pl.BlockSpec((T, bb, 2 * B_GW), lambda i: (0, i, 0)),
                  pl.BlockSpec((bb, B_NK, cw), lambda i: (i, 0, 0))],
        out_specs=[pl.BlockSpec((T, bb, B_GW), lambda i: (0, i, 0)),
                   pl.BlockSpec((T, bb, B_GW), lambda i: (0, i, 0))],
        out_shape=[out, out],
        compiler_params=_cparams(("parallel",)),
        name="attn_sample_d%d" % dil,
    )(q, kvn, cv)


def _shift_rows(x, s, fill):
    n = x.shape[0]
    rolled = pltpu.roll(x, s, 0)
    r = lax.broadcasted_iota(jnp.int32, x.shape, 0)
    return jnp.where(r < s, fill, rolled)


def _rglru_kernel(x_ref, g_ref, prev_ref, h0_ref, cw_ref, cb_ref, wr_ref, br_ref, wi_ref, bi_ref, lam_ref,
                  o_ref, hl_ref, xx_ref, hc_ref, *, unit, hp):
    i = pl.program_id(1)
    tm = x_ref.shape[1]
    halo = (C_CONV - 1) * unit

    @pl.when(i == 0)
    def _():
        xx_ref[hp - halo:hp, :] = prev_ref[0]
        hc_ref[...] = h0_ref[0]

    x = x_ref[0]
    xx_ref[hp:hp + tm, :] = x
    y = cb_ref[...] + xx_ref[hp - halo:hp - halo + tm, :] * cw_ref[0:1, :]
    for j in range(1, C_CONV):
        y = y + xx_ref[hp - halo + j * unit:hp - halo + j * unit + tm, :] * cw_ref[j:j + 1, :]
    tail = xx_ref[hp + tm - halo:hp + tm, :]
    xx_ref[hp - halo:hp, :] = tail

    r = _sigmoid(_dot(y, wr_ref[...]) + br_ref[...])
    ig = _sigmoid(_dot(y, wi_ref[...]) + bi_ref[...])
    log_a = -C_POW * r * _softplus(-lam_ref[...])
    a = jnp.exp(log_a)
    th = jnp.tanh(log_a)
    bx = jnp.sqrt(-2.0 * th / (1.0 - th)) * (ig * y)
    hc = hc_ref[...]
    if unit == 1:
        rr = lax.broadcasted_iota(jnp.int32, bx.shape, 0)
        bx = jnp.where(rr < 1, a * hc + bx, bx)
    elif unit == tm:
        bx = a * hc + bx
    else:
        bx = jnp.concatenate([a[0:unit] * hc + bx[0:unit], bx[unit:]], axis=0)
    s = unit
    while s < tm:
        a_sh = _shift_rows(a, s, 1.0)
        b_sh = _shift_rows(bx, s, 0.0)
        bx = a * b_sh + bx
        a = a * a_sh
        s *= 2
    h = bx
    hl = h[tm - unit:tm, :]
    hc_ref[...] = hl
    hl_ref[0] = hl
    o_ref[0] = h * jax.nn.gelu(g_ref[0])


def _rglru_call(proj3, prev, h0, cw, cb, wr, br, wi, bi, lam, unit, tm):
    nseq, rows, _ = proj3.shape
    halo = (C_CONV - 1) * unit
    hp = -(-halo // 8) * 8
    assert unit == 1 or unit % 8 == 0
    vec = lambda: pl.BlockSpec((1, C_WIDTH), lambda n, i: (0, 0))
    return pl.pallas_call(
        functools.partial(_rglru_kernel, unit=unit, hp=hp),
        grid=(nseq, rows // tm),
        in_specs=[pl.BlockSpec((1, tm, C_WIDTH), lambda n, i: (n, i, OFF_XC // C_WIDTH)),
                  pl.BlockSpec((1, tm, C_WIDTH), lambda n, i: (n, i, OFF_GC // C_WIDTH)),
                  pl.BlockSpec((1, halo, C_WIDTH), lambda n, i: (n, 0, 0)),
                  pl.BlockSpec((1, unit, C_WIDTH), lambda n, i: (n, 0, 0)),
                  pl.BlockSpec((C_CONV, C_WIDTH), lambda n, i: (0, 0)), vec(),
                  pl.BlockSpec((C_WIDTH, C_WIDTH), lambda n, i: (0, 0)), vec(),
                  pl.BlockSpec((C_WIDTH, C_WIDTH), lambda n, i: (0, 0)), vec(), vec()],
        out_specs=[pl.BlockSpec((1, tm, C_WIDTH), lambda n, i: (n, i, 0)),
                   pl.BlockSpec((1, unit, C_WIDTH), lambda n, i: (n, 0, 0))],
        out_shape=[jax.ShapeDtypeStruct((nseq, rows, C_WIDTH), F32),
                   jax.ShapeDtypeStruct((nseq, unit, C_WIDTH), F32)],
        scratch_shapes=[pltpu.VMEM((hp + tm, C_WIDTH), F32), pltpu.VMEM((unit, C_WIDTH), F32)],
        compiler_params=_cparams(("parallel", "arbitrary")),
        name="rglru_u%d" % unit,
    )(proj3, proj3, prev, h0, cw, cb, wr, br, wi, bi, lam)


def _merge_kernel(x_ref, ga_ref, gb_ref, gc_ref, oa_ref, oc_ref,
                  o0_ref, o1_ref, o2_ref, l0_ref, l1_ref, l2_ref,
                  wpa_ref, wpb_ref, wpc_ref, wo_ref, g_ref, b_ref, out_ref):
    l0, l1, l2 = l0_ref[...], l1_ref[...], l2_ref[...]
    m = jnp.maximum(jnp.maximum(l0, l1), l2)
    e0, e1, e2 = jnp.exp(l0 - m), jnp.exp(l1 - m), jnp.exp(l2 - m)
    den = e0 + e1 + e2
    ob = (e0 / den) * o0_ref[...] + (e1 / den) * o1_ref[...] + (e2 / den) * o2_ref[...]
    merged = (_sigmoid(ga_ref[...]) * _dot(oa_ref[...], wpa_ref[...])
              + _sigmoid(gb_ref[...]) * _dot(ob, wpb_ref[...])
              + _sigmoid(gc_ref[...]) * _dot(oc_ref[...], wpc_ref[...]))
    mix = _dot(merged, wo_ref[...])
    out_ref[...] = _layer_norm(DN_ALPHA * x_ref[...] + mix, g_ref[...], b_ref[...])


def _merge_call(x, proj, oa, oc, obs, lses, wpa, wpb, wpc, wo, g, b, tm):
    rows = x.shape[0]
    row_spec = lambda w, j=0: pl.BlockSpec((tm, w), lambda i: (i, j))
    full = lambda a: pl.BlockSpec(a.shape, lambda i: (0, 0))
    return pl.pallas_call(
        _merge_kernel,
        grid=(rows // tm,),
        in_specs=[row_spec(D_MODEL), row_spec(D_MODEL, 0), row_spec(D_MODEL, 1), row_spec(D_MODEL, 2),
                  row_spec(A_VW), row_spec(C_WIDTH)]
                 + [row_spec(B_GW)] * 6
                 + [full(wpa), full(wpb), full(wpc), full(wo), full(g), full(b)],
        out_specs=row_spec(D_MODEL),
        out_shape=jax.ShapeDtypeStruct((rows, D_MODEL), F32),
        compiler_params=_cparams(("parallel",)),
        name="merge_ln",
    )(x, proj, proj, proj, oa, oc, *obs, *lses, wpa, wpb, wpc, wo, g, b)


def _ffn_kernel(x_ref, prev_ref, wu_ref, cw_ref, cb_ref, wd_ref, g_ref, b_ref,
                o_ref, st_ref, xx_ref, *, unit, hp):
    i = pl.program_id(1)
    tm = x_ref.shape[1]
    halo = (F_CONV - 1) * unit

    @pl.when(i == 0)
    def _():
        xx_ref[hp - halo:hp, :] = prev_ref[0]

    x = x_ref[0]
    xb = x.astype(BF16)
    f = jnp.zeros((tm, D_MODEL), F32)
    for c0 in range(0, D_FF, FF_CHUNK):
        c1 = c0 + FF_CHUNK
        gate = jnp.dot(xb, wu_ref[:, c0:c1], preferred_element_type=F32)
        up = jnp.dot(xb, wu_ref[:, D_FF + c0:D_FF + c1], preferred_element_type=F32)
        xx_ref[hp:hp + tm, c0:c1] = gate
        y = cb_ref[:, c0:c1] + xx_ref[hp - halo:hp - halo + tm, c0:c1] * cw_ref[0:1, c0:c1]
        for j in range(1, F_CONV):
            y = y + xx_ref[hp - halo + j * unit:hp - halo + j * unit + tm, c0:c1] * cw_ref[j:j + 1, c0:c1]
        tail = xx_ref[hp + tm - halo:hp + tm, c0:c1]
        xx_ref[hp - halo:hp, c0:c1] = tail
        st_ref[0, :, c0:c1] = tail
        f = f + _dot(_silu(y) * up, wd_ref[c0:c1, :])
    o_ref[0] = _layer_norm(DN_ALPHA * x + f, g_ref[...], b_ref[...])


def _ffn_call(x3, prev, wu, cw, cb, wd, g, b, unit, tm):
    nseq, rows, _ = x3.shape
    halo = (F_CONV - 1) * unit
    hp = -(-halo // 8) * 8
    const = lambda a: pl.BlockSpec(a.shape, lambda n, i: (0, 0), pipeline_mode=pl.Buffered(1))
    return pl.pallas_call(
        functools.partial(_ffn_kernel, unit=unit, hp=hp),
        grid=(nseq, rows // tm),
        in_specs=[pl.BlockSpec((1, tm, D_MODEL), lambda n, i: (n, i, 0)),
                  pl.BlockSpec((1, halo, D_FF), lambda n, i: (n, 0, 0)),
                  const(wu), const(cw), const(cb), const(wd), const(g), const(b)],
        out_specs=[pl.BlockSpec((1, tm, D_MODEL), lambda n, i: (n, i, 0)),
                   pl.BlockSpec((1, halo, D_FF), lambda n, i: (n, 0, 0))],
        out_shape=[jax.ShapeDtypeStruct((nseq, rows, D_MODEL), F32),
                   jax.ShapeDtypeStruct((nseq, halo, D_FF), F32)],
        scratch_shapes=[pltpu.VMEM((hp + tm, D_FF), F32)],
        compiler_params=_cparams(("parallel", "arbitrary")),
        name="conv_ffn_u%d" % unit,
    )(x3, prev, wu, cw, cb, wd, g, b)


def _block_diag(w):
    nblk, bw, _ = w.shape
    eye = jnp.eye(nblk, dtype=w.dtype)
    return (eye[:, None, :, None] * w[:, :, None, :]).reshape(nblk * bw, nblk * bw)


def _layer_params(l, w_in, a_conv_w, a_A_log, a_dt_bias, a_norm_w, c_conv_w, c_conv_b, c_w_r, c_b_r,
                  c_w_i, c_b_i, c_lam, w_pa, w_pb, w_pc, w_o, ln1_g, ln1_b, f_up, f_conv_w, f_conv_b,
                  f_down, ln2_g, ln2_b):
    w = w_in[l]
    o_b = A_QKV
    o_a = o_b + A_HEADS
    o_z = o_a + A_HEADS
    o_qb = o_z + A_VW
    o_xc = o_qb + 3 * B_W
    o_gc = o_xc + C_WIDTH
    o_gt = o_gc + C_WIDTH
    pad = jnp.zeros((D_MODEL, N_PROJ - OFF_BA - 2 * A_HEADS), w.dtype)
    wp = jnp.concatenate([w[:, o_gt:o_gt + 3 * D_MODEL], w[:, o_xc:o_gc], w[:, o_gc:o_gt], w[:, o_z:o_qb],
                          w[:, 0:o_b], w[:, o_qb:o_xc], w[:, o_b:o_z], pad], axis=1).astype(BF16)
    gp = jnp.zeros((2, 128), F32)
    gp = gp.at[0, A_HEADS:2 * A_HEADS].set(a_A_log[l]).at[1, A_HEADS:2 * A_HEADS].set(a_dt_bias[l])
    row = lambda v: v.reshape(1, -1)
    return dict(
        wp=wp, a_cw=a_conv_w[l], gp=gp, a_nw=row(a_norm_w[l]),
        c_cw=c_conv_w[l], c_cb=row(c_conv_b[l]),
        c_wr=_block_diag(c_w_r[l]).astype(BF16), c_br=row(c_b_r[l]),
        c_wi=_block_diag(c_w_i[l]).astype(BF16), c_bi=row(c_b_i[l]), c_lam=row(c_lam[l]),
        wpa=w_pa[l].astype(BF16), wpb=w_pb[l].astype(BF16), wpc=w_pc[l].astype(BF16), wo=w_o[l].astype(BF16),
        ln1_g=row(ln1_g[l]), ln1_b=row(ln1_b[l]),
        f_up=f_up[l].astype(BF16), f_cw=f_conv_w[l], f_cb=row(f_conv_b[l]), f_down=f_down[l].astype(BF16),
        ln2_g=row(ln2_g[l]), ln2_b=row(ln2_b[l]))


def _layer_common(x2, nseq, rows, unit, p, rope_tabs, mixer_a, mixer_b, c_prev, c_h0, f_prev,
                  tm_proj, tm_merge, tm_seq):
    proj = _proj_call(x2, p["wp"], tm_proj)
    proj3 = proj.reshape(nseq, rows, N_PROJ)
    o_a, a_state = mixer_a(proj3)
    q_rot, kv0, kv1, kv2 = _rope_call(proj3, rope_tabs, tm_seq)
    kvs = (kv0, kv1, kv2)
    obs, lses = mixer_b(q_rot, kvs)
    o_c, c_hl = _rglru_call(proj3, c_prev, c_h0, p["c_cw"], p["c_cb"], p["c_wr"], p["c_br"], p["c_wi"],
                            p["c_bi"], p["c_lam"], unit, tm_seq)
    x1 = _merge_call(x2, proj, o_a.reshape(nseq * rows, A_VW), o_c.reshape(nseq * rows, C_WIDTH), obs, lses,
                     p["wpa"], p["wpb"], p["wpc"], p["wo"], p["ln1_g"], p["ln1_b"], tm_merge)
    x_out, f_state = _ffn_call(x1.reshape(nseq, rows, D_MODEL), f_prev, p["f_up"], p["f_cw"], p["f_cb"],
                               p["f_down"], p["ln2_g"], p["ln2_b"], unit, tm_seq)
    return x_out.reshape(nseq * rows, D_MODEL), proj3, a_state, kvs, c_hl, f_state


def kernel(x_prompt, x_sample, state_a_conv, state_a_rec, cache_b_w128, cache_b_w512, cache_b_w2048,
           state_c_conv, state_c_h, state_f_conv, ln_in_g, ln_in_b, w_in, a_conv_w, a_A_log, a_dt_bias,
           a_norm_w, c_conv_w, c_conv_b, c_w_r, c_b_r, c_w_i, c_b_i, c_lam, w_pa, w_pb, w_pc, w_o,
           ln1_g, ln1_b, f_up, f_conv_w, f_conv_b, f_down, ln2_g, ln2_b):
    Bp, Sp, _ = x_prompt.shape
    Bs, Ts, _ = x_sample.shape
    caches = (cache_b_w128, cache_b_w512, cache_b_w2048)
    for (win, dil), cache in zip(B_GROUPS, caches):
        assert win // dil == B_NK and cache.shape[2] == win and Sp % (dil * B_NK) == 0

    lng, lnb = ln_in_g.reshape(1, -1), ln_in_b.reshape(1, -1)
    hp = _ln_call(x_prompt.reshape(Bp * Sp, D_MODEL), lng, lnb, 512)
    xs_tm = jnp.swapaxes(x_sample, 0, 1).reshape(Ts * Bs, D_MODEL)
    hs = _ln_call(xs_tm, lng, lnb, Ts * Bs)

    tabs_p = _rope_tables(jnp.arange(Sp))
    tabs_s = _rope_tables(jnp.repeat(PAST_LEN + jnp.arange(Ts), Bs))

    zeros_c = jnp.zeros((Bp, C_CONV - 1, C_WIDTH), F32)
    zeros_h = jnp.zeros((Bp, 1, C_WIDTH), F32)
    zeros_f = jnp.zeros((Bp, F_CONV - 1, D_FF), F32)

    outs_p, outs_s = [], []
    for l in range(DEPTH):
        p = _layer_params(l, w_in, a_conv_w, a_A_log, a_dt_bias, a_norm_w, c_conv_w, c_conv_b, c_w_r, c_b_r,
                          c_w_i, c_b_i, c_lam, w_pa, w_pb, w_pc, w_o, ln1_g, ln1_b, f_up, f_conv_w,
                          f_conv_b, f_down, ln2_g, ln2_b)

        def mixer_a_p(proj3):
            return _gdn_prompt_call(proj3, p["a_cw"], p["gp"], p["a_nw"])

        def mixer_b_p(q_rot, kvs):
            res = [_attn_prompt_call(q_rot, kvs[g], g, dil) for g, (_, dil) in enumerate(B_GROUPS)]
            return [r[0] for r in res], [r[1] for r in res]

        hp, proj3, a_rec, kvs, c_hl, f_st = _layer_common(
            hp, Bp, Sp, 1, p, tabs_p, mixer_a_p, mixer_b_p, zeros_c, zeros_h, zeros_f, 1024, 256, 256)
        outs_p.append((
            proj3[:, Sp - (A_CONV - 1):, OFF_QKVA:OFF_QKVA + A_QKV],
            a_rec,
            kvs[0][:, Sp - min(B_GROUPS[0][0], Sp):].reshape(Bp, -1, 2, B_HEADS, B_HD),
            kvs[1][:, Sp - min(B_GROUPS[1][0], Sp):].reshape(Bp, -1, 2, B_HEADS, B_HD),
            kvs[2][:, Sp - min(B_GROUPS[2][0], Sp):].reshape(Bp, -1, 2, B_HEADS, B_HD),
            proj3[:, Sp - (C_CONV - 1):, OFF_XC:OFF_XC + C_WIDTH],
            c_hl[:, 0],
            f_st))

        a_st = jnp.swapaxes(state_a_conv[l], 0, 1)

        def mixer_a_s(proj3):
            o, s_new = _gdn_sample_call(proj3.reshape(Ts, Bs, N_PROJ), a_st, state_a_rec[l],
                                        p["a_cw"], p["gp"], p["a_nw"])
            return o, s_new

        def mixer_b_s(q_rot, kvs):
            q3 = q_rot.reshape(Ts, Bs, B_W)
            res = [_attn_sample_call(q3, kvs[g].reshape(Ts, Bs, 2 * B_GW), caches[g][l], g, dil)
                   for g, (_, dil) in enumerate(B_GROUPS)]
            return ([r[0].reshape(Ts * Bs, B_GW) for r in res], [r[1].reshape(Ts * Bs, B_GW) for r in res])

        c_prev = jnp.swapaxes(state_c_conv[l], 0, 1).reshape(1, (C_CONV - 1) * Bs, C_WIDTH)
        f_prev = jnp.swapaxes(state_f_conv[l], 0, 1).reshape(1, (F_CONV - 1) * Bs, D_FF)
        hs, proj3, a_rec, kvs, c_hl, f_st = _layer_common(
            hs, 1, Ts * Bs, Bs, p, tabs_s, mixer_a_s, mixer_b_s, c_prev, state_c_h[l][None], f_prev,
            Ts * Bs, min(256, Ts * Bs), Ts * Bs)
        pt = proj3.reshape(Ts, Bs, N_PROJ)
        tm2bm = lambda a: jnp.swapaxes(a, 0, 1)
        outs_s.append((
            tm2bm(pt[Ts - (A_CONV - 1):, :, OFF_QKVA:OFF_QKVA + A_QKV]),
            a_rec,
            tm2bm(kvs[0].reshape(Ts, Bs, 2 * B_GW)).reshape(Bs, Ts, 2, B_HEADS, B_HD),
            tm2bm(kvs[1].reshape(Ts, Bs, 2 * B_GW)).reshape(Bs, Ts, 2, B_HEADS, B_HD),
            tm2bm(kvs[2].reshape(Ts, Bs, 2 * B_GW)).reshape(Bs, Ts, 2, B_HEADS, B_HD),
            tm2bm(pt[Ts - (C_CONV - 1):, :, OFF_XC:OFF_XC + C_WIDTH]),
            c_hl[0],
            tm2bm(f_st.reshape(F_CONV - 1, Bs, D_FF))))

    stack = lambda outs: [jnp.stack(s, 0) for s in zip(*outs)]
    y_p = hp.reshape(Bp, Sp, D_MODEL)
    y_s = jnp.swapaxes(hs.reshape(Ts, Bs, D_MODEL), 0, 1)
    return (y_p, y_s, *stack(outs_p), *stack(outs_s))
```

```python
import functools
import math

import numpy as np
import jax
import jax.numpy as jnp
from jax import lax
from jax.experimental import pallas as pl
from jax.experimental.pallas import tpu as pltpu

F32 = jnp.float32
BF16 = jnp.bfloat16

D_MODEL = 1024
DEPTH = 4
PAST_LEN = 2048
A_HEADS, A_DK, A_DV, A_CONV, A_CHUNK = 4, 128, 128, 4, 64
B_GROUPS = ((128, 1), (512, 4), (2048, 16))
B_HEADS, B_HD = 4, 64
B_ROT = B_HD // 4
ROPE_THETA = 500000.0
B_NK = 128
C_WIDTH, C_BLOCKS, C_CONV, C_POW = 512, 8, 4, 8.0
C_BW = C_WIDTH // C_BLOCKS
D_FF = ((8 * D_MODEL // 3 + 255) // 256) * 256
F_CONV = 3
FF_CHUNK = 1408
DN_ALPHA = (2 * DEPTH) ** 0.25
LN_EPS = 1e-5
NORM_EPS = 1e-6

A_QK = A_HEADS * A_DK
A_VW = A_HEADS * A_DV
A_QKV = 2 * A_QK + A_VW
B_GW = B_HEADS * B_HD
B_W = len(B_GROUPS) * B_GW

OFF_GATES = 0
OFF_XC = 3072
OFF_GC = 3584
OFF_Z = 4096
OFF_QKVA = 4608
OFF_QB = 6144
OFF_BA = 8448
N_PROJ = 8704
PROJ_TN = 512

VMEM_LIMIT = 56 * 1024 * 1024


def _cparams(sem):
    return pltpu.CompilerParams(dimension_semantics=sem, vmem_limit_bytes=VMEM_LIMIT)


def _sigmoid(x):
    return 1.0 / (1.0 + jnp.exp(-x))


def _silu(x):
    return x * _sigmoid(x)


def _softplus(x):
    return jnp.maximum(x, 0.0) + jnp.log1p(jnp.exp(-jnp.abs(x)))


def _layer_norm(x, g, b):
    mu = jnp.mean(x, axis=-1, keepdims=True)
    xc = x - mu
    var = jnp.mean(xc * xc, axis=-1, keepdims=True)
    return xc * lax.rsqrt(var + LN_EPS) * g + b


def _dot(a, b):
    return jnp.dot(a.astype(BF16), b.astype(BF16), preferred_element_type=F32)


def _dot_nt(a, b):
    return lax.dot_general(a.astype(BF16), b.astype(BF16), (((1,), (1,)), ((), ())),
                           preferred_element_type=F32)


def _dot_tn(a, b):
    return lax.dot_general(a.astype(BF16), b.astype(BF16), (((0,), (0,)), ((), ())),
                           preferred_element_type=F32)


def _split2(a):
    hi = a.astype(BF16)
    lo = (a - hi.astype(F32)).astype(BF16)
    return hi, lo


def _dot3(a, b):
    ah, al = _split2(a)
    bh, bl = _split2(b)
    d = lambda x, y: jnp.dot(x, y, preferred_element_type=F32)
    return d(ah, bh) + (d(ah, bl) + d(al, bh))


def _ln_kernel(x_ref, g_ref, b_ref, o_ref):
    o_ref[...] = _layer_norm(x_ref[...], g_ref[...], b_ref[...])


def _ln_call(x, g, b, tm):
    rows = x.shape[0]
    return pl.pallas_call(
        _ln_kernel,
        grid=(rows // tm,),
        in_specs=[pl.BlockSpec((tm, D_MODEL), lambda i: (i, 0)),
                  pl.BlockSpec((1, D_MODEL), lambda i: (0, 0)),
                  pl.BlockSpec((1, D_MODEL), lambda i: (0, 0))],
        out_specs=pl.BlockSpec((tm, D_MODEL), lambda i: (i, 0)),
        out_shape=jax.ShapeDtypeStruct((rows, D_MODEL), F32),
        compiler_params=_cparams(("parallel",)),
        name="entry_ln",
    )(x, g, b)


def _proj_kernel(x_ref, w_ref, o_ref, xb_ref):
    @pl.when(pl.program_id(1) == 0)
    def _():
        xb_ref[...] = x_ref[...].astype(BF16)

    o_ref[...] = jnp.dot(xb_ref[...], w_ref[...], preferred_element_type=F32)


def _proj_call(x, w, tm):
    rows = x.shape[0]
    return pl.pallas_call(
        _proj_kernel,
        grid=(rows // tm, N_PROJ // PROJ_TN),
        in_specs=[pl.BlockSpec((tm, D_MODEL), lambda i, j: (i, 0)),
                  pl.BlockSpec((D_MODEL, PROJ_TN), lambda i, j: (0, j))],
        out_specs=pl.BlockSpec((tm, PROJ_TN), lambda i, j: (i, j)),
        out_shape=jax.ShapeDtypeStruct((rows, N_PROJ), F32),
        scratch_shapes=[pltpu.VMEM((tm, D_MODEL), BF16)],
        compiler_params=_cparams(("parallel", "arbitrary")),
        name="in_proj",
    )(x, w)


def _gdn_prompt_kernel(x_ref, z_ref, ba_ref, cw_ref, gp_ref, nw_ref,
                       o_ref, sfin_ref, xx_ref, s_ref):
    c = pl.program_id(0)
    C = A_CHUNK
    nb = x_ref.shape[0]

    @pl.when(c == 0)
    def _():
        xx_ref[:, 0:8, :] = jnp.zeros((nb, 8, A_QKV), F32)
        s_ref[...] = jnp.zeros(s_ref.shape, F32)

    ri = lax.broadcasted_iota(jnp.int32, (C, C), 0)
    ci = lax.broadcasted_iota(jnp.int32, (C, C), 1)
    tril = ri >= ci
    strict = ri > ci
    eye = (ri == ci).astype(F32)
    trilf = tril.astype(F32)
    nw = nw_ref[...]

    ch = []
    for b in range(nb):
        x = x_ref[b]
        xx_ref[b, 8:8 + C, :] = x
        y = xx_ref[b, 5:5 + C, :] * cw_ref[0:1, :]
        for j in range(1, A_CONV):
            y = y + xx_ref[b, 5 + j:5 + j + C, :] * cw_ref[j:j + 1, :]
        xx_ref[b, 0:8, :] = x[C - 8:C, :]
        y = _silu(y)
        ba = ba_ref[b]
        beta_all = _sigmoid(ba)
        g_all = -jnp.exp(gp_ref[0:1, :]) * _softplus(ba + gp_ref[1:2, :])
        gc = _dot3(trilf, g_all)
        gct = jnp.concatenate([gc, jnp.zeros((128 - C, 128), F32)], axis=0).T
        for h in range(A_HEADS):
            q = y[:, h * A_DK:(h + 1) * A_DK]
            k = y[:, A_QK + h * A_DK:A_QK + (h + 1) * A_DK]
            v = y[:, 2 * A_QK + h * A_DV:2 * A_QK + (h + 1) * A_DV]
            q = q * lax.rsqrt(jnp.sum(q * q, axis=-1, keepdims=True) + NORM_EPS) * (A_DK ** -0.5)
            k = k * lax.rsqrt(jnp.sum(k * k, axis=-1, keepdims=True) + NORM_EPS)
            beta = beta_all[:, h:h + 1]
            gcol = gc[:, 4 + h:5 + h]
            grow = gct[4 + h:5 + h, 0:C]
            decay = jnp.where(tril, jnp.exp(jnp.where(tril, gcol - grow, 0.0)), 0.0)
            eg = jnp.exp(gcol)
            glast = gcol[C - 1:C, :]
            ch.append(dict(b=b, h=h, q=q, k=k, kb=k * beta, vb=v * beta, decay=decay, eg=eg,
                           k_dec=k * jnp.exp(glast - gcol), q_dec=q * eg, eglast=jnp.exp(glast)))
    for d in ch:
        d["m"] = -jnp.where(strict, _dot_nt(d["kb"], d["k"]) * d["decay"], 0.0)
        d["t"] = eye + d["m"]
    for d in ch:
        d["qk"] = jnp.where(tril, _dot_nt(d["q"], d["k"]) * d["decay"], 0.0)
    for _ in range(5):
        for d in ch:
            d["m"] = _dot(d["m"], d["m"])
        for d in ch:
            d["t"] = d["t"] + _dot(d["t"], d["m"])
    for d in ch:
        d["u"] = _dot(d["t"], d["vb"])
        d["w"] = _dot(d["t"], d["kb"] * d["eg"])
    for d in ch:
        d["s"] = s_ref[d["b"], d["h"]]
        d["v_new"] = d["u"] - _dot(d["w"], d["s"])
    for d in ch:
        d["o"] = _dot(d["q_dec"], d["s"]) + _dot(d["qk"], d["v_new"])
        s_ref[d["b"], d["h"]] = d["s"] * d["eglast"] + _dot_tn(d["k_dec"], d["v_new"])
    for d in ch:
        b, h, o = d["b"], d["h"], d["o"]
        o = o * lax.rsqrt(jnp.mean(o * o, axis=-1, keepdims=True) + NORM_EPS) * nw
        o_ref[b, :, h * A_DV:(h + 1) * A_DV] = o * _silu(z_ref[b, :, h * A_DV:(h + 1) * A_DV])

    @pl.when(c == pl.num_programs(0) - 1)
    def _():
        sfin_ref[...] = s_ref[...]


def _gdn_prompt_call(proj3, cw, gp, nw):
    B, S, _ = proj3.shape
    C = A_CHUNK
    return pl.pallas_call(
        _gdn_prompt_kernel,
        grid=(S // C,),
        in_specs=[pl.BlockSpec((B, C, A_QKV), lambda c: (0, c, OFF_QKVA // A_QKV)),
                  pl.BlockSpec((B, C, A_VW), lambda c: (0, c, OFF_Z // A_VW)),
                  pl.BlockSpec((B, C, 128), lambda c: (0, c, OFF_BA // 128)),
                  pl.BlockSpec((A_CONV, A_QKV), lambda c: (0, 0)),
                  pl.BlockSpec((2, 128), lambda c: (0, 0)),
                  pl.BlockSpec((1, A_DV), lambda c: (0, 0))],
        out_specs=[pl.BlockSpec((B, C, A_VW), lambda c: (0, c, 0)),
                   pl.BlockSpec((B, A_HEADS, A_DK, A_DV), lambda c: (0, 0, 0, 0))],
        out_shape=[jax.ShapeDtypeStruct((B, S, A_VW), F32),
                   jax.ShapeDtypeStruct((B, A_HEADS, A_DK, A_DV), F32)],
        scratch_shapes=[pltpu.VMEM((B, 8 + C, A_QKV), F32),
                        pltpu.VMEM((B, A_HEADS, A_DK, A_DV), F32)],
        compiler_params=_cparams(("arbitrary",)),
        name="gdn_prompt",
    )(proj3, proj3, proj3, cw, gp, nw)


GS_BB = 8


def _gdn_sample_kernel(x_ref, z_ref, ba_ref, st_ref, s0_ref, cw_ref, gp_ref, nw_ref,
                       o_ref, snew_ref, oscr_ref):
    T = x_ref.shape[0]
    bb = GS_BB
    xx = [st_ref[j] for j in range(A_CONV - 1)] + [x_ref[t] for t in range(T)]
    vb = [[None] * A_HEADS for _ in range(T)]
    rows = [[] for _ in range(A_HEADS)]
    kinds = [[[None] * T for _ in range(4)] for _ in range(A_HEADS)]
    for t in range(T):
        y = xx[t] * cw_ref[0:1, :]
        for j in range(1, A_CONV):
            y = y + xx[t + j] * cw_ref[j:j + 1, :]
        y = _silu(y)
        ba = ba_ref[t]
        beta_all = _sigmoid(ba)
        eg_all = jnp.exp(-jnp.exp(gp_ref[0:1, :]) * _softplus(ba + gp_ref[1:2, :]))
        for h in range(A_HEADS):
            q = y[:, h * A_DK:(h + 1) * A_DK]
            k = y[:, A_QK + h * A_DK:A_QK + (h + 1) * A_DK]
            v = y[:, 2 * A_QK + h * A_DV:2 * A_QK + (h + 1) * A_DV]
            q = q * lax.rsqrt(jnp.sum(q * q, axis=-1, keepdims=True) + NORM_EPS) * (A_DK ** -0.5)
            k = k * lax.rsqrt(jnp.sum(k * k, axis=-1, keepdims=True) + NORM_EPS)
            beta = beta_all[:, h:h + 1]
            eg = eg_all[:, 4 + h:5 + h]
            kinds[h][0][t] = q
            kinds[h][1][t] = k
            kinds[h][2][t] = k * (beta * eg)
            kinds[h][3][t] = jnp.broadcast_to(eg, (bb, A_DK))
            vb[t][h] = v * beta
    for h in range(A_HEADS):
        xm = jnp.concatenate([kinds[h][kd][t] for kd in range(4) for t in range(T)], axis=0)
        xt = xm.T
        for b in range(bb):
            s = s0_ref[0, b, h]
            for t in range(T):
                col = lambda kd: xt[:, kd * T * bb + t * bb + b:kd * T * bb + t * bb + b + 1]
                ks = jnp.sum(s * col(2), axis=0, keepdims=True)
                r = vb[t][h][b:b + 1, :] - ks
                s = s * col(3) + col(1) * r
                o = jnp.sum(s * col(0), axis=0, keepdims=True)
                oscr_ref[t, b:b + 1, h * A_DV:(h + 1) * A_DV] = o
            snew_ref[b, h] = s
    nw = nw_ref[...]
    for t in range(T):
        z = z_ref[t]
        for h in range(A_HEADS):
            o = oscr_ref[t, :, h * A_DV:(h + 1) * A_DV]
            o = o * lax.rsqrt(jnp.mean(o * o, axis=-1, keepdims=True) + NORM_EPS) * nw
            o_ref[t, :, h * A_DV:(h + 1) * A_DV] = o * _silu(z[:, h * A_DV:(h + 1) * A_DV])


def _gdn_sample_call(proj3, st, s0, layer, cw, gp, nw):
    T, B, _ = proj3.shape
    bb = GS_BB
    return pl.pallas_call(
        _gdn_sample_kernel,
        grid=(B // bb,),
        in_specs=[pl.BlockSpec((T, bb, A_QKV), lambda i: (0, i, OFF_QKVA // A_QKV)),
                  pl.BlockSpec((T, bb, A_VW), lambda i: (0, i, OFF_Z // A_VW)),
                  pl.BlockSpec((T, bb, 128), lambda i: (0, i, OFF_BA // 128)),
                  pl.BlockSpec((A_CONV - 1, bb, A_QKV), lambda i: (0, i, 0)),
                  pl.BlockSpec((1, bb, A_HEADS, A_DK, A_DV), lambda i: (layer, i, 0, 0, 0)),
                  pl.BlockSpec((A_CONV, A_QKV), lambda i: (0, 0)),
                  pl.BlockSpec((2, 128), lambda i: (0, 0)),
                  pl.BlockSpec((1, A_DV), lambda i: (0, 0))],
        out_specs=[pl.BlockSpec((T, bb, A_VW), lambda i: (0, i, 0)),
                   pl.BlockSpec((bb, A_HEADS, A_DK, A_DV), lambda i: (i, 0, 0, 0))],
        out_shape=[jax.ShapeDtypeStruct((T, B, A_VW), F32),
                   jax.ShapeDtypeStruct(s0.shape[1:], F32)],
        scratch_shapes=[pltpu.VMEM((T, bb, A_VW), F32)],
        compiler_params=_cparams(("parallel",)),
        name="gdn_sample",
    )(proj3, proj3, proj3, st, s0, cw, gp, nw)


def _rope_qkv(q_ref, k_ref, v_ref, c_ref, s1_ref, s2_ref):
    reps = B_W // 128
    c = jnp.concatenate([c_ref[0]] * reps, axis=1)
    s1 = jnp.concatenate([s1_ref[0]] * reps, axis=1)
    s2 = jnp.concatenate([s2_ref[0]] * reps, axis=1)

    def rot(x):
        return x * c + pltpu.roll(x, B_W - B_ROT // 2, 1) * s1 + pltpu.roll(x, B_ROT // 2, 1) * s2

    return rot(q_ref[0]) * (B_HD ** -0.5), rot(k_ref[0]), v_ref[0]


def _rope_in_specs(tm):
    qb = OFF_QB // B_W
    tab_spec = pl.BlockSpec((1, tm, 128), lambda n, i: (0, i, 0))
    return [pl.BlockSpec((1, tm, B_W), lambda n, i: (n, i, qb)),
            pl.BlockSpec((1, tm, B_W), lambda n, i: (n, i, qb + 1)),
            pl.BlockSpec((1, tm, B_W), lambda n, i: (n, i, qb + 2)),
            tab_spec, tab_spec, tab_spec]


def _store_chunks(buf_ref, x):
    for j in range(buf_ref.shape[0]):
        buf_ref[j] = x[:, j * 128:(j + 1) * 128]


def _load_chunks_strided(buf_ref, start, size, stride):
    return jnp.concatenate([buf_ref[j, pl.ds(start, size, stride=stride), :] for j in range(buf_ref.shape[0])],
                           axis=1)


def _rope_prompt_kernel(q_ref, k_ref, v_ref, c_ref, s1_ref, s2_ref,
                        q0_ref, q1_ref, q2_ref, kv0_ref, kv1_ref, kv2_ref, t0_ref, t1_ref, t2_ref,
                        qs_ref, kvs_ref, *, tails):
    i = pl.program_id(1)
    tm = q_ref.shape[1]
    q, k, v = _rope_qkv(q_ref, k_ref, v_ref, c_ref, s1_ref, s2_ref)
    outs = ((q0_ref, kv0_ref, t0_ref), (q1_ref, kv1_ref, t1_ref), (q2_ref, kv2_ref, t2_ref))
    for g, ((_, dil), (qo_ref, kvo_ref, to_ref)) in enumerate(zip(B_GROUPS, outs)):
        lo, hi = g * B_GW, (g + 1) * B_GW
        qg = q[:, lo:hi]
        kv = jnp.concatenate([k[:, lo:hi], v[:, lo:hi]], axis=1)
        if dil == 1:
            qo_ref[0, 0] = qg.astype(BF16)
            kvo_ref[0, 0] = kv
        else:
            _store_chunks(qs_ref, qg)
            _store_chunks(kvs_ref, kv)
            for r in range(dil):
                qo_ref[0, r] = _load_chunks_strided(qs_ref, r, tm // dil, dil).astype(BF16)
                kvo_ref[0, r] = _load_chunks_strided(kvs_ref, r, tm // dil, dil)
        first, w = tails[g]

        @pl.when(i >= first)
        def _(to_ref=to_ref, w=w, kv=kv):
            to_ref[0] = kv[tm - w:tm, :].T


def _rope_prompt_call(proj3, tabs, tm):
    B, S, _ = proj3.shape
    nt = S // tm
    out_shape, out_specs, tails = [], [], []
    for width, dtype in ((B_GW, BF16), (2 * B_GW, F32)):
        for _, dil in B_GROUPS:
            out_shape.append(jax.ShapeDtypeStruct((B, dil, S // dil, width), dtype))
            out_specs.append(pl.BlockSpec((1, dil, tm // dil, width), lambda n, i: (n, 0, i, 0)))
    for win, _ in B_GROUPS:
        win = min(win, S)
        w = min(tm, win)
        first = nt - win // w
        tails.append((first, w))
        out_shape.append(jax.ShapeDtypeStruct((B, 2 * B_GW, win), F32))
        out_specs.append(pl.BlockSpec((1, 2 * B_GW, w),
                                      lambda n, i, first=first: (n, 0, jnp.maximum(i - first, 0))))
    return pl.pallas_call(
        functools.partial(_rope_prompt_kernel, tails=tuple(tails)),
        grid=(B, nt),
        in_specs=_rope_in_specs(tm),
        out_specs=out_specs,
        out_shape=out_shape,
        scratch_shapes=[pltpu.VMEM((B_GW // 128, tm, 128), F32), pltpu.VMEM((2 * B_GW // 128, tm, 128), F32)],
        compiler_params=_cparams(("parallel", "arbitrary")),
        name="rope_prompt",
    )(proj3, proj3, proj3, *tabs)


def _rope_sample_kernel(q_ref, k_ref, v_ref, c_ref, s1_ref, s2_ref,
                        qo_ref, kv0_ref, kv1_ref, kv2_ref, t0_ref, t1_ref, t2_ref, *, unit):
    q, k, v = _rope_qkv(q_ref, k_ref, v_ref, c_ref, s1_ref, s2_ref)
    qo_ref[0] = q
    for g, (kvo_ref, to_ref) in enumerate(((kv0_ref, t0_ref), (kv1_ref, t1_ref), (kv2_ref, t2_ref))):
        kv = jnp.concatenate([k[:, g * B_GW:(g + 1) * B_GW], v[:, g * B_GW:(g + 1) * B_GW]], axis=1)
        kvo_ref[0] = kv
        for t in range(q.shape[0] // unit):
            to_ref[t] = kv[t * unit:(t + 1) * unit, :].T


def _rope_sample_call(proj3, tabs, unit):
    _, rows, _ = proj3.shape
    T = rows // unit
    kv_shape = jax.ShapeDtypeStruct((1, rows, 2 * B_GW), F32)
    kv_spec = pl.BlockSpec((1, rows, 2 * B_GW), lambda n, i: (0, 0, 0))
    t_shape = jax.ShapeDtypeStruct((T, 2 * B_GW, unit), F32)
    t_spec = pl.BlockSpec((T, 2 * B_GW, unit), lambda n, i: (0, 0, 0))
    return pl.pallas_call(
        functools.partial(_rope_sample_kernel, unit=unit),
        grid=(1, 1),
        in_specs=_rope_in_specs(rows),
        out_specs=[pl.BlockSpec((1, rows, B_W), lambda n, i: (0, 0, 0))] + [kv_spec] * 3 + [t_spec] * 3,
        out_shape=[jax.ShapeDtypeStruct((1, rows, B_W), F32)] + [kv_shape] * 3 + [t_shape] * 3,
        compiler_params=_cparams(("arbitrary", "arbitrary")),
        name="rope_sample",
    )(proj3, proj3, proj3, *tabs)


def _rope_tables(pos):
    half = B_ROT // 2
    inv = ROPE_THETA ** (-jnp.arange(half, dtype=F32) / half)
    ang = pos.astype(F32)[:, None] * inv
    cos, sin = jnp.cos(ang), jnp.sin(ang)
    rows = pos.shape[0]
    one = jnp.ones((rows, B_HD - B_ROT), F32)
    zero = jnp.zeros((rows, half), F32)
    zrest = jnp.zeros((rows, B_HD - B_ROT), F32)
    c = jnp.concatenate([cos, cos, one], axis=1)
    s1 = jnp.concatenate([-sin, zero, zrest], axis=1)
    s2 = jnp.concatenate([zero, sin, zrest], axis=1)
    tile = lambda t: jnp.concatenate([t, t], axis=1)[None]
    return tile(c), tile(s1), tile(s2)


def _head_lane_mask(rows):
    lane = lax.broadcasted_iota(jnp.int32, (rows, B_GW), 1)
    return [(lane >= h * B_HD) & (lane < (h + 1) * B_HD) for h in range(B_HEADS)]


def _attn_prompt_kernel(q_ref, kvp_ref, kvc_ref, o_ref, l_ref):
    i = pl.program_id(2)
    nk = B_NK
    q = q_ref[0, 0]
    kk = jnp.concatenate([kvp_ref[0, 0, :, 0:B_GW], kvc_ref[0, 0, :, 0:B_GW]], axis=0).astype(BF16)
    vv = jnp.concatenate([kvp_ref[0, 0, :, B_GW:], kvc_ref[0, 0, :, B_GW:]], axis=0).astype(BF16)
    qi = lax.broadcasted_iota(jnp.int32, (nk, 2 * nk), 0)
    kj = lax.broadcasted_iota(jnp.int32, (nk, 2 * nk), 1)
    valid = (kj >= qi) & (kj <= qi + nk) & ((kj >= nk) | (i > 0))
    masks = _head_lane_mask(nk)
    o_acc = jnp.zeros((nk, B_GW), F32)
    l_acc = jnp.zeros((nk, B_GW), F32)
    zq = jnp.zeros_like(q)
    for h in range(B_HEADS):
        s = _dot_nt(jnp.where(masks[h], q, zq), kk)
        s = jnp.where(valid, s, -jnp.inf)
        m = jnp.max(s, axis=-1, keepdims=True)
        p = jnp.exp(s - m)
        den = jnp.sum(p, axis=-1, keepdims=True)
        lse = m + jnp.log(den)
        oh = jnp.dot((p / den).astype(BF16), vv, preferred_element_type=F32)
        o_acc = jnp.where(masks[h], oh, o_acc)
        l_acc = jnp.where(masks[h], lse, l_acc)
    o_ref[0, 0] = o_acc
    l_ref[0, 0] = l_acc


def _attn_prompt_call(q, kv):
    B, dil, n, _ = q.shape
    blk = lambda w: pl.BlockSpec((1, 1, B_NK, w), lambda b, r, i: (b, r, i, 0))
    out = jax.ShapeDtypeStruct((B, dil, n, B_GW), F32)
    return pl.pallas_call(
        _attn_prompt_kernel,
        grid=(B, dil, n // B_NK),
        in_specs=[blk(B_GW),
                  pl.BlockSpec((1, 1, B_NK, 2 * B_GW), lambda b, r, i: (b, r, jnp.maximum(i - 1, 0), 0)),
                  blk(2 * B_GW)],
        out_specs=[blk(B_GW), blk(B_GW)],
        out_shape=[out, out],
        compiler_params=_cparams(("parallel", "parallel", "arbitrary")),
        name="attn_prompt_d%d" % dil,
    )(q, kv, kv)


AS_QB = 16
AS_CACHE_BYTES = 16 * 1024 * 1024
AS_R = 8
AS_NEW = 16


def _attn_sample_kernel(q_ref, kvn_ref, c_ref, o_ref, l_ref, *, dil, g, sub_blocks):
    T = q_ref.shape[0]
    cb, Lb = c_ref.shape[1], c_ref.shape[3]
    R, NR = AS_R, B_HEADS * AS_R
    sub = pl.program_id(0) % sub_blocks
    row = lax.broadcasted_iota(jnp.int32, (NR, B_GW), 0)
    lane = lax.broadcasted_iota(jnp.int32, (NR, B_GW), 1)
    hmask = (lane // B_HD) == (row // R)

    def query_of(shape):
        t = lax.broadcasted_iota(jnp.int32, shape, 0) % R
        return jnp.where(t < T, t, 0)

    pos = lax.broadcasted_iota(jnp.int32, (NR, Lb), 1)
    tn = lax.broadcasted_iota(jnp.int32, (NR, AS_NEW), 1)
    if dil == 1:
        valid_c = pos >= query_of((NR, Lb))
        valid_n = tn <= query_of((NR, AS_NEW))
    else:
        valid_c = (pos % dil) == query_of((NR, Lb))
        valid_n = tn == query_of((NR, AS_NEW))

    for bi in range(cb):
        b = sub * cb + bi
        qrows = jnp.concatenate([q_ref[t, pl.ds(b, 1), g * B_GW:(g + 1) * B_GW] for t in range(T)]
                                + [jnp.zeros((R - T, B_GW), F32)], axis=0)
        qbd = jnp.where(hmask, jnp.concatenate([qrows] * B_HEADS, axis=0), 0.0)
        new = jnp.concatenate([kvn_ref[t, pl.ds(b, 1), :] for t in range(T)]
                              + [jnp.zeros((AS_NEW - T, 2 * B_GW), F32)], axis=0)
        s = jnp.where(valid_c, _dot(qbd, c_ref[0, bi, 0:B_GW, :]), -jnp.inf)
        sn = jnp.where(valid_n, _dot_nt(qbd, new[:, 0:B_GW]), -jnp.inf)
        m = jnp.maximum(jnp.max(s, axis=-1, keepdims=True), jnp.max(sn, axis=-1, keepdims=True))
        p = jnp.exp(s - m)
        pn = jnp.exp(sn - m)
        den = jnp.sum(p, axis=-1, keepdims=True) + jnp.sum(pn, axis=-1, keepdims=True)
        o = _dot_nt(p / den, c_ref[0, bi, B_GW:2 * B_GW, :]) + _dot(pn / den, new[:, B_GW:])
        om = jnp.where(hmask, o, 0.0)
        lm = jnp.where(hmask, m + jnp.log(den), 0.0)
        ob = om[0:R]
        lb = lm[0:R]
        for h in range(1, B_HEADS):
            ob = ob + om[h * R:(h + 1) * R]
            lb = lb + lm[h * R:(h + 1) * R]
        for t in range(T):
            o_ref[t, pl.ds(b, 1), :] = ob[t:t + 1]
            l_ref[t, pl.ds(b, 1), :] = lb[t:t + 1]


def _attn_sample_call(q, kvn, cache_t, layer, g, dil):
    T, B, _ = q.shape
    Lb = cache_t.shape[3]
    qb = min(AS_QB, B)
    cb = max(1, min(qb, AS_CACHE_BYTES // (2 * B_GW * Lb * 4)))
    sub_blocks = qb // cb
    out = jax.ShapeDtypeStruct((T, B, B_GW), F32)
    row_spec = lambda w: pl.BlockSpec((T, qb, w), lambda i: (0, i // sub_blocks, 0))
    return pl.pallas_call(
        functools.partial(_attn_sample_kernel, dil=dil, g=g, sub_blocks=sub_blocks),
        grid=(B // cb,),
        in_specs=[row_spec(B_W), row_spec(2 * B_GW),
                  pl.BlockSpec((1, cb, 2 * B_GW, Lb), lambda i: (layer, i, 0, 0))],
        out_specs=[row_spec(B_GW), row_spec(B_GW)],
        out_shape=[out, out],
        compiler_params=_cparams(("arbitrary",)),
        name="attn_sample_d%d" % dil,
    )(q, kvn, cache_t)


def _shift_rows(x, s, fill):
    n = x.shape[0]
    rolled = pltpu.roll(x, s, 0)
    r = lax.broadcasted_iota(jnp.int32, x.shape, 0)
    return jnp.where(r < s, fill, rolled)


def _rglru_kernel(x_ref, g_ref, prev_ref, h0_ref, cw_ref, cb_ref, wr_ref, br_ref, wi_ref, bi_ref, lam_ref,
                  o_ref, hl_ref, xx_ref, hc_ref, *, unit, hp):
    i = pl.program_id(1)
    tm = x_ref.shape[1]
    halo = (C_CONV - 1) * unit

    @pl.when(i == 0)
    def _():
        xx_ref[hp - halo:hp, :] = prev_ref[0]
        hc_ref[...] = h0_ref[0]

    x = x_ref[0]
    xx_ref[hp:hp + tm, :] = x
    y = cb_ref[...] + xx_ref[hp - halo:hp - halo + tm, :] * cw_ref[0:1, :]
    for j in range(1, C_CONV):
        y = y + xx_ref[hp - halo + j * unit:hp - halo + j * unit + tm, :] * cw_ref[j:j + 1, :]
    tail = xx_ref[hp + tm - halo:hp + tm, :]
    xx_ref[hp - halo:hp, :] = tail

    r = _sigmoid(_dot(y, wr_ref[...]) + br_ref[...])
    ig = _sigmoid(_dot(y, wi_ref[...]) + bi_ref[...])
    log_a = -C_POW * r * _softplus(-lam_ref[...])
    a = jnp.exp(log_a)
    th = jnp.tanh(log_a)
    bx = jnp.sqrt(-2.0 * th / (1.0 - th)) * (ig * y)
    hc = hc_ref[...]
    if unit == 1:
        rr = lax.broadcasted_iota(jnp.int32, bx.shape, 0)
        bx = jnp.where(rr < 1, a * hc + bx, bx)
    elif unit == tm:
        bx = a * hc + bx
    else:
        bx = jnp.concatenate([a[0:unit] * hc + bx[0:unit], bx[unit:]], axis=0)
    s = unit
    while s < tm:
        a_sh = _shift_rows(a, s, 1.0)
        b_sh = _shift_rows(bx, s, 0.0)
        bx = a * b_sh + bx
        a = a * a_sh
        s *= 2
    h = bx
    hl = h[tm - unit:tm, :]
    hc_ref[...] = hl
    hl_ref[0] = hl
    o_ref[0] = h * jax.nn.gelu(g_ref[0])


def _rglru_call(proj3, prev, h0, cw, cb, wr, br, wi, bi, lam, unit, tm):
    nseq, rows, _ = proj3.shape
    halo = (C_CONV - 1) * unit
    hp = -(-halo // 8) * 8
    assert unit == 1 or unit % 8 == 0
    vec = lambda: pl.BlockSpec((1, C_WIDTH), lambda n, i: (0, 0))
    return pl.pallas_call(
        functools.partial(_rglru_kernel, unit=unit, hp=hp),
        grid=(nseq, rows // tm),
        in_specs=[pl.BlockSpec((1, tm, C_WIDTH), lambda n, i: (n, i, OFF_XC // C_WIDTH)),
                  pl.BlockSpec((1, tm, C_WIDTH), lambda n, i: (n, i, OFF_GC // C_WIDTH)),
                  pl.BlockSpec((1, halo, C_WIDTH), lambda n, i: (n, 0, 0)),
                  pl.BlockSpec((1, unit, C_WIDTH), lambda n, i: (n, 0, 0)),
                  pl.BlockSpec((C_CONV, C_WIDTH), lambda n, i: (0, 0)), vec(),
                  pl.BlockSpec((C_WIDTH, C_WIDTH), lambda n, i: (0, 0)), vec(),
                  pl.BlockSpec((C_WIDTH, C_WIDTH), lambda n, i: (0, 0)), vec(), vec()],
        out_specs=[pl.BlockSpec((1, tm, C_WIDTH), lambda n, i: (n, i, 0)),
                   pl.BlockSpec((1, unit, C_WIDTH), lambda n, i: (n, 0, 0))],
        out_shape=[jax.ShapeDtypeStruct((nseq, rows, C_WIDTH), F32),
                   jax.ShapeDtypeStruct((nseq, unit, C_WIDTH), F32)],
        scratch_shapes=[pltpu.VMEM((hp + tm, C_WIDTH), F32), pltpu.VMEM((unit, C_WIDTH), F32)],
        compiler_params=_cparams(("parallel", "arbitrary")),
        name="rglru_u%d" % unit,
    )(proj3, proj3, prev, h0, cw, cb, wr, br, wi, bi, lam)


def _merge_kernel(x_ref, ga_ref, gb_ref, gc_ref, oa_ref, oc_ref,
                  o0_ref, o1_ref, o2_ref, l0_ref, l1_ref, l2_ref,
                  wpa_ref, wpb_ref, wpc_ref, wo_ref, g_ref, b_ref, out_ref, il_ref, *, dils):
    tm = x_ref.shape[1]

    def rows_in_order(ref, dil):
        if dil == 1:
            return ref[0, 0]
        for r in range(dil):
            for j in range(il_ref.shape[0]):
                il_ref[j, pl.ds(r, tm // dil, stride=dil), :] = ref[0, r, :, j * 128:(j + 1) * 128]
        return jnp.concatenate([il_ref[j] for j in range(il_ref.shape[0])], axis=1)

    o0, o1, o2 = (rows_in_order(r, d) for r, d in zip((o0_ref, o1_ref, o2_ref), dils))
    l0, l1, l2 = (rows_in_order(r, d) for r, d in zip((l0_ref, l1_ref, l2_ref), dils))
    m = jnp.maximum(jnp.maximum(l0, l1), l2)
    e0, e1, e2 = jnp.exp(l0 - m), jnp.exp(l1 - m), jnp.exp(l2 - m)
    den = e0 + e1 + e2
    ob = (e0 / den) * o0 + (e1 / den) * o1 + (e2 / den) * o2
    merged = (_sigmoid(ga_ref[0]) * _dot(oa_ref[0], wpa_ref[...])
              + _sigmoid(gb_ref[0]) * _dot(ob, wpb_ref[...])
              + _sigmoid(gc_ref[0]) * _dot(oc_ref[0], wpc_ref[...]))
    mix = _dot(merged, wo_ref[...])
    out_ref[0] = _layer_norm(DN_ALPHA * x_ref[0] + mix, g_ref[...], b_ref[...])


def _merge_call(x3, proj3, oa, oc, obs, lses, wpa, wpb, wpc, wo, g, b, tm):
    nseq, rows, _ = x3.shape
    dils = tuple(o.shape[1] for o in obs)
    row_spec = lambda w, j=0: pl.BlockSpec((1, tm, w), lambda n, i: (n, i, j))
    grp_spec = lambda d: pl.BlockSpec((1, d, tm // d, B_GW), lambda n, i: (n, 0, i, 0))
    full = lambda a: pl.BlockSpec(a.shape, lambda n, i: (0, 0))
    return pl.pallas_call(
        functools.partial(_merge_kernel, dils=dils),
        grid=(nseq, rows // tm),
        in_specs=[row_spec(D_MODEL), row_spec(D_MODEL, 0), row_spec(D_MODEL, 1), row_spec(D_MODEL, 2),
                  row_spec(A_VW), row_spec(C_WIDTH)]
                 + [grp_spec(d) for d in dils] * 2
                 + [full(wpa), full(wpb), full(wpc), full(wo), full(g), full(b)],
        out_specs=row_spec(D_MODEL),
        out_shape=jax.ShapeDtypeStruct((nseq, rows, D_MODEL), F32),
        scratch_shapes=[pltpu.VMEM((B_GW // 128, tm, 128), F32)],
        compiler_params=_cparams(("parallel", "parallel")),
        name="merge_ln",
    )(x3, proj3, proj3, proj3, oa, oc, *obs, *lses, wpa, wpb, wpc, wo, g, b)


def _ffn_kernel(x_ref, prev_ref, wu_ref, cw_ref, cb_ref, wd_ref, g_ref, b_ref,
                o_ref, st_ref, xx_ref, *, unit, hp):
    i = pl.program_id(1)
    tm = x_ref.shape[1]
    halo = (F_CONV - 1) * unit

    @pl.when(i == 0)
    def _():
        xx_ref[hp - halo:hp, :] = prev_ref[0]

    x = x_ref[0]
    xb = x.astype(BF16)
    f = jnp.zeros((tm, D_MODEL), F32)
    for c0 in range(0, D_FF, FF_CHUNK):
        c1 = c0 + FF_CHUNK
        gate = jnp.dot(xb, wu_ref[:, c0:c1], preferred_element_type=F32)
        up = jnp.dot(xb, wu_ref[:, D_FF + c0:D_FF + c1], preferred_element_type=F32)
        xx_ref[hp:hp + tm, c0:c1] = gate
        y = cb_ref[:, c0:c1] + xx_ref[hp - halo:hp - halo + tm, c0:c1] * cw_ref[0:1, c0:c1]
        for j in range(1, F_CONV):
            y = y + xx_ref[hp - halo + j * unit:hp - halo + j * unit + tm, c0:c1] * cw_ref[j:j + 1, c0:c1]
        tail = xx_ref[hp + tm - halo:hp + tm, c0:c1]
        xx_ref[hp - halo:hp, c0:c1] = tail
        st_ref[0, :, c0:c1] = tail
        f = f + _dot(_silu(y) * up, wd_ref[c0:c1, :])
    o_ref[0] = _layer_norm(DN_ALPHA * x + f, g_ref[...], b_ref[...])


def _ffn_call(x3, prev, wu, cw, cb, wd, g, b, unit, tm):
    nseq, rows, _ = x3.shape
    halo = (F_CONV - 1) * unit
    hp = -(-halo // 8) * 8
    const = lambda a: pl.BlockSpec(a.shape, lambda n, i: (0, 0), pipeline_mode=pl.Buffered(1))
    return pl.pallas_call(
        functools.partial(_ffn_kernel, unit=unit, hp=hp),
        grid=(nseq, rows // tm),
        in_specs=[pl.BlockSpec((1, tm, D_MODEL), lambda n, i: (n, i, 0)),
                  pl.BlockSpec((1, halo, D_FF), lambda n, i: (n, 0, 0)),
                  const(wu), const(cw), const(cb), const(wd), const(g), const(b)],
        out_specs=[pl.BlockSpec((1, tm, D_MODEL), lambda n, i: (n, i, 0)),
                   pl.BlockSpec((1, halo, D_FF), lambda n, i: (n, 0, 0))],
        out_shape=[jax.ShapeDtypeStruct((nseq, rows, D_MODEL), F32),
                   jax.ShapeDtypeStruct((nseq, halo, D_FF), F32)],
        scratch_shapes=[pltpu.VMEM((hp + tm, D_FF), F32)],
        compiler_params=_cparams(("parallel", "arbitrary")),
        name="conv_ffn_u%d" % unit,
    )(x3, prev, wu, cw, cb, wd, g, b)


def _block_diag(w):
    nblk, bw, _ = w.shape
    eye = jnp.eye(nblk, dtype=w.dtype)
    return (eye[:, None, :, None] * w[:, :, None, :]).reshape(nblk * bw, nblk * bw)


def _layer_params(l, w_in, a_conv_w, a_A_log, a_dt_bias, a_norm_w, c_conv_w, c_conv_b, c_w_r, c_b_r,
                  c_w_i, c_b_i, c_lam, w_pa, w_pb, w_pc, w_o, ln1_g, ln1_b, f_up, f_conv_w, f_conv_b,
                  f_down, ln2_g, ln2_b):
    w = w_in[l]
    o_b = A_QKV
    o_a = o_b + A_HEADS
    o_z = o_a + A_HEADS
    o_qb = o_z + A_VW
    o_xc = o_qb + 3 * B_W
    o_gc = o_xc + C_WIDTH
    o_gt = o_gc + C_WIDTH
    pad = jnp.zeros((D_MODEL, N_PROJ - OFF_BA - 2 * A_HEADS), w.dtype)
    wp = jnp.concatenate([w[:, o_gt:o_gt + 3 * D_MODEL], w[:, o_xc:o_gc], w[:, o_gc:o_gt], w[:, o_z:o_qb],
                          w[:, 0:o_b], w[:, o_qb:o_xc], w[:, o_b:o_z], pad], axis=1).astype(BF16)
    gp = jnp.zeros((2, 128), F32)
    gp = gp.at[0, A_HEADS:2 * A_HEADS].set(a_A_log[l]).at[1, A_HEADS:2 * A_HEADS].set(a_dt_bias[l])
    row = lambda v: v.reshape(1, -1)
    return dict(
        wp=wp, a_cw=a_conv_w[l], gp=gp, a_nw=row(a_norm_w[l]),
        c_cw=c_conv_w[l], c_cb=row(c_conv_b[l]),
        c_wr=_block_diag(c_w_r[l]).astype(BF16), c_br=row(c_b_r[l]),
        c_wi=_block_diag(c_w_i[l]).astype(BF16), c_bi=row(c_b_i[l]), c_lam=row(c_lam[l]),
        wpa=w_pa[l].astype(BF16), wpb=w_pb[l].astype(BF16), wpc=w_pc[l].astype(BF16), wo=w_o[l].astype(BF16),
        ln1_g=row(ln1_g[l]), ln1_b=row(ln1_b[l]),
        f_up=f_up[l].astype(BF16), f_cw=f_conv_w[l], f_cb=row(f_conv_b[l]), f_down=f_down[l].astype(BF16),
        ln2_g=row(ln2_g[l]), ln2_b=row(ln2_b[l]))


def _layer_tail(x3, proj3, o_a, obs, lses, unit, p, c_prev, c_h0, f_prev, tm_merge, tm_seq):
    o_c, c_hl = _rglru_call(proj3, c_prev, c_h0, p["c_cw"], p["c_cb"], p["c_wr"], p["c_br"], p["c_wi"],
                            p["c_bi"], p["c_lam"], unit, tm_seq)
    x1 = _merge_call(x3, proj3, o_a, o_c, obs, lses, p["wpa"], p["wpb"], p["wpc"], p["wo"],
                     p["ln1_g"], p["ln1_b"], tm_merge)
    x_out, f_state = _ffn_call(x1, f_prev, p["f_up"], p["f_cw"], p["f_cb"], p["f_down"],
                               p["ln2_g"], p["ln2_b"], unit, tm_seq)
    return x_out, c_hl, f_state


def kernel(x_prompt, x_sample, state_a_conv, state_a_rec, cache_b_w128, cache_b_w512, cache_b_w2048,
           state_c_conv, state_c_h, state_f_conv, ln_in_g, ln_in_b, w_in, a_conv_w, a_A_log, a_dt_bias,
           a_norm_w, c_conv_w, c_conv_b, c_w_r, c_b_r, c_w_i, c_b_i, c_lam, w_pa, w_pb, w_pc, w_o,
           ln1_g, ln1_b, f_up, f_conv_w, f_conv_b, f_down, ln2_g, ln2_b):
    Bp, Sp, _ = x_prompt.shape
    Bs, Ts, _ = x_sample.shape
    caches = (cache_b_w128, cache_b_w512, cache_b_w2048)
    for (win, dil), cache in zip(B_GROUPS, caches):
        assert win // dil == B_NK and cache.shape[2] == win and Sp % (dil * B_NK) == 0

    caches_t = [jnp.transpose(c, (0, 1, 3, 4, 5, 2)).reshape(DEPTH, Bs, 2 * B_GW, c.shape[2]) for c in caches]
    Rs = Ts * Bs

    lng, lnb = ln_in_g.reshape(1, -1), ln_in_b.reshape(1, -1)
    hp = _ln_call(x_prompt.reshape(Bp * Sp, D_MODEL), lng, lnb, 512)
    xs_tm = jnp.swapaxes(x_sample, 0, 1).reshape(Rs, D_MODEL)
    hs = _ln_call(xs_tm, lng, lnb, Rs)

    tabs_p = _rope_tables(jnp.arange(Sp))
    tabs_s = _rope_tables(jnp.repeat(PAST_LEN + jnp.arange(Ts), Bs))

    zeros_c = jnp.zeros((Bp, C_CONV - 1, C_WIDTH), F32)
    zeros_h = jnp.zeros((Bp, 1, C_WIDTH), F32)
    zeros_f = jnp.zeros((Bp, F_CONV - 1, D_FF), F32)

    outs_p, outs_s = [], []
    for l in range(DEPTH):
        p = _layer_params(l, w_in, a_conv_w, a_A_log, a_dt_bias, a_norm_w, c_conv_w, c_conv_b, c_w_r, c_b_r,
                          c_w_i, c_b_i, c_lam, w_pa, w_pb, w_pc, w_o, ln1_g, ln1_b, f_up, f_conv_w,
                          f_conv_b, f_down, ln2_g, ln2_b)

        proj3 = _proj_call(hp, p["wp"], 1024).reshape(Bp, Sp, N_PROJ)
        o_a, a_rec = _gdn_prompt_call(proj3, p["a_cw"], p["gp"], p["a_nw"])
        rp = _rope_prompt_call(proj3, tabs_p, 256)
        res = [_attn_prompt_call(rp[g], rp[3 + g]) for g in range(len(B_GROUPS))]
        hp3, c_hl, f_st = _layer_tail(hp.reshape(Bp, Sp, D_MODEL), proj3, o_a, [r[0] for r in res],
                                      [r[1] for r in res], 1, p, zeros_c, zeros_h, zeros_f, 256, 256)
        hp = hp3.reshape(Bp * Sp, D_MODEL)
        kv_rows_p = [t.reshape(Bp, 2, B_HEADS, B_HD, t.shape[-1]).transpose(0, 4, 1, 2, 3) for t in rp[6:9]]
        outs_p.append((
            proj3[:, Sp - (A_CONV - 1):, OFF_QKVA:OFF_QKVA + A_QKV],
            a_rec, kv_rows_p[0], kv_rows_p[1], kv_rows_p[2],
            proj3[:, Sp - (C_CONV - 1):, OFF_XC:OFF_XC + C_WIDTH],
            c_hl[:, 0],
            f_st))

        proj3 = _proj_call(hs, p["wp"], Rs).reshape(1, Rs, N_PROJ)
        pt = proj3.reshape(Ts, Bs, N_PROJ)
        a_st = jnp.swapaxes(state_a_conv[l], 0, 1)
        o_a, a_rec = _gdn_sample_call(pt, a_st, state_a_rec, l, p["a_cw"], p["gp"], p["a_nw"])
        rs = _rope_sample_call(proj3, tabs_s, Bs)
        q3 = rs[0].reshape(Ts, Bs, B_W)
        res = [_attn_sample_call(q3, rs[1 + g].reshape(Ts, Bs, 2 * B_GW), caches_t[g], l, g, dil)
               for g, (_, dil) in enumerate(B_GROUPS)]
        c_prev = jnp.swapaxes(state_c_conv[l], 0, 1).reshape(1, (C_CONV - 1) * Bs, C_WIDTH)
        f_prev = jnp.swapaxes(state_f_conv[l], 0, 1).reshape(1, (F_CONV - 1) * Bs, D_FF)
        hs3, c_hl, f_st = _layer_tail(hs.reshape(1, Rs, D_MODEL), proj3, o_a.reshape(1, Rs, A_VW),
                                      [r[0].reshape(1, 1, Rs, B_GW) for r in res],
                                      [r[1].reshape(1, 1, Rs, B_GW) for r in res],
                                      Bs, p, c_prev, state_c_h[l][None], f_prev, min(256, Rs), Rs)
        hs = hs3.reshape(Rs, D_MODEL)
        tm2bm = lambda a: jnp.swapaxes(a, 0, 1)
        kv_rows_s = [t.reshape(Ts, 2, B_HEADS, B_HD, Bs).transpose(4, 0, 1, 2, 3) for t in rs[4:7]]
        outs_s.append((
            tm2bm(pt[Ts - (A_CONV - 1):, :, OFF_QKVA:OFF_QKVA + A_QKV]),
            a_rec, kv_rows_s[0], kv_rows_s[1], kv_rows_s[2],
            tm2bm(pt[Ts - (C_CONV - 1):, :, OFF_XC:OFF_XC + C_WIDTH]),
            c_hl[0],
            tm2bm(f_st.reshape(F_CONV - 1, Bs, D_FF))))

    stack = lambda outs: [jnp.stack(s, 0) for s in zip(*outs)]
    y_p = hp.reshape(Bp, Sp, D_MODEL)
    y_s = jnp.swapaxes(hs.reshape(Ts, Bs, D_MODEL), 0, 1)
    return (y_p, y_s, *stack(outs_p), *stack(outs_s))
```

```python
import functools
import math

import numpy as np
import jax
import jax.numpy as jnp
from jax import lax
from jax.experimental import pallas as pl
from jax.experimental.pallas import tpu as pltpu

F32 = jnp.float32
BF16 = jnp.bfloat16

D_MODEL = 1024
DEPTH = 4
PAST_LEN = 2048
A_HEADS, A_DK, A_DV, A_CONV, A_CHUNK = 4, 128, 128, 4, 64
B_GROUPS = ((128, 1), (512, 4), (2048, 16))
B_HEADS, B_HD = 4, 64
B_ROT = B_HD // 4
ROPE_THETA = 500000.0
B_NK = 128
C_WIDTH, C_BLOCKS, C_CONV, C_POW = 512, 8, 4, 8.0
C_BW = C_WIDTH // C_BLOCKS
D_FF = ((8 * D_MODEL // 3 + 255) // 256) * 256
F_CONV = 3
FF_CHUNK = 1408
DN_ALPHA = (2 * DEPTH) ** 0.25
LN_EPS = 1e-5
NORM_EPS = 1e-6

A_QK = A_HEADS * A_DK
A_VW = A_HEADS * A_DV
A_QKV = 2 * A_QK + A_VW
B_GW = B_HEADS * B_HD
B_W = len(B_GROUPS) * B_GW

OFF_GATES = 0
OFF_XC = 3072
OFF_GC = 3584
OFF_Z = 4096
OFF_QKVA = 4608
OFF_QB = 6144
OFF_BA = 8448
N_PROJ = 8704
PROJ_TN = 2176

VMEM_LIMIT = 56 * 1024 * 1024


def _cparams(sem):
    return pltpu.CompilerParams(dimension_semantics=sem, vmem_limit_bytes=VMEM_LIMIT)


def _sigmoid(x):
    return 1.0 / (1.0 + jnp.exp(-x))


def _silu(x):
    return x * _sigmoid(x)


def _softplus(x):
    return jnp.maximum(x, 0.0) + jnp.log1p(jnp.exp(-jnp.abs(x)))


def _layer_norm(x, g, b):
    mu = jnp.mean(x, axis=-1, keepdims=True)
    xc = x - mu
    var = jnp.mean(xc * xc, axis=-1, keepdims=True)
    return xc * lax.rsqrt(var + LN_EPS) * g + b


def _dot(a, b):
    return jnp.dot(a.astype(BF16), b.astype(BF16), preferred_element_type=F32)


def _dot_nt(a, b):
    return lax.dot_general(a.astype(BF16), b.astype(BF16), (((1,), (1,)), ((), ())),
                           preferred_element_type=F32)


def _dot_tn(a, b):
    return lax.dot_general(a.astype(BF16), b.astype(BF16), (((0,), (0,)), ((), ())),
                           preferred_element_type=F32)


def _split2(a):
    hi = a.astype(BF16)
    lo = (a - hi.astype(F32)).astype(BF16)
    return hi, lo


def _dot3(a, b):
    ah, al = _split2(a)
    bh, bl = _split2(b)
    d = lambda x, y: jnp.dot(x, y, preferred_element_type=F32)
    return d(ah, bh) + (d(ah, bl) + d(al, bh))


def _ln_kernel(x_ref, g_ref, b_ref, o_ref):
    o_ref[...] = _layer_norm(x_ref[...], g_ref[...], b_ref[...])


def _ln_call(x, g, b, tm):
    rows = x.shape[0]
    return pl.pallas_call(
        _ln_kernel,
        grid=(rows // tm,),
        in_specs=[pl.BlockSpec((tm, D_MODEL), lambda i: (i, 0)),
                  pl.BlockSpec((1, D_MODEL), lambda i: (0, 0)),
                  pl.BlockSpec((1, D_MODEL), lambda i: (0, 0))],
        out_specs=pl.BlockSpec((tm, D_MODEL), lambda i: (i, 0)),
        out_shape=jax.ShapeDtypeStruct((rows, D_MODEL), F32),
        compiler_params=_cparams(("parallel",)),
        name="entry_ln",
    )(x, g, b)


def _proj_kernel(x_ref, w_ref, o_ref, xb_ref):
    @pl.when(pl.program_id(1) == 0)
    def _():
        xb_ref[...] = x_ref[...].astype(BF16)

    o_ref[...] = jnp.dot(xb_ref[...], w_ref[...], preferred_element_type=F32)


def _proj_call(x, w, tm):
    rows = x.shape[0]
    return pl.pallas_call(
        _proj_kernel,
        grid=(rows // tm, N_PROJ // PROJ_TN),
        in_specs=[pl.BlockSpec((tm, D_MODEL), lambda i, j: (i, 0)),
                  pl.BlockSpec((D_MODEL, PROJ_TN), lambda i, j: (0, j))],
        out_specs=pl.BlockSpec((tm, PROJ_TN), lambda i, j: (i, j)),
        out_shape=jax.ShapeDtypeStruct((rows, N_PROJ), F32),
        scratch_shapes=[pltpu.VMEM((tm, D_MODEL), BF16)],
        compiler_params=_cparams(("parallel", "arbitrary")),
        name="in_proj",
    )(x, w)


def _gdn_prompt_kernel(x_ref, z_ref, ba_ref, cw_ref, gp_ref, nw_ref,
                       o_ref, sfin_ref, xx_ref, s_ref):
    c = pl.program_id(0)
    C = A_CHUNK
    nb = x_ref.shape[0]

    @pl.when(c == 0)
    def _():
        xx_ref[:, 0:8, :] = jnp.zeros((nb, 8, A_QKV), F32)
        s_ref[...] = jnp.zeros(s_ref.shape, F32)

    ri = lax.broadcasted_iota(jnp.int32, (C, C), 0)
    ci = lax.broadcasted_iota(jnp.int32, (C, C), 1)
    tril = ri >= ci
    strict = ri > ci
    eye = (ri == ci).astype(F32)
    trilf = tril.astype(F32)
    nw = nw_ref[...]

    ch = []
    for b in range(nb):
        x = x_ref[b]
        xx_ref[b, 8:8 + C, :] = x
        y = xx_ref[b, 5:5 + C, :] * cw_ref[0:1, :]
        for j in range(1, A_CONV):
            y = y + xx_ref[b, 5 + j:5 + j + C, :] * cw_ref[j:j + 1, :]
        xx_ref[b, 0:8, :] = x[C - 8:C, :]
        y = _silu(y)
        ba = ba_ref[b]
        beta_all = _sigmoid(ba)
        g_all = -jnp.exp(gp_ref[0:1, :]) * _softplus(ba + gp_ref[1:2, :])
        gc = _dot3(trilf, g_all)
        gct = jnp.concatenate([gc, jnp.zeros((128 - C, 128), F32)], axis=0).T
        for h in range(A_HEADS):
            q = y[:, h * A_DK:(h + 1) * A_DK]
            k = y[:, A_QK + h * A_DK:A_QK + (h + 1) * A_DK]
            v = y[:, 2 * A_QK + h * A_DV:2 * A_QK + (h + 1) * A_DV]
            q = q * lax.rsqrt(jnp.sum(q * q, axis=-1, keepdims=True) + NORM_EPS) * (A_DK ** -0.5)
            k = k * lax.rsqrt(jnp.sum(k * k, axis=-1, keepdims=True) + NORM_EPS)
            beta = beta_all[:, h:h + 1]
            gcol = gc[:, 4 + h:5 + h]
            grow = gct[4 + h:5 + h, 0:C]
            decay = jnp.where(tril, jnp.exp(jnp.where(tril, gcol - grow, 0.0)), 0.0)
            eg = jnp.exp(gcol)
            glast = gcol[C - 1:C, :]
            ch.append(dict(b=b, h=h, q=q, k=k, kb=k * beta, vb=v * beta, decay=decay, eg=eg,
                           k_dec=k * jnp.exp(glast - gcol), q_dec=q * eg, eglast=jnp.exp(glast)))
    for d in ch:
        d["m"] = -jnp.where(strict, _dot_nt(d["kb"], d["k"]) * d["decay"], 0.0)
        d["t"] = eye + d["m"]
    for d in ch:
        d["qk"] = jnp.where(tril, _dot_nt(d["q"], d["k"]) * d["decay"], 0.0)
    for _ in range(5):
        for d in ch:
            d["m"] = _dot(d["m"], d["m"])
        for d in ch:
            d["t"] = d["t"] + _dot(d["t"], d["m"])
    for d in ch:
        d["u"] = _dot(d["t"], d["vb"])
        d["w"] = _dot(d["t"], d["kb"] * d["eg"])
    for d in ch:
        d["s"] = s_ref[d["b"], d["h"]]
        d["v_new"] = d["u"] - _dot(d["w"], d["s"])
    for d in ch:
        d["o"] = _dot(d["q_dec"], d["s"]) + _dot(d["qk"], d["v_new"])
        s_ref[d["b"], d["h"]] = d["s"] * d["eglast"] + _dot_tn(d["k_dec"], d["v_new"])
    for d in ch:
        b, h, o = d["b"], d["h"], d["o"]
        o = o * lax.rsqrt(jnp.mean(o * o, axis=-1, keepdims=True) + NORM_EPS) * nw
        o_ref[b, :, h * A_DV:(h + 1) * A_DV] = o * _silu(z_ref[b, :, h * A_DV:(h + 1) * A_DV])

    @pl.when(c == pl.num_programs(0) - 1)
    def _():
        sfin_ref[...] = s_ref[...]


def _gdn_prompt_call(proj3, cw, gp, nw):
    B, S, _ = proj3.shape
    C = A_CHUNK
    return pl.pallas_call(
        _gdn_prompt_kernel,
        grid=(S // C,),
        in_specs=[pl.BlockSpec((B, C, A_QKV), lambda c: (0, c, OFF_QKVA // A_QKV)),
                  pl.BlockSpec((B, C, A_VW), lambda c: (0, c, OFF_Z // A_VW)),
                  pl.BlockSpec((B, C, 128), lambda c: (0, c, OFF_BA // 128)),
                  pl.BlockSpec((A_CONV, A_QKV), lambda c: (0, 0)),
                  pl.BlockSpec((2, 128), lambda c: (0, 0)),
                  pl.BlockSpec((1, A_DV), lambda c: (0, 0))],
        out_specs=[pl.BlockSpec((B, C, A_VW), lambda c: (0, c, 0)),
                   pl.BlockSpec((B, A_HEADS, A_DK, A_DV), lambda c: (0, 0, 0, 0))],
        out_shape=[jax.ShapeDtypeStruct((B, S, A_VW), F32),
                   jax.ShapeDtypeStruct((B, A_HEADS, A_DK, A_DV), F32)],
        scratch_shapes=[pltpu.VMEM((B, 8 + C, A_QKV), F32),
                        pltpu.VMEM((B, A_HEADS, A_DK, A_DV), F32)],
        compiler_params=_cparams(("arbitrary",)),
        name="gdn_prompt",
    )(proj3, proj3, proj3, cw, gp, nw)


GS_BB = 8
GS_C = 16


def _gdn_sample_kernel(x_ref, z_ref, ba_ref, st_ref, s0_ref, cw_ref, gp_ref, nw_ref,
                       o_ref, snew_ref, oscr_ref):
    T = x_ref.shape[0]
    bb = GS_BB
    xx = [st_ref[j] for j in range(A_CONV - 1)] + [x_ref[t] for t in range(T)]
    C = GS_C
    assert T <= C
    qs, ks, kbs, vbs, gcs = [], [], [], [], []
    for t in range(T):
        y = xx[t] * cw_ref[0:1, :]
        for j in range(1, A_CONV):
            y = y + xx[t + j] * cw_ref[j:j + 1, :]
        y = _silu(y)
        ba = ba_ref[t]
        beta_all = _sigmoid(ba)
        g_all = -jnp.exp(gp_ref[0:1, :]) * _softplus(ba + gp_ref[1:2, :])
        gcs.append(g_all if t == 0 else gcs[-1] + g_all)
        qs.append([]), ks.append([]), kbs.append([]), vbs.append([])
        for h in range(A_HEADS):
            q = y[:, h * A_DK:(h + 1) * A_DK]
            k = y[:, A_QK + h * A_DK:A_QK + (h + 1) * A_DK]
            v = y[:, 2 * A_QK + h * A_DV:2 * A_QK + (h + 1) * A_DV]
            q = q * lax.rsqrt(jnp.sum(q * q, axis=-1, keepdims=True) + NORM_EPS) * (A_DK ** -0.5)
            k = k * lax.rsqrt(jnp.sum(k * k, axis=-1, keepdims=True) + NORM_EPS)
            beta = beta_all[:, h:h + 1]
            qs[t].append(q), ks[t].append(k), kbs[t].append(k * beta), vbs[t].append(v * beta)

    ri = lax.broadcasted_iota(jnp.int32, (C, C), 0)
    ci = lax.broadcasted_iota(jnp.int32, (C, C), 1)
    tril = ri >= ci
    strict = ri > ci
    eye = (ri == ci).astype(F32)
    zrows = jnp.zeros((C - T, A_DK), F32)
    rows_of = lambda slabs, b, h: jnp.concatenate([slabs[t][h][b:b + 1, :] for t in range(T)] + [zrows], axis=0)
    ch = []
    for h in range(A_HEADS):
        for b in range(bb):
            gcol = jnp.concatenate([gcs[t][b:b + 1, 4 + h:5 + h] for t in range(T)]
                                   + [gcs[T - 1][b:b + 1, 4 + h:5 + h]] * (C - T), axis=0)
            grow = jnp.sum(eye * gcol, axis=0, keepdims=True)
            decay = jnp.where(tril, jnp.exp(jnp.where(tril, gcol - grow, 0.0)), 0.0)
            eg = jnp.exp(gcol)
            glast = gcol[C - 1:C, :]
            q, k, kb, vb = (rows_of(s, b, h) for s in (qs, ks, kbs, vbs))
            ch.append(dict(b=b, h=h, q=q, k=k, kb=kb, vb=vb, decay=decay, eg=eg,
                           k_dec=k * jnp.exp(glast - gcol), q_dec=q * eg, eglast=jnp.exp(glast)))
    for d in ch:
        d["m"] = -jnp.where(strict, _dot_nt(d["kb"], d["k"]) * d["decay"], 0.0)
        d["qk"] = jnp.where(tril, _dot_nt(d["q"], d["k"]) * d["decay"], 0.0)
    for d in ch:
        d["t"] = eye + d["m"]
    for _ in range(GS_C.bit_length() - 2):
        for d in ch:
            d["m"] = _dot(d["m"], d["m"])
        for d in ch:
            d["t"] = d["t"] + _dot(d["t"], d["m"])
    for d in ch:
        d["u"] = _dot(d["t"], d["vb"])
        d["w"] = _dot(d["t"], d["kb"] * d["eg"])
    for d in ch:
        d["s"] = s0_ref[0, d["b"], d["h"]]
        ws = _dot(jnp.concatenate([d["w"], d["q_dec"]], axis=0), d["s"])
        d["v_new"] = d["u"] - ws[0:C]
        d["qs"] = ws[C:2 * C]
    for d in ch:
        b, h = d["b"], d["h"]
        o = d["qs"] + _dot(d["qk"], d["v_new"])
        snew_ref[b, h] = d["s"] * d["eglast"] + _dot_tn(d["k_dec"], d["v_new"])
        for t in range(T):
            oscr_ref[t, b:b + 1, h * A_DV:(h + 1) * A_DV] = o[t:t + 1, :]
    nw = nw_ref[...]
    for t in range(T):
        z = z_ref[t]
        for h in range(A_HEADS):
            o = oscr_ref[t, :, h * A_DV:(h + 1) * A_DV]
            o = o * lax.rsqrt(jnp.mean(o * o, axis=-1, keepdims=True) + NORM_EPS) * nw
            o_ref[t, :, h * A_DV:(h + 1) * A_DV] = o * _silu(z[:, h * A_DV:(h + 1) * A_DV])


def _gdn_sample_call(proj3, st, s0, layer, cw, gp, nw):
    T, B, _ = proj3.shape
    bb = GS_BB
    return pl.pallas_call(
        _gdn_sample_kernel,
        grid=(B // bb,),
        in_specs=[pl.BlockSpec((T, bb, A_QKV), lambda i: (0, i, OFF_QKVA // A_QKV)),
                  pl.BlockSpec((T, bb, A_VW), lambda i: (0, i, OFF_Z // A_VW)),
                  pl.BlockSpec((T, bb, 128), lambda i: (0, i, OFF_BA // 128)),
                  pl.BlockSpec((A_CONV - 1, bb, A_QKV), lambda i: (0, i, 0)),
                  pl.BlockSpec((1, bb, A_HEADS, A_DK, A_DV), lambda i: (layer, i, 0, 0, 0)),
                  pl.BlockSpec((A_CONV, A_QKV), lambda i: (0, 0)),
                  pl.BlockSpec((2, 128), lambda i: (0, 0)),
                  pl.BlockSpec((1, A_DV), lambda i: (0, 0))],
        out_specs=[pl.BlockSpec((T, bb, A_VW), lambda i: (0, i, 0)),
                   pl.BlockSpec((bb, A_HEADS, A_DK, A_DV), lambda i: (i, 0, 0, 0))],
        out_shape=[jax.ShapeDtypeStruct((T, B, A_VW), F32),
                   jax.ShapeDtypeStruct(s0.shape[1:], F32)],
        scratch_shapes=[pltpu.VMEM((T, bb, A_VW), F32)],
        compiler_params=_cparams(("parallel",)),
        name="gdn_sample",
    )(proj3, proj3, proj3, st, s0, cw, gp, nw)


def _rope_qkv(q_ref, k_ref, v_ref, c_ref, s1_ref, s2_ref):
    reps = B_W // 128
    c = jnp.concatenate([c_ref[0]] * reps, axis=1)
    s1 = jnp.concatenate([s1_ref[0]] * reps, axis=1)
    s2 = jnp.concatenate([s2_ref[0]] * reps, axis=1)

    def rot(x):
        return x * c + pltpu.roll(x, B_W - B_ROT // 2, 1) * s1 + pltpu.roll(x, B_ROT // 2, 1) * s2

    return rot(q_ref[0]) * (B_HD ** -0.5), rot(k_ref[0]), v_ref[0]


def _rope_in_specs(tm):
    qb = OFF_QB // B_W
    tab_spec = pl.BlockSpec((1, tm, 128), lambda n, i: (0, i, 0))
    return [pl.BlockSpec((1, tm, B_W), lambda n, i: (n, i, qb)),
            pl.BlockSpec((1, tm, B_W), lambda n, i: (n, i, qb + 1)),
            pl.BlockSpec((1, tm, B_W), lambda n, i: (n, i, qb + 2)),
            tab_spec, tab_spec, tab_spec]


def _store_chunks(buf_ref, x):
    for j in range(buf_ref.shape[0]):
        buf_ref[j] = x[:, j * 128:(j + 1) * 128]


def _load_chunks_strided(buf_ref, start, size, stride):
    return jnp.concatenate([buf_ref[j, pl.ds(start, size, stride=stride), :] for j in range(buf_ref.shape[0])],
                           axis=1)


def _rope_prompt_kernel(q_ref, k_ref, v_ref, c_ref, s1_ref, s2_ref,
                        q0_ref, q1_ref, q2_ref, kv0_ref, kv1_ref, kv2_ref, t0_ref, t1_ref, t2_ref,
                        qs_ref, kvs_ref, *, tails):
    i = pl.program_id(1)
    tm = q_ref.shape[1]
    q, k, v = _rope_qkv(q_ref, k_ref, v_ref, c_ref, s1_ref, s2_ref)
    outs = ((q0_ref, kv0_ref, t0_ref), (q1_ref, kv1_ref, t1_ref), (q2_ref, kv2_ref, t2_ref))
    for g, ((_, dil), (qo_ref, kvo_ref, to_ref)) in enumerate(zip(B_GROUPS, outs)):
        lo, hi = g * B_GW, (g + 1) * B_GW
        qg = q[:, lo:hi]
        kv = jnp.concatenate([k[:, lo:hi], v[:, lo:hi]], axis=1)
        if dil == 1:
            qo_ref[0, 0] = qg.astype(BF16)
            kvo_ref[0, 0] = kv
        else:
            _store_chunks(qs_ref, qg)
            _store_chunks(kvs_ref, kv)
            for r in range(dil):
                qo_ref[0, r] = _load_chunks_strided(qs_ref, r, tm // dil, dil).astype(BF16)
                kvo_ref[0, r] = _load_chunks_strided(kvs_ref, r, tm // dil, dil)
        first, w = tails[g]

        @pl.when(i >= first)
        def _(to_ref=to_ref, w=w, kv=kv):
            to_ref[0] = kv[tm - w:tm, :].T


def _rope_prompt_call(proj3, tabs, tm):
    B, S, _ = proj3.shape
    nt = S // tm
    out_shape, out_specs, tails = [], [], []
    for width, dtype in ((B_GW, BF16), (2 * B_GW, F32)):
        for _, dil in B_GROUPS:
            out_shape.append(jax.ShapeDtypeStruct((B, dil, S // dil, width), dtype))
            out_specs.append(pl.BlockSpec((1, dil, tm // dil, width), lambda n, i: (n, 0, i, 0)))
    for win, _ in B_GROUPS:
        win = min(win, S)
        w = min(tm, win)
        first = nt - win // w
        tails.append((first, w))
        out_shape.append(jax.ShapeDtypeStruct((B, 2 * B_GW, win), F32))
        out_specs.append(pl.BlockSpec((1, 2 * B_GW, w),
                                      lambda n, i, first=first: (n, 0, jnp.maximum(i - first, 0))))
    return pl.pallas_call(
        functools.partial(_rope_prompt_kernel, tails=tuple(tails)),
        grid=(B, nt),
        in_specs=_rope_in_specs(tm),
        out_specs=out_specs,
        out_shape=out_shape,
        scratch_shapes=[pltpu.VMEM((B_GW // 128, tm, 128), F32), pltpu.VMEM((2 * B_GW // 128, tm, 128), F32)],
        compiler_params=_cparams(("parallel", "arbitrary")),
        name="rope_prompt",
    )(proj3, proj3, proj3, *tabs)


def _rope_sample_kernel(q_ref, k_ref, v_ref, c_ref, s1_ref, s2_ref,
                        qo_ref, kv0_ref, kv1_ref, kv2_ref, t0_ref, t1_ref, t2_ref, *, unit):
    q, k, v = _rope_qkv(q_ref, k_ref, v_ref, c_ref, s1_ref, s2_ref)
    qo_ref[0] = q
    for g, (kvo_ref, to_ref) in enumerate(((kv0_ref, t0_ref), (kv1_ref, t1_ref), (kv2_ref, t2_ref))):
        kv = jnp.concatenate([k[:, g * B_GW:(g + 1) * B_GW], v[:, g * B_GW:(g + 1) * B_GW]], axis=1)
        kvo_ref[0] = kv
        for t in range(q.shape[0] // unit):
            to_ref[t] = kv[t * unit:(t + 1) * unit, :].T


def _rope_sample_call(proj3, tabs, unit):
    _, rows, _ = proj3.shape
    T = rows // unit
    kv_shape = jax.ShapeDtypeStruct((1, rows, 2 * B_GW), F32)
    kv_spec = pl.BlockSpec((1, rows, 2 * B_GW), lambda n, i: (0, 0, 0))
    t_shape = jax.ShapeDtypeStruct((T, 2 * B_GW, unit), F32)
    t_spec = pl.BlockSpec((T, 2 * B_GW, unit), lambda n, i: (0, 0, 0))
    return pl.pallas_call(
        functools.partial(_rope_sample_kernel, unit=unit),
        grid=(1, 1),
        in_specs=_rope_in_specs(rows),
        out_specs=[pl.BlockSpec((1, rows, B_W), lambda n, i: (0, 0, 0))] + [kv_spec] * 3 + [t_spec] * 3,
        out_shape=[jax.ShapeDtypeStruct((1, rows, B_W), F32)] + [kv_shape] * 3 + [t_shape] * 3,
        compiler_params=_cparams(("arbitrary", "arbitrary")),
        name="rope_sample",
    )(proj3, proj3, proj3, *tabs)


def _rope_tables(pos):
    half = B_ROT // 2
    inv = ROPE_THETA ** (-jnp.arange(half, dtype=F32) / half)
    ang = pos.astype(F32)[:, None] * inv
    cos, sin = jnp.cos(ang), jnp.sin(ang)
    rows = pos.shape[0]
    one = jnp.ones((rows, B_HD - B_ROT), F32)
    zero = jnp.zeros((rows, half), F32)
    zrest = jnp.zeros((rows, B_HD - B_ROT), F32)
    c = jnp.concatenate([cos, cos, one], axis=1)
    s1 = jnp.concatenate([-sin, zero, zrest], axis=1)
    s2 = jnp.concatenate([zero, sin, zrest], axis=1)
    tile = lambda t: jnp.concatenate([t, t], axis=1)[None]
    return tile(c), tile(s1), tile(s2)


def _head_lane_mask(rows):
    lane = lax.broadcasted_iota(jnp.int32, (rows, B_GW), 1)
    return [(lane >= h * B_HD) & (lane < (h + 1) * B_HD) for h in range(B_HEADS)]


def _attn_prompt_kernel(q_ref, kvp_ref, kvc_ref, o_ref, l_ref):
    i = pl.program_id(2)
    nk = B_NK
    nblk = q_ref.shape[2] // nk
    q = q_ref[0, 0]
    kk = jnp.concatenate([kvp_ref[0, 0, :, 0:B_GW], kvc_ref[0, 0, :, 0:B_GW]], axis=0).astype(BF16)
    vv = jnp.concatenate([kvp_ref[0, 0, :, B_GW:], kvc_ref[0, 0, :, B_GW:]], axis=0).astype(BF16)
    qi = lax.broadcasted_iota(jnp.int32, (nk, 2 * nk), 0)
    kj = lax.broadcasted_iota(jnp.int32, (nk, 2 * nk), 1)
    band = (kj >= qi) & (kj <= qi + nk)
    band_first = band & ((kj >= nk) | (i > 0))
    masks = _head_lane_mask(nk)
    zq = jnp.zeros((nk, B_GW), BF16)
    chains = [(j, h) for j in range(nblk) for h in range(B_HEADS)]
    s, pn, lse, oh = {}, {}, {}, {}
    for j, h in chains:
        s[j, h] = _dot_nt(jnp.where(masks[h], q[j * nk:(j + 1) * nk], zq), kk[j * nk:(j + 2) * nk])
    for j, h in chains:
        sm = jnp.where(band_first if j == 0 else band, s[j, h], -jnp.inf)
        m = jnp.max(sm, axis=-1, keepdims=True)
        p = jnp.exp(sm - m)
        den = jnp.sum(p, axis=-1, keepdims=True)
        lse[j, h] = m + jnp.log(den)
        pn[j, h] = (p / den).astype(BF16)
    for j, h in chains:
        oh[j, h] = jnp.dot(pn[j, h], vv[j * nk:(j + 2) * nk], preferred_element_type=F32)
    for j in range(nblk):
        o_acc = jnp.zeros((nk, B_GW), F32)
        l_acc = jnp.zeros((nk, B_GW), F32)
        for h in range(B_HEADS):
            o_acc = jnp.where(masks[h], oh[j, h], o_acc)
            l_acc = jnp.where(masks[h], lse[j, h], l_acc)
        o_ref[0, 0, j * nk:(j + 1) * nk, :] = o_acc
        l_ref[0, 0, j * nk:(j + 1) * nk, :] = l_acc


AP_TQ = 512


def _attn_prompt_call(q, kv):
    B, dil, n, _ = q.shape
    tq = min(AP_TQ, n)
    per = tq // B_NK
    blk = lambda w: pl.BlockSpec((1, 1, tq, w), lambda b, r, i: (b, r, i, 0))
    out = jax.ShapeDtypeStruct((B, dil, n, B_GW), F32)
    return pl.pallas_call(
        _attn_prompt_kernel,
        grid=(B, dil, n // tq),
        in_specs=[blk(B_GW),
                  pl.BlockSpec((1, 1, B_NK, 2 * B_GW), lambda b, r, i: (b, r, jnp.maximum(i * per - 1, 0), 0)),
                  blk(2 * B_GW)],
        out_specs=[blk(B_GW), blk(B_GW)],
        out_shape=[out, out],
        compiler_params=_cparams(("parallel", "parallel", "arbitrary")),
        name="attn_prompt_d%d" % dil,
    )(q, kv, kv)


AS_QB = 16
AS_CACHE_BYTES = 16 * 1024 * 1024
AS_R = 8
AS_NEW = 16


def _attn_sample_kernel(q_ref, kvn_ref, c_ref, o_ref, l_ref, *, dil, g, sub_blocks):
    T = q_ref.shape[0]
    cb, Lb = c_ref.shape[1], c_ref.shape[3]
    R, NR = AS_R, B_HEADS * AS_R
    sub = pl.program_id(0) % sub_blocks
    row = lax.broadcasted_iota(jnp.int32, (NR, B_GW), 0)
    lane = lax.broadcasted_iota(jnp.int32, (NR, B_GW), 1)
    hmask = (lane // B_HD) == (row // R)

    def query_of(shape):
        t = lax.broadcasted_iota(jnp.int32, shape, 0) % R
        return jnp.where(t < T, t, 0)

    pos = lax.broadcasted_iota(jnp.int32, (NR, Lb), 1)
    tn = lax.broadcasted_iota(jnp.int32, (NR, AS_NEW), 1)
    if dil == 1:
        valid_c = pos >= query_of((NR, Lb))
        valid_n = tn <= query_of((NR, AS_NEW))
    else:
        valid_c = (pos % dil) == query_of((NR, Lb))
        valid_n = tn == query_of((NR, AS_NEW))

    for bi in range(cb):
        b = sub * cb + bi
        qrows = jnp.concatenate([q_ref[t, pl.ds(b, 1), g * B_GW:(g + 1) * B_GW] for t in range(T)]
                                + [jnp.zeros((R - T, B_GW), F32)], axis=0)
        qbd = jnp.where(hmask, jnp.concatenate([qrows] * B_HEADS, axis=0), 0.0)
        new = jnp.concatenate([kvn_ref[t, pl.ds(b, 1), :] for t in range(T)]
                              + [jnp.zeros((AS_NEW - T, 2 * B_GW), F32)], axis=0)
        s = jnp.where(valid_c, _dot(qbd, c_ref[0, bi, 0:B_GW, :]), -jnp.inf)
        sn = jnp.where(valid_n, _dot_nt(qbd, new[:, 0:B_GW]), -jnp.inf)
        m = jnp.maximum(jnp.max(s, axis=-1, keepdims=True), jnp.max(sn, axis=-1, keepdims=True))
        p = jnp.exp(s - m)
        pn = jnp.exp(sn - m)
        den = jnp.sum(p, axis=-1, keepdims=True) + jnp.sum(pn, axis=-1, keepdims=True)
        o = _dot_nt(p / den, c_ref[0, bi, B_GW:2 * B_GW, :]) + _dot(pn / den, new[:, B_GW:])
        om = jnp.where(hmask, o, 0.0)
        lm = jnp.where(hmask, m + jnp.log(den), 0.0)
        ob = om[0:R]
        lb = lm[0:R]
        for h in range(1, B_HEADS):
            ob = ob + om[h * R:(h + 1) * R]
            lb = lb + lm[h * R:(h + 1) * R]
        for t in range(T):
            o_ref[t, pl.ds(b, 1), :] = ob[t:t + 1]
            l_ref[t, pl.ds(b, 1), :] = lb[t:t + 1]


def _attn_sample_call(q, kvn, cache_t, layer, g, dil):
    T, B, _ = q.shape
    Lb = cache_t.shape[3]
    qb = min(AS_QB, B)
    cb = max(1, min(qb, AS_CACHE_BYTES // (2 * B_GW * Lb * 4)))
    sub_blocks = qb // cb
    out = jax.ShapeDtypeStruct((T, B, B_GW), F32)
    row_spec = lambda w: pl.BlockSpec((T, qb, w), lambda i: (0, i // sub_blocks, 0))
    return pl.pallas_call(
        functools.partial(_attn_sample_kernel, dil=dil, g=g, sub_blocks=sub_blocks),
        grid=(B // cb,),
        in_specs=[row_spec(B_W), row_spec(2 * B_GW),
                  pl.BlockSpec((1, cb, 2 * B_GW, Lb), lambda i: (layer, i, 0, 0))],
        out_specs=[row_spec(B_GW), row_spec(B_GW)],
        out_shape=[out, out],
        compiler_params=_cparams(("arbitrary",)),
        name="attn_sample_d%d" % dil,
    )(q, kvn, cache_t)


def _shift_rows(x, s, fill):
    n = x.shape[0]
    rolled = pltpu.roll(x, s, 0)
    r = lax.broadcasted_iota(jnp.int32, x.shape, 0)
    return jnp.where(r < s, fill, rolled)


def _rglru_kernel(x_ref, g_ref, prev_ref, h0_ref, cw_ref, cb_ref, wr_ref, br_ref, wi_ref, bi_ref, lam_ref,
                  o_ref, hl_ref, xx_ref, hc_ref, *, unit, hp):
    i = pl.program_id(1)
    tm = x_ref.shape[1]
    halo = (C_CONV - 1) * unit

    @pl.when(i == 0)
    def _():
        xx_ref[hp - halo:hp, :] = prev_ref[0]
        hc_ref[...] = h0_ref[0]

    x = x_ref[0]
    xx_ref[hp:hp + tm, :] = x
    y = cb_ref[...] + xx_ref[hp - halo:hp - halo + tm, :] * cw_ref[0:1, :]
    for j in range(1, C_CONV):
        y = y + xx_ref[hp - halo + j * unit:hp - halo + j * unit + tm, :] * cw_ref[j:j + 1, :]
    tail = xx_ref[hp + tm - halo:hp + tm, :]
    xx_ref[hp - halo:hp, :] = tail

    r = _sigmoid(_dot(y, wr_ref[...]) + br_ref[...])
    ig = _sigmoid(_dot(y, wi_ref[...]) + bi_ref[...])
    log_a = -C_POW * r * _softplus(-lam_ref[...])
    a = jnp.exp(log_a)
    th = jnp.tanh(log_a)
    bx = jnp.sqrt(-2.0 * th / (1.0 - th)) * (ig * y)
    hc = hc_ref[...]
    if unit == 1:
        rr = lax.broadcasted_iota(jnp.int32, bx.shape, 0)
        bx = jnp.where(rr < 1, a * hc + bx, bx)
    elif unit == tm:
        bx = a * hc + bx
    else:
        bx = jnp.concatenate([a[0:unit] * hc + bx[0:unit], bx[unit:]], axis=0)
    s = unit
    while s < tm:
        a_sh = _shift_rows(a, s, 1.0)
        b_sh = _shift_rows(bx, s, 0.0)
        bx = a * b_sh + bx
        a = a * a_sh
        s *= 2
    h = bx
    hl = h[tm - unit:tm, :]
    hc_ref[...] = hl
    hl_ref[0] = hl
    o_ref[0] = h * jax.nn.gelu(g_ref[0])


def _rglru_call(proj3, prev, h0, cw, cb, wr, br, wi, bi, lam, unit, tm):
    nseq, rows, _ = proj3.shape
    halo = (C_CONV - 1) * unit
    hp = -(-halo // 8) * 8
    assert unit == 1 or unit % 8 == 0
    vec = lambda: pl.BlockSpec((1, C_WIDTH), lambda n, i: (0, 0))
    return pl.pallas_call(
        functools.partial(_rglru_kernel, unit=unit, hp=hp),
        grid=(nseq, rows // tm),
        in_specs=[pl.BlockSpec((1, tm, C_WIDTH), lambda n, i: (n, i, OFF_XC // C_WIDTH)),
                  pl.BlockSpec((1, tm, C_WIDTH), lambda n, i: (n, i, OFF_GC // C_WIDTH)),
                  pl.BlockSpec((1, halo, C_WIDTH), lambda n, i: (n, 0, 0)),
                  pl.BlockSpec((1, unit, C_WIDTH), lambda n, i: (n, 0, 0)),
                  pl.BlockSpec((C_CONV, C_WIDTH), lambda n, i: (0, 0)), vec(),
                  pl.BlockSpec((C_WIDTH, C_WIDTH), lambda n, i: (0, 0)), vec(),
                  pl.BlockSpec((C_WIDTH, C_WIDTH), lambda n, i: (0, 0)), vec(), vec()],
        out_specs=[pl.BlockSpec((1, tm, C_WIDTH), lambda n, i: (n, i, 0)),
                   pl.BlockSpec((1, unit, C_WIDTH), lambda n, i: (n, 0, 0))],
        out_shape=[jax.ShapeDtypeStruct((nseq, rows, C_WIDTH), F32),
                   jax.ShapeDtypeStruct((nseq, unit, C_WIDTH), F32)],
        scratch_shapes=[pltpu.VMEM((hp + tm, C_WIDTH), F32), pltpu.VMEM((unit, C_WIDTH), F32)],
        compiler_params=_cparams(("parallel", "arbitrary")),
        name="rglru_u%d" % unit,
    )(proj3, proj3, prev, h0, cw, cb, wr, br, wi, bi, lam)


def _merge_kernel(x_ref, ga_ref, gb_ref, gc_ref, oa_ref, oc_ref,
                  o0_ref, o1_ref, o2_ref, l0_ref, l1_ref, l2_ref,
                  wpa_ref, wpb_ref, wpc_ref, wo_ref, g_ref, b_ref, out_ref, il_ref, *, dils):
    tm = x_ref.shape[1]

    def rows_in_order(ref, dil):
        if dil == 1:
            return ref[0, 0]
        for r in range(dil):
            for j in range(il_ref.shape[0]):
                il_ref[j, pl.ds(r, tm // dil, stride=dil), :] = ref[0, r, :, j * 128:(j + 1) * 128]
        return jnp.concatenate([il_ref[j] for j in range(il_ref.shape[0])], axis=1)

    o0, o1, o2 = (rows_in_order(r, d) for r, d in zip((o0_ref, o1_ref, o2_ref), dils))
    l0, l1, l2 = (rows_in_order(r, d) for r, d in zip((l0_ref, l1_ref, l2_ref), dils))
    m = jnp.maximum(jnp.maximum(l0, l1), l2)
    e0, e1, e2 = jnp.exp(l0 - m), jnp.exp(l1 - m), jnp.exp(l2 - m)
    den = e0 + e1 + e2
    ob = (e0 / den) * o0 + (e1 / den) * o1 + (e2 / den) * o2
    merged = (_sigmoid(ga_ref[0]) * _dot(oa_ref[0], wpa_ref[...])
              + _sigmoid(gb_ref[0]) * _dot(ob, wpb_ref[...])
              + _sigmoid(gc_ref[0]) * _dot(oc_ref[0], wpc_ref[...]))
    mix = _dot(merged, wo_ref[...])
    out_ref[0] = _layer_norm(DN_ALPHA * x_ref[0] + mix, g_ref[...], b_ref[...])


def _merge_call(x3, proj3, oa, oc, obs, lses, wpa, wpb, wpc, wo, g, b, tm):
    nseq, rows, _ = x3.shape
    dils = tuple(o.shape[1] for o in obs)
    row_spec = lambda w, j=0: pl.BlockSpec((1, tm, w), lambda n, i: (n, i, j))
    grp_spec = lambda d: pl.BlockSpec((1, d, tm // d, B_GW), lambda n, i: (n, 0, i, 0))
    full = lambda a: pl.BlockSpec(a.shape, lambda n, i: (0, 0))
    return pl.pallas_call(
        functools.partial(_merge_kernel, dils=dils),
        grid=(nseq, rows // tm),
        in_specs=[row_spec(D_MODEL), row_spec(D_MODEL, 0), row_spec(D_MODEL, 1), row_spec(D_MODEL, 2),
                  row_spec(A_VW), row_spec(C_WIDTH)]
                 + [grp_spec(d) for d in dils] * 2
                 + [full(wpa), full(wpb), full(wpc), full(wo), full(g), full(b)],
        out_specs=row_spec(D_MODEL),
        out_shape=jax.ShapeDtypeStruct((nseq, rows, D_MODEL), F32),
        scratch_shapes=[pltpu.VMEM((B_GW // 128, tm, 128), F32)],
        compiler_params=_cparams(("parallel", "parallel")),
        name="merge_ln",
    )(x3, proj3, proj3, proj3, oa, oc, *obs, *lses, wpa, wpb, wpc, wo, g, b)


def _ffn_kernel(x_ref, prev_ref, wu_ref, cw_ref, cb_ref, wd_ref, g_ref, b_ref,
                o_ref, st_ref, xx_ref, *, unit, hp):
    i = pl.program_id(1)
    tm = x_ref.shape[1]
    halo = (F_CONV - 1) * unit

    @pl.when(i == 0)
    def _():
        xx_ref[hp - halo:hp, :] = prev_ref[0]

    x = x_ref[0]
    xb = x.astype(BF16)
    f = jnp.zeros((tm, D_MODEL), F32)
    for c0 in range(0, D_FF, FF_CHUNK):
        c1 = c0 + FF_CHUNK
        gate = jnp.dot(xb, wu_ref[:, c0:c1], preferred_element_type=F32)
        up = jnp.dot(xb, wu_ref[:, D_FF + c0:D_FF + c1], preferred_element_type=F32)
        xx_ref[hp:hp + tm, c0:c1] = gate
        y = cb_ref[:, c0:c1] + xx_ref[hp - halo:hp - halo + tm, c0:c1] * cw_ref[0:1, c0:c1]
        for j in range(1, F_CONV):
            y = y + xx_ref[hp - halo + j * unit:hp - halo + j * unit + tm, c0:c1] * cw_ref[j:j + 1, c0:c1]
        tail = xx_ref[hp + tm - halo:hp + tm, c0:c1]
        xx_ref[hp - halo:hp, c0:c1] = tail
        st_ref[0, :, c0:c1] = tail
        f = f + _dot(_silu(y) * up, wd_ref[c0:c1, :])
    o_ref[0] = _layer_norm(DN_ALPHA * x + f, g_ref[...], b_ref[...])


def _ffn_call(x3, prev, wu, cw, cb, wd, g, b, unit, tm):
    nseq, rows, _ = x3.shape
    halo = (F_CONV - 1) * unit
    hp = -(-halo // 8) * 8
    const = lambda a: pl.BlockSpec(a.shape, lambda n, i: (0, 0), pipeline_mode=pl.Buffered(1))
    return pl.pallas_call(
        functools.partial(_ffn_kernel, unit=unit, hp=hp),
        grid=(nseq, rows // tm),
        in_specs=[pl.BlockSpec((1, tm, D_MODEL), lambda n, i: (n, i, 0)),
                  pl.BlockSpec((1, halo, D_FF), lambda n, i: (n, 0, 0)),
                  const(wu), const(cw), const(cb), const(wd), const(g), const(b)],
        out_specs=[pl.BlockSpec((1, tm, D_MODEL), lambda n, i: (n, i, 0)),
                   pl.BlockSpec((1, halo, D_FF), lambda n, i: (n, 0, 0))],
        out_shape=[jax.ShapeDtypeStruct((nseq, rows, D_MODEL), F32),
                   jax.ShapeDtypeStruct((nseq, halo, D_FF), F32)],
        scratch_shapes=[pltpu.VMEM((hp + tm, D_FF), F32)],
        compiler_params=_cparams(("parallel", "arbitrary")),
        name="conv_ffn_u%d" % unit,
    )(x3, prev, wu, cw, cb, wd, g, b)


def _block_diag(w):
    nblk, bw, _ = w.shape
    eye = jnp.eye(nblk, dtype=w.dtype)
    return (eye[:, None, :, None] * w[:, :, None, :]).reshape(nblk * bw, nblk * bw)


def _layer_params(l, w_in, a_conv_w, a_A_log, a_dt_bias, a_norm_w, c_conv_w, c_conv_b, c_w_r, c_b_r,
                  c_w_i, c_b_i, c_lam, w_pa, w_pb, w_pc, w_o, ln1_g, ln1_b, f_up, f_conv_w, f_conv_b,
                  f_down, ln2_g, ln2_b):
    w = w_in[l]
    o_b = A_QKV
    o_a = o_b + A_HEADS
    o_z = o_a + A_HEADS
    o_qb = o_z + A_VW
    o_xc = o_qb + 3 * B_W
    o_gc = o_xc + C_WIDTH
    o_gt = o_gc + C_WIDTH
    pad = jnp.zeros((D_MODEL, N_PROJ - OFF_BA - 2 * A_HEADS), w.dtype)
    wp = jnp.concatenate([w[:, o_gt:o_gt + 3 * D_MODEL], w[:, o_xc:o_gc], w[:, o_gc:o_gt], w[:, o_z:o_qb],
                          w[:, 0:o_b], w[:, o_qb:o_xc], w[:, o_b:o_z], pad], axis=1).astype(BF16)
    gp = jnp.zeros((2, 128), F32)
    gp = gp.at[0, A_HEADS:2 * A_HEADS].set(a_A_log[l]).at[1, A_HEADS:2 * A_HEADS].set(a_dt_bias[l])
    row = lambda v: v.reshape(1, -1)
    return dict(
        wp=wp, a_cw=a_conv_w[l], gp=gp, a_nw=row(a_norm_w[l]),
        c_cw=c_conv_w[l], c_cb=row(c_conv_b[l]),
        c_wr=_block_diag(c_w_r[l]).astype(BF16), c_br=row(c_b_r[l]),
        c_wi=_block_diag(c_w_i[l]).astype(BF16), c_bi=row(c_b_i[l]), c_lam=row(c_lam[l]),
        wpa=w_pa[l].astype(BF16), wpb=w_pb[l].astype(BF16), wpc=w_pc[l].astype(BF16), wo=w_o[l].astype(BF16),
        ln1_g=row(ln1_g[l]), ln1_b=row(ln1_b[l]),
        f_up=f_up[l].astype(BF16), f_cw=f_conv_w[l], f_cb=row(f_conv_b[l]), f_down=f_down[l].astype(BF16),
        ln2_g=row(ln2_g[l]), ln2_b=row(ln2_b[l]))


def _layer_tail(x3, proj3, o_a, obs, lses, unit, p, c_prev, c_h0, f_prev, tm_merge, tm_seq, tm_ffn):
    o_c, c_hl = _rglru_call(proj3, c_prev, c_h0, p["c_cw"], p["c_cb"], p["c_wr"], p["c_br"], p["c_wi"],
                            p["c_bi"], p["c_lam"], unit, tm_seq)
    x1 = _merge_call(x3, proj3, o_a, o_c, obs, lses, p["wpa"], p["wpb"], p["wpc"], p["wo"],
                     p["ln1_g"], p["ln1_b"], tm_merge)
    x_out, f_state = _ffn_call(x1, f_prev, p["f_up"], p["f_cw"], p["f_cb"], p["f_down"],
                               p["ln2_g"], p["ln2_b"], unit, tm_ffn)
    return x_out, c_hl, f_state


def kernel(x_prompt, x_sample, state_a_conv, state_a_rec, cache_b_w128, cache_b_w512, cache_b_w2048,
           state_c_conv, state_c_h, state_f_conv, ln_in_g, ln_in_b, w_in, a_conv_w, a_A_log, a_dt_bias,
           a_norm_w, c_conv_w, c_conv_b, c_w_r, c_b_r, c_w_i, c_b_i, c_lam, w_pa, w_pb, w_pc, w_o,
           ln1_g, ln1_b, f_up, f_conv_w, f_conv_b, f_down, ln2_g, ln2_b):
    Bp, Sp, _ = x_prompt.shape
    Bs, Ts, _ = x_sample.shape
    caches = (cache_b_w128, cache_b_w512, cache_b_w2048)
    for (win, dil), cache in zip(B_GROUPS, caches):
        assert win // dil == B_NK and cache.shape[2] == win and Sp % (dil * B_NK) == 0

    caches_t = [jnp.transpose(c, (0, 1, 3, 4, 5, 2)).reshape(DEPTH, Bs, 2 * B_GW, c.shape[2]) for c in caches]
    Rs = Ts * Bs

    lng, lnb = ln_in_g.reshape(1, -1), ln_in_b.reshape(1, -1)
    hp = _ln_call(x_prompt.reshape(Bp * Sp, D_MODEL), lng, lnb, 512)
    xs_tm = jnp.swapaxes(x_sample, 0, 1).reshape(Rs, D_MODEL)
    hs = _ln_call(xs_tm, lng, lnb, Rs)

    tabs_p = _rope_tables(jnp.arange(Sp))
    tabs_s = _rope_tables(jnp.repeat(PAST_LEN + jnp.arange(Ts), Bs))

    zeros_c = jnp.zeros((Bp, C_CONV - 1, C_WIDTH), F32)
    zeros_h = jnp.zeros((Bp, 1, C_WIDTH), F32)
    zeros_f = jnp.zeros((Bp, F_CONV - 1, D_FF), F32)

    outs_p, outs_s = [], []
    for l in range(DEPTH):
        p = _layer_params(l, w_in, a_conv_w, a_A_log, a_dt_bias, a_norm_w, c_conv_w, c_conv_b, c_w_r, c_b_r,
                          c_w_i, c_b_i, c_lam, w_pa, w_pb, w_pc, w_o, ln1_g, ln1_b, f_up, f_conv_w,
                          f_conv_b, f_down, ln2_g, ln2_b)

        proj3 = _proj_call(hp, p["wp"], 1024).reshape(Bp, Sp, N_PROJ)
        o_a, a_rec = _gdn_prompt_call(proj3, p["a_cw"], p["gp"], p["a_nw"])
        rp = _rope_prompt_call(proj3, tabs_p, 256)
        res = [_attn_prompt_call(rp[g], rp[3 + g]) for g in range(len(B_GROUPS))]
        hp3, c_hl, f_st = _layer_tail(hp.reshape(Bp, Sp, D_MODEL), proj3, o_a, [r[0] for r in res],
                                      [r[1] for r in res], 1, p, zeros_c, zeros_h, zeros_f, 256, 256, 512)
        hp = hp3.reshape(Bp * Sp, D_MODEL)
        kv_rows_p = [t.reshape(Bp, 2, B_HEADS, B_HD, t.shape[-1]).transpose(0, 4, 1, 2, 3) for t in rp[6:9]]
        outs_p.append((
            proj3[:, Sp - (A_CONV - 1):, OFF_QKVA:OFF_QKVA + A_QKV],
            a_rec, kv_rows_p[0], kv_rows_p[1], kv_rows_p[2],
            proj3[:, Sp - (C_CONV - 1):, OFF_XC:OFF_XC + C_WIDTH],
            c_hl[:, 0],
            f_st))

        proj3 = _proj_call(hs, p["wp"], Rs).reshape(1, Rs, N_PROJ)
        pt = proj3.reshape(Ts, Bs, N_PROJ)
        a_st = jnp.swapaxes(state_a_conv[l], 0, 1)
        o_a, a_rec = _gdn_sample_call(pt, a_st, state_a_rec, l, p["a_cw"], p["gp"], p["a_nw"])
        rs = _rope_sample_call(proj3, tabs_s, Bs)
        q3 = rs[0].reshape(Ts, Bs, B_W)
        res = [_attn_sample_call(q3, rs[1 + g].reshape(Ts, Bs, 2 * B_GW), caches_t[g], l, g, dil)
               for g, (_, dil) in enumerate(B_GROUPS)]
        c_prev = jnp.swapaxes(state_c_conv[l], 0, 1).reshape(1, (C_CONV - 1) * Bs, C_WIDTH)
        f_prev = jnp.swapaxes(state_f_conv[l], 0, 1).reshape(1, (F_CONV - 1) * Bs, D_FF)
        hs3, c_hl, f_st = _layer_tail(hs.reshape(1, Rs, D_MODEL), proj3, o_a.reshape(1, Rs, A_VW),
                                      [r[0].reshape(1, 1, Rs, B_GW) for r in res],
                                      [r[1].reshape(1, 1, Rs, B_GW) for r in res],
                                      Bs, p, c_prev, state_c_h[l][None], f_prev, min(256, Rs), Rs, Rs)
        hs = hs3.reshape(Rs, D_MODEL)
        tm2bm = lambda a: jnp.swapaxes(a, 0, 1)
        kv_rows_s = [t.reshape(Ts, 2, B_HEADS, B_HD, Bs).transpose(4, 0, 1, 2, 3) for t in rs[4:7]]
        outs_s.append((
            tm2bm(pt[Ts - (A_CONV - 1):, :, OFF_QKVA:OFF_QKVA + A_QKV]),
            a_rec, kv_rows_s[0], kv_rows_s[1], kv_rows_s[2],
            tm2bm(pt[Ts - (C_CONV - 1):, :, OFF_XC:OFF_XC + C_WIDTH]),
            c_hl[0],
            tm2bm(f_st.reshape(F_CONV - 1, Bs, D_FF))))

    stack = lambda outs: [jnp.stack(s, 0) for s in zip(*outs)]
    y_p = hp.reshape(Bp, Sp, D_MODEL)
    y_s = jnp.swapaxes(hs.reshape(Ts, Bs, D_MODEL), 0, 1)
    return (y_p, y_s, *stack(outs_p), *stack(outs_s))
```

```python
import functools
import math

import numpy as np
import jax
import jax.numpy as jnp
from jax import lax
from jax.experimental import pallas as pl
from jax.experimental.pallas import tpu as pltpu

F32 = jnp.float32
BF16 = jnp.bfloat16

D_MODEL = 1024
DEPTH = 4
PAST_LEN = 2048
A_HEADS, A_DK, A_DV, A_CONV, A_CHUNK = 4, 128, 128, 4, 64
B_GROUPS = ((128, 1), (512, 4), (2048, 16))
B_HEADS, B_HD = 4, 64
B_ROT = B_HD // 4
ROPE_THETA = 500000.0
B_NK = 128
C_WIDTH, C_BLOCKS, C_CONV, C_POW = 512, 8, 4, 8.0
C_BW = C_WIDTH // C_BLOCKS
D_FF = ((8 * D_MODEL // 3 + 255) // 256) * 256
F_CONV = 3
FF_CHUNK = 1408
DN_ALPHA = (2 * DEPTH) ** 0.25
LN_EPS = 1e-5
NORM_EPS = 1e-6

A_QK = A_HEADS * A_DK
A_VW = A_HEADS * A_DV
A_QKV = 2 * A_QK + A_VW
B_GW = B_HEADS * B_HD
B_W = len(B_GROUPS) * B_GW

OFF_GATES = 0
OFF_XC = 3072
OFF_GC = 3584
OFF_Z = 4096
OFF_QKVA = 4608
OFF_QB = 6144
OFF_BA = 8448
N_PROJ = 8704
PROJ_TN = 2176
TAIL_ROWS = 8

VMEM_LIMIT = 56 * 1024 * 1024


def _cparams(sem):
    return pltpu.CompilerParams(dimension_semantics=sem, vmem_limit_bytes=VMEM_LIMIT)


def _sigmoid(x):
    return 1.0 / (1.0 + jnp.exp(-x))


def _silu(x):
    return x * _sigmoid(x)


def _softplus(x):
    return jnp.maximum(x, 0.0) + jnp.log1p(jnp.exp(-jnp.abs(x)))


def _layer_norm(x, g, b):
    mu = jnp.mean(x, axis=-1, keepdims=True)
    xc = x - mu
    var = jnp.mean(xc * xc, axis=-1, keepdims=True)
    return xc * lax.rsqrt(var + LN_EPS) * g + b


def _dot(a, b):
    return jnp.dot(a.astype(BF16), b.astype(BF16), preferred_element_type=F32)


def _dot_nt(a, b):
    return lax.dot_general(a.astype(BF16), b.astype(BF16), (((1,), (1,)), ((), ())),
                           preferred_element_type=F32)


def _dot_tn(a, b):
    return lax.dot_general(a.astype(BF16), b.astype(BF16), (((0,), (0,)), ((), ())),
                           preferred_element_type=F32)


def _split2(a):
    hi = a.astype(BF16)
    lo = (a - hi.astype(F32)).astype(BF16)
    return hi, lo


def _dot3(a, b):
    ah, al = _split2(a)
    bh, bl = _split2(b)
    d = lambda x, y: jnp.dot(x, y, preferred_element_type=F32)
    return d(ah, bh) + (d(ah, bl) + d(al, bh))


def _ln_kernel(x_ref, g_ref, b_ref, o_ref):
    o_ref[...] = _layer_norm(x_ref[...], g_ref[...], b_ref[...])


def _ln_call(x, g, b, tm):
    rows = x.shape[0]
    return pl.pallas_call(
        _ln_kernel,
        grid=(rows // tm,),
        in_specs=[pl.BlockSpec((tm, D_MODEL), lambda i: (i, 0)),
                  pl.BlockSpec((1, D_MODEL), lambda i: (0, 0)),
                  pl.BlockSpec((1, D_MODEL), lambda i: (0, 0))],
        out_specs=pl.BlockSpec((tm, D_MODEL), lambda i: (i, 0)),
        out_shape=jax.ShapeDtypeStruct((rows, D_MODEL), F32),
        compiler_params=_cparams(("parallel",)),
        name="entry_ln",
    )(x, g, b)


def _proj_kernel(x_ref, w_ref, o_ref, xb_ref):
    @pl.when(pl.program_id(1) == 0)
    def _():
        xb_ref[...] = x_ref[...].astype(BF16)

    o_ref[...] = jnp.dot(xb_ref[...], w_ref[...], preferred_element_type=F32).astype(o_ref.dtype)


def _proj_call(x, w, tm, out_dtype):
    rows = x.shape[0]
    return pl.pallas_call(
        _proj_kernel,
        grid=(rows // tm, N_PROJ // PROJ_TN),
        in_specs=[pl.BlockSpec((tm, D_MODEL), lambda i, j: (i, 0)),
                  pl.BlockSpec((D_MODEL, PROJ_TN), lambda i, j: (0, j))],
        out_specs=pl.BlockSpec((tm, PROJ_TN), lambda i, j: (i, j)),
        out_shape=jax.ShapeDtypeStruct((rows, N_PROJ), out_dtype),
        scratch_shapes=[pltpu.VMEM((tm, D_MODEL), BF16)],
        compiler_params=_cparams(("parallel", "arbitrary")),
        name="in_proj",
    )(x, w)


def _gdn_prompt_kernel(x_ref, z_ref, ba_ref, cw_ref, gp_ref, nw_ref,
                       o_ref, sfin_ref, xx_ref, s_ref):
    c = pl.program_id(0)
    C = A_CHUNK
    nb = x_ref.shape[0]

    @pl.when(c == 0)
    def _():
        xx_ref[:, 0:8, :] = jnp.zeros((nb, 8, A_QKV), F32)
        s_ref[...] = jnp.zeros(s_ref.shape, F32)

    ri = lax.broadcasted_iota(jnp.int32, (C, C), 0)
    ci = lax.broadcasted_iota(jnp.int32, (C, C), 1)
    tril = ri >= ci
    strict = ri > ci
    eye = (ri == ci).astype(F32)
    trilf = tril.astype(F32)
    nw = nw_ref[...]

    ch = []
    for b in range(nb):
        x = x_ref[b].astype(F32)
        xx_ref[b, 8:8 + C, :] = x
        y = xx_ref[b, 5:5 + C, :] * cw_ref[0:1, :]
        for j in range(1, A_CONV):
            y = y + xx_ref[b, 5 + j:5 + j + C, :] * cw_ref[j:j + 1, :]
        xx_ref[b, 0:8, :] = x[C - 8:C, :]
        y = _silu(y)
        ba = ba_ref[b].astype(F32)
        beta_all = _sigmoid(ba)
        g_all = -jnp.exp(gp_ref[0:1, :]) * _softplus(ba + gp_ref[1:2, :])
        gc = _dot3(trilf, g_all)
        gct = jnp.concatenate([gc, jnp.zeros((128 - C, 128), F32)], axis=0).T
        for h in range(A_HEADS):
            q = y[:, h * A_DK:(h + 1) * A_DK]
            k = y[:, A_QK + h * A_DK:A_QK + (h + 1) * A_DK]
            v = y[:, 2 * A_QK + h * A_DV:2 * A_QK + (h + 1) * A_DV]
            q = q * lax.rsqrt(jnp.sum(q * q, axis=-1, keepdims=True) + NORM_EPS) * (A_DK ** -0.5)
            k = k * lax.rsqrt(jnp.sum(k * k, axis=-1, keepdims=True) + NORM_EPS)
            beta = beta_all[:, h:h + 1]
            gcol = gc[:, 4 + h:5 + h]
            grow = gct[4 + h:5 + h, 0:C]
            decay = jnp.where(tril, jnp.exp(jnp.where(tril, gcol - grow, 0.0)), 0.0)
            eg = jnp.exp(gcol)
            glast = gcol[C - 1:C, :]
            ch.append(dict(b=b, h=h, q=q, k=k, kb=k * beta, vb=v * beta, decay=decay, eg=eg,
                           k_dec=k * jnp.exp(glast - gcol), q_dec=q * eg, eglast=jnp.exp(glast)))
    for d in ch:
        d["m"] = -jnp.where(strict, _dot_nt(d["kb"], d["k"]) * d["decay"], 0.0)
        d["t"] = eye + d["m"]
    for d in ch:
        d["qk"] = jnp.where(tril, _dot_nt(d["q"], d["k"]) * d["decay"], 0.0)
    for _ in range(5):
        for d in ch:
            d["m"] = _dot(d["m"], d["m"])
        for d in ch:
            d["t"] = d["t"] + _dot(d["t"], d["m"])
    for d in ch:
        d["u"] = _dot(d["t"], d["vb"])
        d["w"] = _dot(d["t"], d["kb"] * d["eg"])
    for d in ch:
        d["s"] = s_ref[d["b"], d["h"]]
        d["v_new"] = d["u"] - _dot(d["w"], d["s"])
    for d in ch:
        d["o"] = _dot(d["q_dec"], d["s"]) + _dot(d["qk"], d["v_new"])
        s_ref[d["b"], d["h"]] = d["s"] * d["eglast"] + _dot_tn(d["k_dec"], d["v_new"])
    for d in ch:
        b, h, o = d["b"], d["h"], d["o"]
        o = o * lax.rsqrt(jnp.mean(o * o, axis=-1, keepdims=True) + NORM_EPS) * nw
        o_ref[b, :, h * A_DV:(h + 1) * A_DV] = o * _silu(z_ref[b, :, h * A_DV:(h + 1) * A_DV].astype(F32))

    @pl.when(c == pl.num_programs(0) - 1)
    def _():
        sfin_ref[...] = s_ref[...]


def _gdn_prompt_call(proj3, cw, gp, nw):
    B, S, _ = proj3.shape
    C = A_CHUNK
    return pl.pallas_call(
        _gdn_prompt_kernel,
        grid=(S // C,),
        in_specs=[pl.BlockSpec((B, C, A_QKV), lambda c: (0, c, OFF_QKVA // A_QKV)),
                  pl.BlockSpec((B, C, A_VW), lambda c: (0, c, OFF_Z // A_VW)),
                  pl.BlockSpec((B, C, 128), lambda c: (0, c, OFF_BA // 128)),
                  pl.BlockSpec((A_CONV, A_QKV), lambda c: (0, 0)),
                  pl.BlockSpec((2, 128), lambda c: (0, 0)),
                  pl.BlockSpec((1, A_DV), lambda c: (0, 0))],
        out_specs=[pl.BlockSpec((B, C, A_VW), lambda c: (0, c, 0)),
                   pl.BlockSpec((B, A_HEADS, A_DK, A_DV), lambda c: (0, 0, 0, 0))],
        out_shape=[jax.ShapeDtypeStruct((B, S, A_VW), F32),
                   jax.ShapeDtypeStruct((B, A_HEADS, A_DK, A_DV), F32)],
        scratch_shapes=[pltpu.VMEM((B, 8 + C, A_QKV), F32),
                        pltpu.VMEM((B, A_HEADS, A_DK, A_DV), F32)],
        compiler_params=_cparams(("arbitrary",)),
        name="gdn_prompt",
    )(proj3, proj3, proj3, cw, gp, nw)


GS_BB = 8
GS_C = 16


def _gdn_sample_kernel(x_ref, z_ref, ba_ref, st_ref, s0_ref, cw_ref, gp_ref, nw_ref,
                       o_ref, snew_ref, oscr_ref):
    T = x_ref.shape[0]
    bb = GS_BB
    xx = [st_ref[j] for j in range(A_CONV - 1)] + [x_ref[t].astype(F32) for t in range(T)]
    C = GS_C
    assert T <= C
    qs, ks, kbs, vbs, gcs = [], [], [], [], []
    for t in range(T):
        y = xx[t] * cw_ref[0:1, :]
        for j in range(1, A_CONV):
            y = y + xx[t + j] * cw_ref[j:j + 1, :]
        y = _silu(y)
        ba = ba_ref[t].astype(F32)
        beta_all = _sigmoid(ba)
        g_all = -jnp.exp(gp_ref[0:1, :]) * _softplus(ba + gp_ref[1:2, :])
        gcs.append(g_all if t == 0 else gcs[-1] + g_all)
        qs.append([]), ks.append([]), kbs.append([]), vbs.append([])
        for h in range(A_HEADS):
            q = y[:, h * A_DK:(h + 1) * A_DK]
            k = y[:, A_QK + h * A_DK:A_QK + (h + 1) * A_DK]
            v = y[:, 2 * A_QK + h * A_DV:2 * A_QK + (h + 1) * A_DV]
            q = q * lax.rsqrt(jnp.sum(q * q, axis=-1, keepdims=True) + NORM_EPS) * (A_DK ** -0.5)
            k = k * lax.rsqrt(jnp.sum(k * k, axis=-1, keepdims=True) + NORM_EPS)
            beta = beta_all[:, h:h + 1]
            qs[t].append(q), ks[t].append(k), kbs[t].append(k * beta), vbs[t].append(v * beta)

    ri = lax.broadcasted_iota(jnp.int32, (C, C), 0)
    ci = lax.broadcasted_iota(jnp.int32, (C, C), 1)
    tril = ri >= ci
    strict = ri > ci
    eye = (ri == ci).astype(F32)
    zrows = jnp.zeros((C - T, A_DK), F32)
    rows_of = lambda slabs, b, h: jnp.concatenate([slabs[t][h][b:b + 1, :] for t in range(T)] + [zrows], axis=0)
    ch = []
    for h in range(A_HEADS):
        for b in range(bb):
            gcol = jnp.concatenate([gcs[t][b:b + 1, 4 + h:5 + h] for t in range(T)]
                                   + [gcs[T - 1][b:b + 1, 4 + h:5 + h]] * (C - T), axis=0)
            grow = jnp.sum(eye * gcol, axis=0, keepdims=True)
            decay = jnp.where(tril, jnp.exp(jnp.where(tril, gcol - grow, 0.0)), 0.0)
            eg = jnp.exp(gcol)
            glast = gcol[C - 1:C, :]
            q, k, kb, vb = (rows_of(s, b, h) for s in (qs, ks, kbs, vbs))
            ch.append(dict(b=b, h=h, q=q, k=k, kb=kb, vb=vb, decay=decay, eg=eg,
                           k_dec=k * jnp.exp(glast - gcol), q_dec=q * eg, eglast=jnp.exp(glast)))
    for d in ch:
        d["m"] = -jnp.where(strict, _dot_nt(d["kb"], d["k"]) * d["decay"], 0.0)
        d["qk"] = jnp.where(tril, _dot_nt(d["q"], d["k"]) * d["decay"], 0.0)
    for d in ch:
        d["t"] = eye + d["m"]
    for _ in range(GS_C.bit_length() - 2):
        for d in ch:
            d["m"] = _dot(d["m"], d["m"])
        for d in ch:
            d["t"] = d["t"] + _dot(d["t"], d["m"])
    for d in ch:
        d["u"] = _dot(d["t"], d["vb"])
        d["w"] = _dot(d["t"], d["kb"] * d["eg"])
    for d in ch:
        d["s"] = s0_ref[0, d["b"], d["h"]]
        ws = _dot(jnp.concatenate([d["w"], d["q_dec"]], axis=0), d["s"])
        d["v_new"] = d["u"] - ws[0:C]
        d["qs"] = ws[C:2 * C]
    for d in ch:
        b, h = d["b"], d["h"]
        o = d["qs"] + _dot(d["qk"], d["v_new"])
        snew_ref[b, h] = d["s"] * d["eglast"] + _dot_tn(d["k_dec"], d["v_new"])
        for t in range(T):
            oscr_ref[t, b:b + 1, h * A_DV:(h + 1) * A_DV] = o[t:t + 1, :]
    nw = nw_ref[...]
    for t in range(T):
        z = z_ref[t].astype(F32)
        for h in range(A_HEADS):
            o = oscr_ref[t, :, h * A_DV:(h + 1) * A_DV]
            o = o * lax.rsqrt(jnp.mean(o * o, axis=-1, keepdims=True) + NORM_EPS) * nw
            o_ref[t, :, h * A_DV:(h + 1) * A_DV] = o * _silu(z[:, h * A_DV:(h + 1) * A_DV])


def _gdn_sample_call(proj3, st, s0, layer, cw, gp, nw):
    T, B, _ = proj3.shape
    bb = GS_BB
    return pl.pallas_call(
        _gdn_sample_kernel,
        grid=(B // bb,),
        in_specs=[pl.BlockSpec((T, bb, A_QKV), lambda i: (0, i, OFF_QKVA // A_QKV)),
                  pl.BlockSpec((T, bb, A_VW), lambda i: (0, i, OFF_Z // A_VW)),
                  pl.BlockSpec((T, bb, 128), lambda i: (0, i, OFF_BA // 128)),
                  pl.BlockSpec((A_CONV - 1, bb, A_QKV), lambda i: (0, i, 0)),
                  pl.BlockSpec((1, bb, A_HEADS, A_DK, A_DV), lambda i: (layer, i, 0, 0, 0)),
                  pl.BlockSpec((A_CONV, A_QKV), lambda i: (0, 0)),
                  pl.BlockSpec((2, 128), lambda i: (0, 0)),
                  pl.BlockSpec((1, A_DV), lambda i: (0, 0))],
        out_specs=[pl.BlockSpec((T, bb, A_VW), lambda i: (0, i, 0)),
                   pl.BlockSpec((bb, A_HEADS, A_DK, A_DV), lambda i: (i, 0, 0, 0))],
        out_shape=[jax.ShapeDtypeStruct((T, B, A_VW), F32),
                   jax.ShapeDtypeStruct(s0.shape[1:], F32)],
        scratch_shapes=[pltpu.VMEM((T, bb, A_VW), F32)],
        compiler_params=_cparams(("parallel",)),
        name="gdn_sample",
    )(proj3, proj3, proj3, st, s0, cw, gp, nw)


def _rope_qkv(q_ref, k_ref, v_ref, c_ref, s1_ref, s2_ref):
    reps = B_W // 128
    c = jnp.concatenate([c_ref[0]] * reps, axis=1)
    s1 = jnp.concatenate([s1_ref[0]] * reps, axis=1)
    s2 = jnp.concatenate([s2_ref[0]] * reps, axis=1)

    def rot(x):
        return x * c + pltpu.roll(x, B_W - B_ROT // 2, 1) * s1 + pltpu.roll(x, B_ROT // 2, 1) * s2

    return rot(q_ref[0].astype(F32)) * (B_HD ** -0.5), rot(k_ref[0].astype(F32)), v_ref[0].astype(F32)


def _rope_in_specs(tm):
    qb = OFF_QB // B_W
    tab_spec = pl.BlockSpec((1, tm, 128), lambda n, i: (0, i, 0))
    return [pl.BlockSpec((1, tm, B_W), lambda n, i: (n, i, qb)),
            pl.BlockSpec((1, tm, B_W), lambda n, i: (n, i, qb + 1)),
            pl.BlockSpec((1, tm, B_W), lambda n, i: (n, i, qb + 2)),
            tab_spec, tab_spec, tab_spec]


def _store_chunks(buf_ref, x):
    for j in range(buf_ref.shape[0]):
        buf_ref[j] = x[:, j * 128:(j + 1) * 128]


def _load_chunks_strided(buf_ref, start, size, stride):
    return jnp.concatenate([buf_ref[j, pl.ds(start, size, stride=stride), :] for j in range(buf_ref.shape[0])],
                           axis=1)


def _rope_prompt_kernel(q_ref, k_ref, v_ref, c_ref, s1_ref, s2_ref,
                        q0_ref, q1_ref, q2_ref, kv0_ref, kv1_ref, kv2_ref, t0_ref, t1_ref, t2_ref,
                        qs_ref, kvs_ref, *, tails):
    i = pl.program_id(1)
    tm = q_ref.shape[1]
    q, k, v = _rope_qkv(q_ref, k_ref, v_ref, c_ref, s1_ref, s2_ref)
    outs = ((q0_ref, kv0_ref, t0_ref), (q1_ref, kv1_ref, t1_ref), (q2_ref, kv2_ref, t2_ref))
    for g, ((_, dil), (qo_ref, kvo_ref, to_ref)) in enumerate(zip(B_GROUPS, outs)):
        lo, hi = g * B_GW, (g + 1) * B_GW
        qg = q[:, lo:hi]
        kv = jnp.concatenate([k[:, lo:hi], v[:, lo:hi]], axis=1)
        if dil == 1:
            qo_ref[0, 0] = qg.astype(BF16)
            kvo_ref[0, 0] = kv
        else:
            _store_chunks(qs_ref, qg)
            _store_chunks(kvs_ref, kv)
            for r in range(dil):
                qo_ref[0, r] = _load_chunks_strided(qs_ref, r, tm // dil, dil).astype(BF16)
                kvo_ref[0, r] = _load_chunks_strided(kvs_ref, r, tm // dil, dil)
        first, w = tails[g]

        @pl.when(i >= first)
        def _(to_ref=to_ref, w=w, kv=kv):
            to_ref[0] = kv[tm - w:tm, :].T


def _rope_prompt_call(proj3, tabs, tm):
    B, S, _ = proj3.shape
    nt = S // tm
    out_shape, out_specs, tails = [], [], []
    for width, dtype in ((B_GW, BF16), (2 * B_GW, F32)):
        for _, dil in B_GROUPS:
            out_shape.append(jax.ShapeDtypeStruct((B, dil, S // dil, width), dtype))
            out_specs.append(pl.BlockSpec((1, dil, tm // dil, width), lambda n, i: (n, 0, i, 0)))
    for win, _ in B_GROUPS:
        win = min(win, S)
        w = min(tm, win)
        first = nt - win // w
        tails.append((first, w))
        out_shape.append(jax.ShapeDtypeStruct((B, 2 * B_GW, win), F32))
        out_specs.append(pl.BlockSpec((1, 2 * B_GW, w),
                                      lambda n, i, first=first: (n, 0, jnp.maximum(i - first, 0))))
    return pl.pallas_call(
        functools.partial(_rope_prompt_kernel, tails=tuple(tails)),
        grid=(B, nt),
        in_specs=_rope_in_specs(tm),
        out_specs=out_specs,
        out_shape=out_shape,
        scratch_shapes=[pltpu.VMEM((B_GW // 128, tm, 128), F32), pltpu.VMEM((2 * B_GW // 128, tm, 128), F32)],
        compiler_params=_cparams(("parallel", "arbitrary")),
        name="rope_prompt",
    )(proj3, proj3, proj3, *tabs)


def _rope_sample_kernel(q_ref, k_ref, v_ref, c_ref, s1_ref, s2_ref,
                        qo_ref, kv0_ref, kv1_ref, kv2_ref, t0_ref, t1_ref, t2_ref, *, unit):
    q, k, v = _rope_qkv(q_ref, k_ref, v_ref, c_ref, s1_ref, s2_ref)
    qo_ref[0] = q
    for g, (kvo_ref, to_ref) in enumerate(((kv0_ref, t0_ref), (kv1_ref, t1_ref), (kv2_ref, t2_ref))):
        kv = jnp.concatenate([k[:, g * B_GW:(g + 1) * B_GW], v[:, g * B_GW:(g + 1) * B_GW]], axis=1)
        kvo_ref[0] = kv
        for t in range(q.shape[0] // unit):
            to_ref[t] = kv[t * unit:(t + 1) * unit, :].T


def _rope_sample_call(proj3, tabs, unit):
    _, rows, _ = proj3.shape
    T = rows // unit
    kv_shape = jax.ShapeDtypeStruct((1, rows, 2 * B_GW), F32)
    kv_spec = pl.BlockSpec((1, rows, 2 * B_GW), lambda n, i: (0, 0, 0))
    t_shape = jax.ShapeDtypeStruct((T, 2 * B_GW, unit), F32)
    t_spec = pl.BlockSpec((T, 2 * B_GW, unit), lambda n, i: (0, 0, 0))
    return pl.pallas_call(
        functools.partial(_rope_sample_kernel, unit=unit),
        grid=(1, 1),
        in_specs=_rope_in_specs(rows),
        out_specs=[pl.BlockSpec((1, rows, B_W), lambda n, i: (0, 0, 0))] + [kv_spec] * 3 + [t_spec] * 3,
        out_shape=[jax.ShapeDtypeStruct((1, rows, B_W), F32)] + [kv_shape] * 3 + [t_shape] * 3,
        compiler_params=_cparams(("arbitrary", "arbitrary")),
        name="rope_sample",
    )(proj3, proj3, proj3, *tabs)


def _rope_tables(pos):
    half = B_ROT // 2
    inv = ROPE_THETA ** (-jnp.arange(half, dtype=F32) / half)
    ang = pos.astype(F32)[:, None] * inv
    cos, sin = jnp.cos(ang), jnp.sin(ang)
    rows = pos.shape[0]
    one = jnp.ones((rows, B_HD - B_ROT), F32)
    zero = jnp.zeros((rows, half), F32)
    zrest = jnp.zeros((rows, B_HD - B_ROT), F32)
    c = jnp.concatenate([cos, cos, one], axis=1)
    s1 = jnp.concatenate([-sin, zero, zrest], axis=1)
    s2 = jnp.concatenate([zero, sin, zrest], axis=1)
    tile = lambda t: jnp.concatenate([t, t], axis=1)[None]
    return tile(c), tile(s1), tile(s2)


def _head_lane_mask(rows):
    lane = lax.broadcasted_iota(jnp.int32, (rows, B_GW), 1)
    return [(lane >= h * B_HD) & (lane < (h + 1) * B_HD) for h in range(B_HEADS)]


def _attn_prompt_kernel(q_ref, kvp_ref, kvc_ref, o_ref, l_ref):
    i = pl.program_id(2)
    nk = B_NK
    nblk = q_ref.shape[2] // nk
    q = q_ref[0, 0]
    kk = jnp.concatenate([kvp_ref[0, 0, :, 0:B_GW], kvc_ref[0, 0, :, 0:B_GW]], axis=0).astype(BF16)
    vv = jnp.concatenate([kvp_ref[0, 0, :, B_GW:], kvc_ref[0, 0, :, B_GW:]], axis=0).astype(BF16)
    qi = lax.broadcasted_iota(jnp.int32, (nk, 2 * nk), 0)
    kj = lax.broadcasted_iota(jnp.int32, (nk, 2 * nk), 1)
    band = (kj >= qi) & (kj <= qi + nk)
    band_first = band & ((kj >= nk) | (i > 0))
    masks = _head_lane_mask(nk)
    zq = jnp.zeros((nk, B_GW), BF16)
    chains = [(j, h) for j in range(nblk) for h in range(B_HEADS)]
    s, pn, lse, oh = {}, {}, {}, {}
    for j, h in chains:
        s[j, h] = _dot_nt(jnp.where(masks[h], q[j * nk:(j + 1) * nk], zq), kk[j * nk:(j + 2) * nk])
    for j, h in chains:
        sm = jnp.where(band_first if j == 0 else band, s[j, h], -jnp.inf)
        m = jnp.max(sm, axis=-1, keepdims=True)
        p = jnp.exp(sm - m)
        den = jnp.sum(p, axis=-1, keepdims=True)
        lse[j, h] = m + jnp.log(den)
        pn[j, h] = (p / den).astype(BF16)
    for j, h in chains:
        oh[j, h] = jnp.dot(pn[j, h], vv[j * nk:(j + 2) * nk], preferred_element_type=F32)
    for j in range(nblk):
        o_acc = jnp.zeros((nk, B_GW), F32)
        l_acc = jnp.zeros((nk, B_GW), F32)
        for h in range(B_HEADS):
            o_acc = jnp.where(masks[h], oh[j, h], o_acc)
            l_acc = jnp.where(masks[h], lse[j, h], l_acc)
        o_ref[0, 0, j * nk:(j + 1) * nk, :] = o_acc
        l_ref[0, 0, j * nk:(j + 1) * nk, :] = l_acc


AP_TQ = 512


def _attn_prompt_call(q, kv):
    B, dil, n, _ = q.shape
    tq = min(AP_TQ, n)
    per = tq // B_NK
    blk = lambda w: pl.BlockSpec((1, 1, tq, w), lambda b, r, i: (b, r, i, 0))
    out = jax.ShapeDtypeStruct((B, dil, n, B_GW), F32)
    return pl.pallas_call(
        _attn_prompt_kernel,
        grid=(B, dil, n // tq),
        in_specs=[blk(B_GW),
                  pl.BlockSpec((1, 1, B_NK, 2 * B_GW), lambda b, r, i: (b, r, jnp.maximum(i * per - 1, 0), 0)),
                  blk(2 * B_GW)],
        out_specs=[blk(B_GW), blk(B_GW)],
        out_shape=[out, out],
        compiler_params=_cparams(("parallel", "parallel", "arbitrary")),
        name="attn_prompt_d%d" % dil,
    )(q, kv, kv)


AS_QB = 16
AS_CACHE_BYTES = 16 * 1024 * 1024
AS_R = 8
AS_NEW = 16


def _attn_sample_kernel(q_ref, kvn_ref, c_ref, o_ref, l_ref, *, dil, g, sub_blocks):
    T = q_ref.shape[0]
    cb, Lb = c_ref.shape[1], c_ref.shape[3]
    R, NR = AS_R, B_HEADS * AS_R
    sub = pl.program_id(0) % sub_blocks
    row = lax.broadcasted_iota(jnp.int32, (NR, B_GW), 0)
    lane = lax.broadcasted_iota(jnp.int32, (NR, B_GW), 1)
    hmask = (lane // B_HD) == (row // R)

    def query_of(shape):
        t = lax.broadcasted_iota(jnp.int32, shape, 0) % R
        return jnp.where(t < T, t, 0)

    pos = lax.broadcasted_iota(jnp.int32, (NR, Lb), 1)
    tn = lax.broadcasted_iota(jnp.int32, (NR, AS_NEW), 1)
    if dil == 1:
        valid_c = pos >= query_of((NR, Lb))
        valid_n = tn <= query_of((NR, AS_NEW))
    else:
        valid_c = (pos % dil) == query_of((NR, Lb))
        valid_n = tn == query_of((NR, AS_NEW))

    for bi in range(cb):
        b = sub * cb + bi
        qrows = jnp.concatenate([q_ref[t, pl.ds(b, 1), g * B_GW:(g + 1) * B_GW] for t in range(T)]
                                + [jnp.zeros((R - T, B_GW), F32)], axis=0)
        qbd = jnp.where(hmask, jnp.concatenate([qrows] * B_HEADS, axis=0), 0.0)
        new = jnp.concatenate([kvn_ref[t, pl.ds(b, 1), :] for t in range(T)]
                              + [jnp.zeros((AS_NEW - T, 2 * B_GW), F32)], axis=0)
        s = jnp.where(valid_c, _dot(qbd, c_ref[0, bi, 0:B_GW, :]), -jnp.inf)
        sn = jnp.where(valid_n, _dot_nt(qbd, new[:, 0:B_GW]), -jnp.inf)
        m = jnp.maximum(jnp.max(s, axis=-1, keepdims=True), jnp.max(sn, axis=-1, keepdims=True))
        p = jnp.exp(s - m)
        pn = jnp.exp(sn - m)
        den = jnp.sum(p, axis=-1, keepdims=True) + jnp.sum(pn, axis=-1, keepdims=True)
        o = _dot_nt(p / den, c_ref[0, bi, B_GW:2 * B_GW, :]) + _dot(pn / den, new[:, B_GW:])
        om = jnp.where(hmask, o, 0.0)
        lm = jnp.where(hmask, m + jnp.log(den), 0.0)
        ob = om[0:R]
        lb = lm[0:R]
        for h in range(1, B_HEADS):
            ob = ob + om[h * R:(h + 1) * R]
            lb = lb + lm[h * R:(h + 1) * R]
        for t in range(T):
            o_ref[t, pl.ds(b, 1), :] = ob[t:t + 1]
            l_ref[t, pl.ds(b, 1), :] = lb[t:t + 1]


def _attn_sample_call(q, kvn, cache_t, layer, g, dil):
    T, B, _ = q.shape
    Lb = cache_t.shape[3]
    qb = min(AS_QB, B)
    cb = max(1, min(qb, AS_CACHE_BYTES // (2 * B_GW * Lb * 4)))
    sub_blocks = qb // cb
    out = jax.ShapeDtypeStruct((T, B, B_GW), F32)
    row_spec = lambda w: pl.BlockSpec((T, qb, w), lambda i: (0, i // sub_blocks, 0))
    return pl.pallas_call(
        functools.partial(_attn_sample_kernel, dil=dil, g=g, sub_blocks=sub_blocks),
        grid=(B // cb,),
        in_specs=[row_spec(B_W), row_spec(2 * B_GW),
                  pl.BlockSpec((1, cb, 2 * B_GW, Lb), lambda i: (layer, i, 0, 0))],
        out_specs=[row_spec(B_GW), row_spec(B_GW)],
        out_shape=[out, out],
        compiler_params=_cparams(("arbitrary",)),
        name="attn_sample_d%d" % dil,
    )(q, kvn, cache_t)


def _shift_rows(x, s, fill):
    n = x.shape[0]
    rolled = pltpu.roll(x, s, 0)
    r = lax.broadcasted_iota(jnp.int32, x.shape, 0)
    return jnp.where(r < s, fill, rolled)


def _rglru_kernel(x_ref, g_ref, prev_ref, h0_ref, cw_ref, cb_ref, wr_ref, br_ref, wi_ref, bi_ref, lam_ref,
                  o_ref, hl_ref, xx_ref, hc_ref, *, unit, hp):
    i = pl.program_id(1)
    tm = x_ref.shape[1]
    halo = (C_CONV - 1) * unit

    @pl.when(i == 0)
    def _():
        xx_ref[hp - halo:hp, :] = prev_ref[0]
        hc_ref[...] = h0_ref[0]

    x = x_ref[0].astype(F32)
    xx_ref[hp:hp + tm, :] = x
    y = cb_ref[...] + xx_ref[hp - halo:hp - halo + tm, :] * cw_ref[0:1, :]
    for j in range(1, C_CONV):
        y = y + xx_ref[hp - halo + j * unit:hp - halo + j * unit + tm, :] * cw_ref[j:j + 1, :]
    tail = xx_ref[hp + tm - halo:hp + tm, :]
    xx_ref[hp - halo:hp, :] = tail

    r = _sigmoid(_dot(y, wr_ref[...]) + br_ref[...])
    ig = _sigmoid(_dot(y, wi_ref[...]) + bi_ref[...])
    log_a = -C_POW * r * _softplus(-lam_ref[...])
    a = jnp.exp(log_a)
    th = jnp.tanh(log_a)
    bx = jnp.sqrt(-2.0 * th / (1.0 - th)) * (ig * y)
    hc = hc_ref[...]
    if unit == 1:
        rr = lax.broadcasted_iota(jnp.int32, bx.shape, 0)
        bx = jnp.where(rr < 1, a * hc + bx, bx)
    elif unit == tm:
        bx = a * hc + bx
    else:
        bx = jnp.concatenate([a[0:unit] * hc + bx[0:unit], bx[unit:]], axis=0)
    s = unit
    while s < tm:
        if s % 8 == 0:
            bx = jnp.concatenate([bx[0:s], a[s:] * bx[0:tm - s] + bx[s:]], axis=0)
            a = jnp.concatenate([a[0:s], a[s:] * a[0:tm - s]], axis=0)
        else:
            a_sh = _shift_rows(a, s, 1.0)
            b_sh = _shift_rows(bx, s, 0.0)
            bx = a * b_sh + bx
            a = a * a_sh
        s *= 2
    h = bx
    hl = h[tm - unit:tm, :]
    hc_ref[...] = hl
    hl_ref[0] = hl
    o_ref[0] = h * jax.nn.gelu(g_ref[0].astype(F32))


def _rglru_call(proj3, prev, h0, cw, cb, wr, br, wi, bi, lam, unit, tm):
    nseq, rows, _ = proj3.shape
    halo = (C_CONV - 1) * unit
    hp = -(-halo // 8) * 8
    assert unit == 1 or unit % 8 == 0
    vec = lambda: pl.BlockSpec((1, C_WIDTH), lambda n, i: (0, 0))
    return pl.pallas_call(
        functools.partial(_rglru_kernel, unit=unit, hp=hp),
        grid=(nseq, rows // tm),
        in_specs=[pl.BlockSpec((1, tm, C_WIDTH), lambda n, i: (n, i, OFF_XC // C_WIDTH)),
                  pl.BlockSpec((1, tm, C_WIDTH), lambda n, i: (n, i, OFF_GC // C_WIDTH)),
                  pl.BlockSpec((1, halo, C_WIDTH), lambda n, i: (n, 0, 0)),
                  pl.BlockSpec((1, unit, C_WIDTH), lambda n, i: (n, 0, 0)),
                  pl.BlockSpec((C_CONV, C_WIDTH), lambda n, i: (0, 0)), vec(),
                  pl.BlockSpec((C_WIDTH, C_WIDTH), lambda n, i: (0, 0)), vec(),
                  pl.BlockSpec((C_WIDTH, C_WIDTH), lambda n, i: (0, 0)), vec(), vec()],
        out_specs=[pl.BlockSpec((1, tm, C_WIDTH), lambda n, i: (n, i, 0)),
                   pl.BlockSpec((1, unit, C_WIDTH), lambda n, i: (n, 0, 0))],
        out_shape=[jax.ShapeDtypeStruct((nseq, rows, C_WIDTH), F32),
                   jax.ShapeDtypeStruct((nseq, unit, C_WIDTH), F32)],
        scratch_shapes=[pltpu.VMEM((hp + tm, C_WIDTH), F32), pltpu.VMEM((unit, C_WIDTH), F32)],
        compiler_params=_cparams(("parallel", "arbitrary")),
        name="rglru_u%d" % unit,
    )(proj3, proj3, prev, h0, cw, cb, wr, br, wi, bi, lam)


def _merge_kernel(x_ref, ga_ref, gb_ref, gc_ref, oa_ref, oc_ref,
                  o0_ref, o1_ref, o2_ref, l0_ref, l1_ref, l2_ref,
                  wpa_ref, wpb_ref, wpc_ref, wo_ref, g_ref, b_ref, out_ref, il_ref, *, dils):
    tm = x_ref.shape[1]

    def rows_in_order(ref, dil):
        if dil == 1:
            return ref[0, 0]
        for r in range(dil):
            for j in range(il_ref.shape[0]):
                il_ref[j, pl.ds(r, tm // dil, stride=dil), :] = ref[0, r, :, j * 128:(j + 1) * 128]
        return jnp.concatenate([il_ref[j] for j in range(il_ref.shape[0])], axis=1)

    o0, o1, o2 = (rows_in_order(r, d) for r, d in zip((o0_ref, o1_ref, o2_ref), dils))
    l0, l1, l2 = (rows_in_order(r, d) for r, d in zip((l0_ref, l1_ref, l2_ref), dils))
    m = jnp.maximum(jnp.maximum(l0, l1), l2)
    e0, e1, e2 = jnp.exp(l0 - m), jnp.exp(l1 - m), jnp.exp(l2 - m)
    den = e0 + e1 + e2
    ob = (e0 / den) * o0 + (e1 / den) * o1 + (e2 / den) * o2
    merged = (_sigmoid(ga_ref[0].astype(F32)) * _dot(oa_ref[0], wpa_ref[...])
              + _sigmoid(gb_ref[0].astype(F32)) * _dot(ob, wpb_ref[...])
              + _sigmoid(gc_ref[0].astype(F32)) * _dot(oc_ref[0], wpc_ref[...]))
    mix = _dot(merged, wo_ref[...])
    out_ref[0] = _layer_norm(DN_ALPHA * x_ref[0] + mix, g_ref[...], b_ref[...])


def _merge_call(x3, proj3, oa, oc, obs, lses, wpa, wpb, wpc, wo, g, b, tm):
    nseq, rows, _ = x3.shape
    dils = tuple(o.shape[1] for o in obs)
    row_spec = lambda w, j=0: pl.BlockSpec((1, tm, w), lambda n, i: (n, i, j))
    grp_spec = lambda d: pl.BlockSpec((1, d, tm // d, B_GW), lambda n, i: (n, 0, i, 0))
    full = lambda a: pl.BlockSpec(a.shape, lambda n, i: (0, 0))
    return pl.pallas_call(
        functools.partial(_merge_kernel, dils=dils),
        grid=(nseq, rows // tm),
        in_specs=[row_spec(D_MODEL), row_spec(D_MODEL, 0), row_spec(D_MODEL, 1), row_spec(D_MODEL, 2),
                  row_spec(A_VW), row_spec(C_WIDTH)]
                 + [grp_spec(d) for d in dils] * 2
                 + [full(wpa), full(wpb), full(wpc), full(wo), full(g), full(b)],
        out_specs=row_spec(D_MODEL),
        out_shape=jax.ShapeDtypeStruct((nseq, rows, D_MODEL), F32),
        scratch_shapes=[pltpu.VMEM((B_GW // 128, tm, 128), F32)],
        compiler_params=_cparams(("parallel", "parallel")),
        name="merge_ln",
    )(x3, proj3, proj3, proj3, oa, oc, *obs, *lses, wpa, wpb, wpc, wo, g, b)


def _ffn_kernel(x_ref, prev_ref, wu_ref, cw_ref, cb_ref, wd_ref, g_ref, b_ref,
                o_ref, st_ref, xx_ref, *, unit, hp):
    i = pl.program_id(1)
    tm = x_ref.shape[1]
    halo = (F_CONV - 1) * unit

    @pl.when(i == 0)
    def _():
        xx_ref[hp - halo:hp, :] = prev_ref[0]

    x = x_ref[0]
    xb = x.astype(BF16)
    f = jnp.zeros((tm, D_MODEL), F32)
    for c0 in range(0, D_FF, FF_CHUNK):
        c1 = c0 + FF_CHUNK
        gate = jnp.dot(xb, wu_ref[:, c0:c1], preferred_element_type=F32)
        up = jnp.dot(xb, wu_ref[:, D_FF + c0:D_FF + c1], preferred_element_type=F32)
        xx_ref[hp:hp + tm, c0:c1] = gate
        y = cb_ref[:, c0:c1] + xx_ref[hp - halo:hp - halo + tm, c0:c1] * cw_ref[0:1, c0:c1]
        for j in range(1, F_CONV):
            y = y + xx_ref[hp - halo + j * unit:hp - halo + j * unit + tm, c0:c1] * cw_ref[j:j + 1, c0:c1]
        tail = xx_ref[hp + tm - halo:hp + tm, c0:c1]
        xx_ref[hp - halo:hp, c0:c1] = tail
        st_ref[0, :, c0:c1] = tail
        f = f + _dot(_silu(y) * up, wd_ref[c0:c1, :])
    o_ref[0] = _layer_norm(DN_ALPHA * x + f, g_ref[...], b_ref[...])


def _ffn_call(x3, prev, wu, cw, cb, wd, g, b, unit, tm):
    nseq, rows, _ = x3.shape
    halo = (F_CONV - 1) * unit
    hp = -(-halo // 8) * 8
    const = lambda a: pl.BlockSpec(a.shape, lambda n, i: (0, 0), pipeline_mode=pl.Buffered(1))
    return pl.pallas_call(
        functools.partial(_ffn_kernel, unit=unit, hp=hp),
        grid=(nseq, rows // tm),
        in_specs=[pl.BlockSpec((1, tm, D_MODEL), lambda n, i: (n, i, 0)),
                  pl.BlockSpec((1, halo, D_FF), lambda n, i: (n, 0, 0)),
                  const(wu), const(cw), const(cb), const(wd), const(g), const(b)],
        out_specs=[pl.BlockSpec((1, tm, D_MODEL), lambda n, i: (n, i, 0)),
                   pl.BlockSpec((1, halo, D_FF), lambda n, i: (n, 0, 0))],
        out_shape=[jax.ShapeDtypeStruct((nseq, rows, D_MODEL), F32),
                   jax.ShapeDtypeStruct((nseq, halo, D_FF), F32)],
        scratch_shapes=[pltpu.VMEM((hp + tm, D_FF), F32)],
        compiler_params=_cparams(("parallel", "arbitrary")),
        name="conv_ffn_u%d" % unit,
    )(x3, prev, wu, cw, cb, wd, g, b)


def _block_diag(w):
    nblk, bw, _ = w.shape
    eye = jnp.eye(nblk, dtype=w.dtype)
    return (eye[:, None, :, None] * w[:, :, None, :]).reshape(nblk * bw, nblk * bw)


def _layer_params(l, w_in, a_conv_w, a_A_log, a_dt_bias, a_norm_w, c_conv_w, c_conv_b, c_w_r, c_b_r,
                  c_w_i, c_b_i, c_lam, w_pa, w_pb, w_pc, w_o, ln1_g, ln1_b, f_up, f_conv_w, f_conv_b,
                  f_down, ln2_g, ln2_b):
    w = w_in[l]
    o_b = A_QKV
    o_a = o_b + A_HEADS
    o_z = o_a + A_HEADS
    o_qb = o_z + A_VW
    o_xc = o_qb + 3 * B_W
    o_gc = o_xc + C_WIDTH
    o_gt = o_gc + C_WIDTH
    pad = jnp.zeros((D_MODEL, N_PROJ - OFF_BA - 2 * A_HEADS), w.dtype)
    wp = jnp.concatenate([w[:, o_gt:o_gt + 3 * D_MODEL], w[:, o_xc:o_gc], w[:, o_gc:o_gt], w[:, o_z:o_qb],
                          w[:, 0:o_b], w[:, o_qb:o_xc], w[:, o_b:o_z], pad], axis=1).astype(BF16)
    gp = jnp.zeros((2, 128), F32)
    gp = gp.at[0, A_HEADS:2 * A_HEADS].set(a_A_log[l]).at[1, A_HEADS:2 * A_HEADS].set(a_dt_bias[l])
    row = lambda v: v.reshape(1, -1)
    return dict(
        wp=wp, a_cw=a_conv_w[l], gp=gp, a_nw=row(a_norm_w[l]),
        c_cw=c_conv_w[l], c_cb=row(c_conv_b[l]),
        c_wr=_block_diag(c_w_r[l]).astype(BF16), c_br=row(c_b_r[l]),
        c_wi=_block_diag(c_w_i[l]).astype(BF16), c_bi=row(c_b_i[l]), c_lam=row(c_lam[l]),
        wpa=w_pa[l].astype(BF16), wpb=w_pb[l].astype(BF16), wpc=w_pc[l].astype(BF16), wo=w_o[l].astype(BF16),
        ln1_g=row(ln1_g[l]), ln1_b=row(ln1_b[l]),
        f_up=f_up[l].astype(BF16), f_cw=f_conv_w[l], f_cb=row(f_conv_b[l]), f_down=f_down[l].astype(BF16),
        ln2_g=row(ln2_g[l]), ln2_b=row(ln2_b[l]))


def _layer_tail(x3, proj3, o_a, obs, lses, unit, p, c_prev, c_h0, f_prev, tm_merge, tm_seq, tm_ffn):
    o_c, c_hl = _rglru_call(proj3, c_prev, c_h0, p["c_cw"], p["c_cb"], p["c_wr"], p["c_br"], p["c_wi"],
                            p["c_bi"], p["c_lam"], unit, tm_seq)
    x1 = _merge_call(x3, proj3, o_a, o_c, obs, lses, p["wpa"], p["wpb"], p["wpc"], p["wo"],
                     p["ln1_g"], p["ln1_b"], tm_merge)
    x_out, f_state = _ffn_call(x1, f_prev, p["f_up"], p["f_cw"], p["f_cb"], p["f_down"],
                               p["ln2_g"], p["ln2_b"], unit, tm_ffn)
    return x_out, c_hl, f_state


def kernel(x_prompt, x_sample, state_a_conv, state_a_rec, cache_b_w128, cache_b_w512, cache_b_w2048,
           state_c_conv, state_c_h, state_f_conv, ln_in_g, ln_in_b, w_in, a_conv_w, a_A_log, a_dt_bias,
           a_norm_w, c_conv_w, c_conv_b, c_w_r, c_b_r, c_w_i, c_b_i, c_lam, w_pa, w_pb, w_pc, w_o,
           ln1_g, ln1_b, f_up, f_conv_w, f_conv_b, f_down, ln2_g, ln2_b):
    Bp, Sp, _ = x_prompt.shape
    Bs, Ts, _ = x_sample.shape
    caches = (cache_b_w128, cache_b_w512, cache_b_w2048)
    for (win, dil), cache in zip(B_GROUPS, caches):
        assert win // dil == B_NK and cache.shape[2] == win and Sp % (dil * B_NK) == 0

    caches_t = [jnp.transpose(c, (0, 1, 3, 4, 5, 2)).reshape(DEPTH, Bs, 2 * B_GW, c.shape[2]) for c in caches]
    Rs = Ts * Bs

    lng, lnb = ln_in_g.reshape(1, -1), ln_in_b.reshape(1, -1)
    hp = _ln_call(x_prompt.reshape(Bp * Sp, D_MODEL), lng, lnb, 512)
    xs_tm = jnp.swapaxes(x_sample, 0, 1).reshape(Rs, D_MODEL)
    hs = _ln_call(xs_tm, lng, lnb, Rs)

    tabs_p = _rope_tables(jnp.arange(Sp))
    tabs_s = _rope_tables(jnp.repeat(PAST_LEN + jnp.arange(Ts), Bs))

    zeros_c = jnp.zeros((Bp, C_CONV - 1, C_WIDTH), F32)
    zeros_h = jnp.zeros((Bp, 1, C_WIDTH), F32)
    zeros_f = jnp.zeros((Bp, F_CONV - 1, D_FF), F32)

    outs_p, outs_s = [], []
    for l in range(DEPTH):
        p = _layer_params(l, w_in, a_conv_w, a_A_log, a_dt_bias, a_norm_w, c_conv_w, c_conv_b, c_w_r, c_b_r,
                          c_w_i, c_b_i, c_lam, w_pa, w_pb, w_pc, w_o, ln1_g, ln1_b, f_up, f_conv_w,
                          f_conv_b, f_down, ln2_g, ln2_b)

        proj3 = _proj_call(hp, p["wp"], 1024, BF16).reshape(Bp, Sp, N_PROJ)
        hp_tail = hp.reshape(Bp, Sp, D_MODEL)[:, Sp - TAIL_ROWS:].reshape(Bp * TAIL_ROWS, D_MODEL)
        proj_tail = _proj_call(hp_tail, p["wp"], Bp * TAIL_ROWS, F32).reshape(Bp, TAIL_ROWS, N_PROJ)
        o_a, a_rec = _gdn_prompt_call(proj3, p["a_cw"], p["gp"], p["a_nw"])
        rp = _rope_prompt_call(proj3, tabs_p, 256)
        res = [_attn_prompt_call(rp[g], rp[3 + g]) for g in range(len(B_GROUPS))]
        hp3, c_hl, f_st = _layer_tail(hp.reshape(Bp, Sp, D_MODEL), proj3, o_a, [r[0] for r in res],
                                      [r[1] for r in res], 1, p, zeros_c, zeros_h, zeros_f, 256, 256, 512)
        hp = hp3.reshape(Bp * Sp, D_MODEL)
        kv_rows_p = [t.reshape(Bp, 2, B_HEADS, B_HD, t.shape[-1]).transpose(0, 4, 1, 2, 3) for t in rp[6:9]]
        outs_p.append((
            proj_tail[:, TAIL_ROWS - (A_CONV - 1):, OFF_QKVA:OFF_QKVA + A_QKV],
            a_rec, kv_rows_p[0], kv_rows_p[1], kv_rows_p[2],
            proj_tail[:, TAIL_ROWS - (C_CONV - 1):, OFF_XC:OFF_XC + C_WIDTH],
            c_hl[:, 0],
            f_st))

        proj3 = _proj_call(hs, p["wp"], Rs, F32).reshape(1, Rs, N_PROJ)
        pt = proj3.reshape(Ts, Bs, N_PROJ)
        a_st = jnp.swapaxes(state_a_conv[l], 0, 1)
        o_a, a_rec = _gdn_sample_call(pt, a_st, state_a_rec, l, p["a_cw"], p["gp"], p["a_nw"])
        rs = _rope_sample_call(proj3, tabs_s, Bs)
        q3 = rs[0].reshape(Ts, Bs, B_W)
        res = [_attn_sample_call(q3, rs[1 + g].reshape(Ts, Bs, 2 * B_GW), caches_t[g], l, g, dil)
               for g, (_, dil) in enumerate(B_GROUPS)]
        c_prev = jnp.swapaxes(state_c_conv[l], 0, 1).reshape(1, (C_CONV - 1) * Bs, C_WIDTH)
        f_prev = jnp.swapaxes(state_f_conv[l], 0, 1).reshape(1, (F_CONV - 1) * Bs, D_FF)
        hs3, c_hl, f_st = _layer_tail(hs.reshape(1, Rs, D_MODEL), proj3, o_a.reshape(1, Rs, A_VW),
                                      [r[0].reshape(1, 1, Rs, B_GW) for r in res],
                                      [r[1].reshape(1, 1, Rs, B_GW) for r in res],
                                      Bs, p, c_prev, state_c_h[l][None], f_prev, min(256, Rs), Rs, Rs)
        hs = hs3.reshape(Rs, D_MODEL)
        tm2bm = lambda a: jnp.swapaxes(a, 0, 1)
        kv_rows_s = [t.reshape(Ts, 2, B_HEADS, B_HD, Bs).transpose(4, 0, 1, 2, 3) for t in rs[4:7]]
        outs_s.append((
            tm2bm(pt[Ts - (A_CONV - 1):, :, OFF_QKVA:OFF_QKVA + A_QKV]),
            a_rec, kv_rows_s[0], kv_rows_s[1], kv_rows_s[2],
            tm2bm(pt[Ts - (C_CONV - 1):, :, OFF_XC:OFF_XC + C_WIDTH]),
            c_hl[0],
            tm2bm(f_st.reshape(F_CONV - 1, Bs, D_FF))))

    stack = lambda outs: [jnp.stack(s, 0) for s in zip(*outs)]
    y_p = hp.reshape(Bp, Sp, D_MODEL)
    y_s = jnp.swapaxes(hs.reshape(Ts, Bs, D_MODEL), 0, 1)
    return (y_p, y_s, *stack(outs_p), *stack(outs_s))
```

```python
import functools
import math

import numpy as np
import jax
import jax.numpy as jnp
from jax import lax
from jax.experimental import pallas as pl
from jax.experimental.pallas import tpu as pltpu

F32 = jnp.float32
BF16 = jnp.bfloat16

D_MODEL = 1024
DEPTH = 4
PAST_LEN = 2048
A_HEADS, A_DK, A_DV, A_CONV, A_CHUNK = 4, 128, 128, 4, 64
B_GROUPS = ((128, 1), (512, 4), (2048, 16))
B_HEADS, B_HD = 4, 64
B_ROT = B_HD // 4
ROPE_THETA = 500000.0
B_NK = 128
C_WIDTH, C_BLOCKS, C_CONV, C_POW = 512, 8, 4, 8.0
C_BW = C_WIDTH // C_BLOCKS
D_FF = ((8 * D_MODEL // 3 + 255) // 256) * 256
F_CONV = 3
FF_CHUNK = 2816
DN_ALPHA = (2 * DEPTH) ** 0.25
LN_EPS = 1e-5
NORM_EPS = 1e-6

A_QK = A_HEADS * A_DK
A_VW = A_HEADS * A_DV
A_QKV = 2 * A_QK + A_VW
B_GW = B_HEADS * B_HD
B_W = len(B_GROUPS) * B_GW

OFF_GATES = 0
OFF_XC = 3072
OFF_GC = 3584
OFF_Z = 4096
OFF_QKVA = 4608
OFF_QB = 6144
OFF_BA = 8448
N_PROJ = 8704
PROJ_TN = 2176
TAIL_ROWS = 8

VMEM_LIMIT = 56 * 1024 * 1024


def _cparams(sem, **kw):
    return pltpu.CompilerParams(dimension_semantics=sem, vmem_limit_bytes=VMEM_LIMIT, **kw)


def _sigmoid(x):
    return 1.0 / (1.0 + jnp.exp(-x))


def _silu(x):
    return x * _sigmoid(x)


def _softplus(x):
    return jnp.maximum(x, 0.0) + jnp.log1p(jnp.exp(-jnp.abs(x)))


def _layer_norm(x, g, b):
    mu = jnp.mean(x, axis=-1, keepdims=True)
    xc = x - mu
    var = jnp.mean(xc * xc, axis=-1, keepdims=True)
    return xc * lax.rsqrt(var + LN_EPS) * g + b


def _dot(a, b):
    return jnp.dot(a.astype(BF16), b.astype(BF16), preferred_element_type=F32)


def _dot_nt(a, b):
    return lax.dot_general(a.astype(BF16), b.astype(BF16), (((1,), (1,)), ((), ())),
                           preferred_element_type=F32)


def _dot_tn(a, b):
    return lax.dot_general(a.astype(BF16), b.astype(BF16), (((0,), (0,)), ((), ())),
                           preferred_element_type=F32)


def _split2(a):
    hi = a.astype(BF16)
    lo = (a - hi.astype(F32)).astype(BF16)
    return hi, lo


def _dot3(a, b):
    ah, al = _split2(a)
    bh, bl = _split2(b)
    d = lambda x, y: jnp.dot(x, y, preferred_element_type=F32)
    return d(ah, bh) + (d(ah, bl) + d(al, bh))


def _ln_kernel(x_ref, g_ref, b_ref, o_ref):
    o_ref[...] = _layer_norm(x_ref[...], g_ref[...], b_ref[...])


def _ln_call(x, g, b, tm):
    rows = x.shape[0]
    return pl.pallas_call(
        _ln_kernel,
        grid=(rows // tm,),
        in_specs=[pl.BlockSpec((tm, D_MODEL), lambda i: (i, 0)),
                  pl.BlockSpec((1, D_MODEL), lambda i: (0, 0)),
                  pl.BlockSpec((1, D_MODEL), lambda i: (0, 0))],
        out_specs=pl.BlockSpec((tm, D_MODEL), lambda i: (i, 0)),
        out_shape=jax.ShapeDtypeStruct((rows, D_MODEL), F32),
        compiler_params=_cparams(("parallel",)),
        name="entry_ln",
    )(x, g, b)


def _proj_kernel(x_ref, w_ref, o_ref, xb_ref):
    @pl.when(pl.program_id(1) == 0)
    def _():
        xb_ref[...] = x_ref[...].astype(BF16)

    o_ref[...] = jnp.dot(xb_ref[...], w_ref[...], preferred_element_type=F32).astype(o_ref.dtype)


def _proj_call(x, w, tm, out_dtype):
    rows = x.shape[0]
    return pl.pallas_call(
        _proj_kernel,
        grid=(rows // tm, N_PROJ // PROJ_TN),
        in_specs=[pl.BlockSpec((tm, D_MODEL), lambda i, j: (i, 0)),
                  pl.BlockSpec((D_MODEL, PROJ_TN), lambda i, j: (0, j))],
        out_specs=pl.BlockSpec((tm, PROJ_TN), lambda i, j: (i, j)),
        out_shape=jax.ShapeDtypeStruct((rows, N_PROJ), out_dtype),
        scratch_shapes=[pltpu.VMEM((tm, D_MODEL), BF16)],
        compiler_params=_cparams(("parallel", "arbitrary")),
        name="in_proj",
    )(x, w)


def _gdn_prompt_kernel(x_ref, z_ref, ba_ref, cw_ref, gp_ref, nw_ref,
                       o_ref, sfin_ref, xx_ref, s_ref):
    c = pl.program_id(0)
    C = A_CHUNK
    nb = x_ref.shape[0]

    @pl.when(c == 0)
    def _():
        xx_ref[...] = jnp.zeros((nb, 8, A_QKV), F32)
        s_ref[...] = jnp.zeros(s_ref.shape, F32)

    ri = lax.broadcasted_iota(jnp.int32, (C, C), 0)
    ci = lax.broadcasted_iota(jnp.int32, (C, C), 1)
    tril = ri >= ci
    strict = ri > ci
    eye = (ri == ci).astype(F32)
    trilf = tril.astype(F32)
    nw = nw_ref[...]

    ch = []
    for b in range(nb):
        x = x_ref[b].astype(F32)
        xcat = jnp.concatenate([xx_ref[b], x], axis=0)
        y = pltpu.roll(xcat, A_CONV - 1, 0)[8:8 + C, :] * cw_ref[0:1, :]
        for j in range(1, A_CONV - 1):
            y = y + pltpu.roll(xcat, A_CONV - 1 - j, 0)[8:8 + C, :] * cw_ref[j:j + 1, :]
        y = y + x * cw_ref[A_CONV - 1:A_CONV, :]
        xx_ref[b] = x[C - 8:C, :]
        y = _silu(y)
        ba = ba_ref[b].astype(F32)
        beta_all = _sigmoid(ba)
        g_all = -jnp.exp(gp_ref[0:1, :]) * _softplus(ba + gp_ref[1:2, :])
        gc = _dot3(trilf, g_all)
        gct = jnp.concatenate([gc, jnp.zeros((128 - C, 128), F32)], axis=0).T
        for h in range(A_HEADS):
            q = y[:, h * A_DK:(h + 1) * A_DK]
            k = y[:, A_QK + h * A_DK:A_QK + (h + 1) * A_DK]
            v = y[:, 2 * A_QK + h * A_DV:2 * A_QK + (h + 1) * A_DV]
            q = q * lax.rsqrt(jnp.sum(q * q, axis=-1, keepdims=True) + NORM_EPS) * (A_DK ** -0.5)
            k = k * lax.rsqrt(jnp.sum(k * k, axis=-1, keepdims=True) + NORM_EPS)
            beta = beta_all[:, h:h + 1]
            gcol = gc[:, 4 + h:5 + h]
            grow = gct[4 + h:5 + h, 0:C]
            decay = jnp.where(tril, jnp.exp(jnp.where(tril, gcol - grow, 0.0)), 0.0)
            eg = jnp.exp(gcol)
            glast = gcol[C - 1:C, :]
            ch.append(dict(b=b, h=h, q=q, k=k, kb=k * beta, vb=v * beta, decay=decay, eg=eg,
                           k_dec=k * jnp.exp(glast - gcol), q_dec=q * eg, eglast=jnp.exp(glast)))
    for d in ch:
        d["m"] = -jnp.where(strict, _dot_nt(d["kb"], d["k"]) * d["decay"], 0.0)
        d["t"] = eye + d["m"]
    for d in ch:
        d["qk"] = jnp.where(tril, _dot_nt(d["q"], d["k"]) * d["decay"], 0.0)
    for _ in range(5):
        for d in ch:
            d["m"] = _dot(d["m"], d["m"])
        for d in ch:
            d["t"] = d["t"] + _dot(d["t"], d["m"])
    for d in ch:
        d["u"] = _dot(d["t"], d["vb"])
        d["w"] = _dot(d["t"], d["kb"] * d["eg"])
    for d in ch:
        d["s"] = s_ref[d["b"], d["h"]]
        d["v_new"] = d["u"] - _dot(d["w"], d["s"])
    for d in ch:
        d["o"] = _dot(d["q_dec"], d["s"]) + _dot(d["qk"], d["v_new"])
        s_ref[d["b"], d["h"]] = d["s"] * d["eglast"] + _dot_tn(d["k_dec"], d["v_new"])
    for d in ch:
        b, h, o = d["b"], d["h"], d["o"]
        o = o * lax.rsqrt(jnp.mean(o * o, axis=-1, keepdims=True) + NORM_EPS) * nw
        o_ref[b, :, h * A_DV:(h + 1) * A_DV] = o * _silu(z_ref[b, :, h * A_DV:(h + 1) * A_DV].astype(F32))

    @pl.when(c == pl.num_programs(0) - 1)
    def _():
        sfin_ref[...] = s_ref[...]


def _gdn_prompt_call(proj3, cw, gp, nw):
    B, S, _ = proj3.shape
    C = A_CHUNK
    return pl.pallas_call(
        _gdn_prompt_kernel,
        grid=(S // C,),
        in_specs=[pl.BlockSpec((B, C, A_QKV), lambda c: (0, c, OFF_QKVA // A_QKV)),
                  pl.BlockSpec((B, C, A_VW), lambda c: (0, c, OFF_Z // A_VW)),
                  pl.BlockSpec((B, C, 128), lambda c: (0, c, OFF_BA // 128)),
                  pl.BlockSpec((A_CONV, A_QKV), lambda c: (0, 0)),
                  pl.BlockSpec((2, 128), lambda c: (0, 0)),
                  pl.BlockSpec((1, A_DV), lambda c: (0, 0))],
        out_specs=[pl.BlockSpec((B, C, A_VW), lambda c: (0, c, 0)),
                   pl.BlockSpec((B, A_HEADS, A_DK, A_DV), lambda c: (0, 0, 0, 0))],
        out_shape=[jax.ShapeDtypeStruct((B, S, A_VW), F32),
                   jax.ShapeDtypeStruct((B, A_HEADS, A_DK, A_DV), F32)],
        scratch_shapes=[pltpu.VMEM((B, 8, A_QKV), F32),
                        pltpu.VMEM((B, A_HEADS, A_DK, A_DV), F32)],
        compiler_params=_cparams(("arbitrary",)),
        name="gdn_prompt",
    )(proj3, proj3, proj3, cw, gp, nw)


GS_BB = 8
GS_C = 16


def _gdn_sample_kernel(x_ref, z_ref, ba_ref, st_ref, s0_ref, cw_ref, gp_ref, nw_ref,
                       o_ref, snew_ref, oscr_ref):
    T = x_ref.shape[0]
    bb = GS_BB
    xx = [st_ref[j] for j in range(A_CONV - 1)] + [x_ref[t].astype(F32) for t in range(T)]
    C = GS_C
    assert T <= C
    qs, ks, kbs, vbs, gcs = [], [], [], [], []
    for t in range(T):
        y = xx[t] * cw_ref[0:1, :]
        for j in range(1, A_CONV):
            y = y + xx[t + j] * cw_ref[j:j + 1, :]
        y = _silu(y)
        ba = ba_ref[t].astype(F32)
        beta_all = _sigmoid(ba)
        g_all = -jnp.exp(gp_ref[0:1, :]) * _softplus(ba + gp_ref[1:2, :])
        gcs.append(g_all if t == 0 else gcs[-1] + g_all)
        qs.append([]), ks.append([]), kbs.append([]), vbs.append([])
        for h in range(A_HEADS):
            q = y[:, h * A_DK:(h + 1) * A_DK]
            k = y[:, A_QK + h * A_DK:A_QK + (h + 1) * A_DK]
            v = y[:, 2 * A_QK + h * A_DV:2 * A_QK + (h + 1) * A_DV]
            q = q * lax.rsqrt(jnp.sum(q * q, axis=-1, keepdims=True) + NORM_EPS) * (A_DK ** -0.5)
            k = k * lax.rsqrt(jnp.sum(k * k, axis=-1, keepdims=True) + NORM_EPS)
            beta = beta_all[:, h:h + 1]
            qs[t].append(q), ks[t].append(k), kbs[t].append(k * beta), vbs[t].append(v * beta)

    ri = lax.broadcasted_iota(jnp.int32, (C, C), 0)
    ci = lax.broadcasted_iota(jnp.int32, (C, C), 1)
    tril = ri >= ci
    strict = ri > ci
    eye = (ri == ci).astype(F32)
    zrows = jnp.zeros((C - T, A_DK), F32)
    rows_of = lambda slabs, b, h: jnp.concatenate([slabs[t][h][b:b + 1, :] for t in range(T)] + [zrows], axis=0)
    ch = []
    for h in range(A_HEADS):
        for b in range(bb):
            gcol = jnp.concatenate([gcs[t][b:b + 1, 4 + h:5 + h] for t in range(T)]
                                   + [gcs[T - 1][b:b + 1, 4 + h:5 + h]] * (C - T), axis=0)
            grow = jnp.sum(eye * gcol, axis=0, keepdims=True)
            decay = jnp.where(tril, jnp.exp(jnp.where(tril, gcol - grow, 0.0)), 0.0)
            eg = jnp.exp(gcol)
            glast = gcol[C - 1:C, :]
            q, k, kb, vb = (rows_of(s, b, h) for s in (qs, ks, kbs, vbs))
            ch.append(dict(b=b, h=h, q=q, k=k, kb=kb, vb=vb, decay=decay, eg=eg,
                           k_dec=k * jnp.exp(glast - gcol), q_dec=q * eg, eglast=jnp.exp(glast)))
    for d in ch:
        d["m"] = -jnp.where(strict, _dot_nt(d["kb"], d["k"]) * d["decay"], 0.0)
        d["qk"] = jnp.where(tril, _dot_nt(d["q"], d["k"]) * d["decay"], 0.0)
    for d in ch:
        d["t"] = eye + d["m"]
    for _ in range(GS_C.bit_length() - 2):
        for d in ch:
            d["m"] = _dot(d["m"], d["m"])
        for d in ch:
            d["t"] = d["t"] + _dot(d["t"], d["m"])
    for d in ch:
        d["u"] = _dot(d["t"], d["vb"])
        d["w"] = _dot(d["t"], d["kb"] * d["eg"])
    for d in ch:
        d["s"] = s0_ref[0, d["b"], d["h"]]
        ws = _dot(jnp.concatenate([d["w"], d["q_dec"]], axis=0), d["s"])
        d["v_new"] = d["u"] - ws[0:C]
        d["qs"] = ws[C:2 * C]
    for d in ch:
        b, h = d["b"], d["h"]
        o = d["qs"] + _dot(d["qk"], d["v_new"])
        snew_ref[b, h] = d["s"] * d["eglast"] + _dot_tn(d["k_dec"], d["v_new"])
        for t in range(T):
            oscr_ref[t, b:b + 1, h * A_DV:(h + 1) * A_DV] = o[t:t + 1, :]
    nw = nw_ref[...]
    for t in range(T):
        z = z_ref[t].astype(F32)
        for h in range(A_HEADS):
            o = oscr_ref[t, :, h * A_DV:(h + 1) * A_DV]
            o = o * lax.rsqrt(jnp.mean(o * o, axis=-1, keepdims=True) + NORM_EPS) * nw
            o_ref[t, :, h * A_DV:(h + 1) * A_DV] = o * _silu(z[:, h * A_DV:(h + 1) * A_DV])


def _gdn_sample_call(proj3, st, s0, layer, cw, gp, nw):
    T, B, _ = proj3.shape
    bb = GS_BB
    return pl.pallas_call(
        _gdn_sample_kernel,
        grid=(B // bb,),
        in_specs=[pl.BlockSpec((T, bb, A_QKV), lambda i: (0, i, OFF_QKVA // A_QKV)),
                  pl.BlockSpec((T, bb, A_VW), lambda i: (0, i, OFF_Z // A_VW)),
                  pl.BlockSpec((T, bb, 128), lambda i: (0, i, OFF_BA // 128)),
                  pl.BlockSpec((A_CONV - 1, bb, A_QKV), lambda i: (0, i, 0)),
                  pl.BlockSpec((1, bb, A_HEADS, A_DK, A_DV), lambda i: (layer, i, 0, 0, 0)),
                  pl.BlockSpec((A_CONV, A_QKV), lambda i: (0, 0)),
                  pl.BlockSpec((2, 128), lambda i: (0, 0)),
                  pl.BlockSpec((1, A_DV), lambda i: (0, 0))],
        out_specs=[pl.BlockSpec((T, bb, A_VW), lambda i: (0, i, 0)),
                   pl.BlockSpec((bb, A_HEADS, A_DK, A_DV), lambda i: (i, 0, 0, 0))],
        out_shape=[jax.ShapeDtypeStruct((T, B, A_VW), F32),
                   jax.ShapeDtypeStruct(s0.shape[1:], F32)],
        scratch_shapes=[pltpu.VMEM((T, bb, A_VW), F32)],
        compiler_params=_cparams(("parallel",)),
        name="gdn_sample",
    )(proj3, proj3, proj3, st, s0, cw, gp, nw)


def _rope_qkv(q_ref, k_ref, v_ref, c_ref, s1_ref, s2_ref):
    reps = B_W // 128
    c = jnp.concatenate([c_ref[0]] * reps, axis=1)
    s1 = jnp.concatenate([s1_ref[0]] * reps, axis=1)
    s2 = jnp.concatenate([s2_ref[0]] * reps, axis=1)

    def rot(x):
        return x * c + pltpu.roll(x, B_W - B_ROT // 2, 1) * s1 + pltpu.roll(x, B_ROT // 2, 1) * s2

    return rot(q_ref[0].astype(F32)) * (B_HD ** -0.5), rot(k_ref[0].astype(F32)), v_ref[0].astype(F32)


def _rope_in_specs(tm):
    qb = OFF_QB // B_W
    tab_spec = pl.BlockSpec((1, tm, 128), lambda n, i: (0, i, 0))
    return [pl.BlockSpec((1, tm, B_W), lambda n, i: (n, i, qb)),
            pl.BlockSpec((1, tm, B_W), lambda n, i: (n, i, qb + 1)),
            pl.BlockSpec((1, tm, B_W), lambda n, i: (n, i, qb + 2)),
            tab_spec, tab_spec, tab_spec]


def _store_chunks(buf_ref, x):
    for j in range(buf_ref.shape[0]):
        buf_ref[j] = x[:, j * 128:(j + 1) * 128]


def _load_chunks_strided(buf_ref, start, size, stride):
    return jnp.concatenate([buf_ref[j, pl.ds(start, size, stride=stride), :] for j in range(buf_ref.shape[0])],
                           axis=1)


def _rope_prompt_kernel(q_ref, k_ref, v_ref, c_ref, s1_ref, s2_ref,
                        q0_ref, q1_ref, q2_ref, kv0_ref, kv1_ref, kv2_ref, t0_ref, t1_ref, t2_ref,
                        qs_ref, kvs_ref, *, tails):
    i = pl.program_id(1)
    tm = q_ref.shape[1]
    q, k, v = _rope_qkv(q_ref, k_ref, v_ref, c_ref, s1_ref, s2_ref)
    outs = ((q0_ref, kv0_ref, t0_ref), (q1_ref, kv1_ref, t1_ref), (q2_ref, kv2_ref, t2_ref))
    for g, ((_, dil), (qo_ref, kvo_ref, to_ref)) in enumerate(zip(B_GROUPS, outs)):
        lo, hi = g * B_GW, (g + 1) * B_GW
        qg = q[:, lo:hi]
        kv = jnp.concatenate([k[:, lo:hi], v[:, lo:hi]], axis=1)
        if dil == 1:
            qo_ref[0, 0] = qg.astype(BF16)
            kvo_ref[0, 0] = kv
        else:
            _store_chunks(qs_ref, qg)
            _store_chunks(kvs_ref, kv)
            for r in range(dil):
                qo_ref[0, r] = _load_chunks_strided(qs_ref, r, tm // dil, dil).astype(BF16)
                kvo_ref[0, r] = _load_chunks_strided(kvs_ref, r, tm // dil, dil)
        first, w = tails[g]

        @pl.when(i >= first)
        def _(to_ref=to_ref, w=w, kv=kv):
            to_ref[0] = kv[tm - w:tm, :].T


def _rope_prompt_call(proj3, tabs, tm):
    B, S, _ = proj3.shape
    nt = S // tm
    out_shape, out_specs, tails = [], [], []
    for width, dtype in ((B_GW, BF16), (2 * B_GW, F32)):
        for _, dil in B_GROUPS:
            out_shape.append(jax.ShapeDtypeStruct((B, dil, S // dil, width), dtype))
            out_specs.append(pl.BlockSpec((1, dil, tm // dil, width), lambda n, i: (n, 0, i, 0)))
    for win, _ in B_GROUPS:
        win = min(win, S)
        w = min(tm, win)
        first = nt - win // w
        tails.append((first, w))
        out_shape.append(jax.ShapeDtypeStruct((B, 2 * B_GW, win), F32))
        out_specs.append(pl.BlockSpec((1, 2 * B_GW, w),
                                      lambda n, i, first=first: (n, 0, jnp.maximum(i - first, 0))))
    return pl.pallas_call(
        functools.partial(_rope_prompt_kernel, tails=tuple(tails)),
        grid=(B, nt),
        in_specs=_rope_in_specs(tm),
        out_specs=out_specs,
        out_shape=out_shape,
        scratch_shapes=[pltpu.VMEM((B_GW // 128, tm, 128), F32), pltpu.VMEM((2 * B_GW // 128, tm, 128), F32)],
        compiler_params=_cparams(("parallel", "arbitrary")),
        name="rope_prompt",
    )(proj3, proj3, proj3, *tabs)


def _rope_sample_kernel(q_ref, k_ref, v_ref, c_ref, s1_ref, s2_ref,
                        qo_ref, kv0_ref, kv1_ref, kv2_ref, t0_ref, t1_ref, t2_ref, *, unit):
    q, k, v = _rope_qkv(q_ref, k_ref, v_ref, c_ref, s1_ref, s2_ref)
    qo_ref[0] = q
    for g, (kvo_ref, to_ref) in enumerate(((kv0_ref, t0_ref), (kv1_ref, t1_ref), (kv2_ref, t2_ref))):
        kv = jnp.concatenate([k[:, g * B_GW:(g + 1) * B_GW], v[:, g * B_GW:(g + 1) * B_GW]], axis=1)
        kvo_ref[0] = kv
        for t in range(q.shape[0] // unit):
            to_ref[t] = kv[t * unit:(t + 1) * unit, :].T


def _rope_sample_call(proj3, tabs, unit):
    _, rows, _ = proj3.shape
    T = rows // unit
    kv_shape = jax.ShapeDtypeStruct((1, rows, 2 * B_GW), F32)
    kv_spec = pl.BlockSpec((1, rows, 2 * B_GW), lambda n, i: (0, 0, 0))
    t_shape = jax.ShapeDtypeStruct((T, 2 * B_GW, unit), F32)
    t_spec = pl.BlockSpec((T, 2 * B_GW, unit), lambda n, i: (0, 0, 0))
    return pl.pallas_call(
        functools.partial(_rope_sample_kernel, unit=unit),
        grid=(1, 1),
        in_specs=_rope_in_specs(rows),
        out_specs=[pl.BlockSpec((1, rows, B_W), lambda n, i: (0, 0, 0))] + [kv_spec] * 3 + [t_spec] * 3,
        out_shape=[jax.ShapeDtypeStruct((1, rows, B_W), F32)] + [kv_shape] * 3 + [t_shape] * 3,
        compiler_params=_cparams(("arbitrary", "arbitrary")),
        name="rope_sample",
    )(proj3, proj3, proj3, *tabs)


def _rope_tables(pos):
    half = B_ROT // 2
    inv = ROPE_THETA ** (-jnp.arange(half, dtype=F32) / half)
    ang = pos.astype(F32)[:, None] * inv
    cos, sin = jnp.cos(ang), jnp.sin(ang)
    rows = pos.shape[0]
    one = jnp.ones((rows, B_HD - B_ROT), F32)
    zero = jnp.zeros((rows, half), F32)
    zrest = jnp.zeros((rows, B_HD - B_ROT), F32)
    c = jnp.concatenate([cos, cos, one], axis=1)
    s1 = jnp.concatenate([-sin, zero, zrest], axis=1)
    s2 = jnp.concatenate([zero, sin, zrest], axis=1)
    tile = lambda t: jnp.concatenate([t, t], axis=1)[None]
    return tile(c), tile(s1), tile(s2)


def _head_lane_mask(rows):
    lane = lax.broadcasted_iota(jnp.int32, (rows, B_GW), 1)
    return [(lane >= h * B_HD) & (lane < (h + 1) * B_HD) for h in range(B_HEADS)]


def _attn_prompt_kernel(q_ref, kvp_ref, kvc_ref, o_ref, l_ref):
    i = pl.program_id(2)
    nk = B_NK
    nr, nblk = q_ref.shape[1], q_ref.shape[2] // nk
    qi = lax.broadcasted_iota(jnp.int32, (nk, 2 * nk), 0)
    kj = lax.broadcasted_iota(jnp.int32, (nk, 2 * nk), 1)
    band = (kj >= qi) & (kj <= qi + nk)
    band_first = band & ((kj >= nk) | (i > 0))
    masks = _head_lane_mask(nk)
    zq = jnp.zeros((nk, B_GW), BF16)
    q, kk, vv = [], [], []
    for r in range(nr):
        q.append(q_ref[0, r])
        kk.append(jnp.concatenate([kvp_ref[0, r, :, 0:B_GW], kvc_ref[0, r, :, 0:B_GW]], axis=0).astype(BF16))
        vv.append(jnp.concatenate([kvp_ref[0, r, :, B_GW:], kvc_ref[0, r, :, B_GW:]], axis=0).astype(BF16))
    chains = [(r, j, h) for r in range(nr) for j in range(nblk) for h in range(B_HEADS)]
    s, pn, lse, oh = {}, {}, {}, {}
    for r, j, h in chains:
        s[r, j, h] = _dot_nt(jnp.where(masks[h], q[r][j * nk:(j + 1) * nk], zq), kk[r][j * nk:(j + 2) * nk])
    for r, j, h in chains:
        sm = jnp.where(band_first if j == 0 else band, s[r, j, h], -jnp.inf)
        m = jnp.max(sm, axis=-1, keepdims=True)
        p = jnp.exp(sm - m)
        den = jnp.sum(p, axis=-1, keepdims=True)
        lse[r, j, h] = m + jnp.log(den)
        pn[r, j, h] = (p / den).astype(BF16)
    for r, j, h in chains:
        oh[r, j, h] = jnp.dot(pn[r, j, h], vv[r][j * nk:(j + 2) * nk], preferred_element_type=F32)
    for r in range(nr):
        for j in range(nblk):
            o_acc = jnp.zeros((nk, B_GW), F32)
            l_acc = jnp.zeros((nk, B_GW), F32)
            for h in range(B_HEADS):
                o_acc = jnp.where(masks[h], oh[r, j, h], o_acc)
                l_acc = jnp.where(masks[h], lse[r, j, h], l_acc)
            o_ref[0, r, j * nk:(j + 1) * nk, :] = o_acc
            l_ref[0, r, j * nk:(j + 1) * nk, :] = l_acc


AP_ROWS = 512


def _attn_prompt_call(q, kv):
    B, dil, n, _ = q.shape
    tq = min(AP_ROWS, n)
    nr = min(dil, AP_ROWS // tq)
    per = tq // B_NK
    blk = lambda w: pl.BlockSpec((1, nr, tq, w), lambda b, r, i: (b, r, i, 0))
    out = jax.ShapeDtypeStruct((B, dil, n, B_GW), F32)
    return pl.pallas_call(
        _attn_prompt_kernel,
        grid=(B, dil // nr, n // tq),
        in_specs=[blk(B_GW),
                  pl.BlockSpec((1, nr, B_NK, 2 * B_GW), lambda b, r, i: (b, r, jnp.maximum(i * per - 1, 0), 0)),
                  blk(2 * B_GW)],
        out_specs=[blk(B_GW), blk(B_GW)],
        out_shape=[out, out],
        compiler_params=_cparams(("parallel", "parallel", "arbitrary")),
        name="attn_prompt_d%d" % dil,
    )(q, kv, kv)


AS_QB = 16
AS_CACHE_BYTES = 16 * 1024 * 1024
AS_R = 8
AS_NEW = 16


def _attn_sample_kernel(q_ref, kvn_ref, c_ref, o_ref, l_ref, *, dil, g, sub_blocks):
    T = q_ref.shape[0]
    cb, Lb = c_ref.shape[1], c_ref.shape[3]
    R, NR = AS_R, B_HEADS * AS_R
    sub = pl.program_id(0) % sub_blocks
    row = lax.broadcasted_iota(jnp.int32, (NR, B_GW), 0)
    lane = lax.broadcasted_iota(jnp.int32, (NR, B_GW), 1)
    hmask = (lane // B_HD) == (row // R)

    def query_of(shape):
        t = lax.broadcasted_iota(jnp.int32, shape, 0) % R
        return jnp.where(t < T, t, 0)

    pos = lax.broadcasted_iota(jnp.int32, (NR, Lb), 1)
    tn = lax.broadcasted_iota(jnp.int32, (NR, AS_NEW), 1)
    if dil == 1:
        valid_c = pos >= query_of((NR, Lb))
        valid_n = tn <= query_of((NR, AS_NEW))
    else:
        valid_c = (pos % dil) == query_of((NR, Lb))
        valid_n = tn == query_of((NR, AS_NEW))

    for bi in range(cb):
        b = sub * cb + bi
        qrows = jnp.concatenate([q_ref[t, pl.ds(b, 1), g * B_GW:(g + 1) * B_GW] for t in range(T)]
                                + [jnp.zeros((R - T, B_GW), F32)], axis=0)
        qbd = jnp.where(hmask, jnp.concatenate([qrows] * B_HEADS, axis=0), 0.0)
        new = jnp.concatenate([kvn_ref[t, pl.ds(b, 1), :] for t in range(T)]
                              + [jnp.zeros((AS_NEW - T, 2 * B_GW), F32)], axis=0)
        s = jnp.where(valid_c, _dot(qbd, c_ref[0, bi, 0:B_GW, :]), -jnp.inf)
        sn = jnp.where(valid_n, _dot_nt(qbd, new[:, 0:B_GW]), -jnp.inf)
        m = jnp.maximum(jnp.max(s, axis=-1, keepdims=True), jnp.max(sn, axis=-1, keepdims=True))
        p = jnp.exp(s - m)
        pn = jnp.exp(sn - m)
        den = jnp.sum(p, axis=-1, keepdims=True) + jnp.sum(pn, axis=-1, keepdims=True)
        o = _dot_nt(p / den, c_ref[0, bi, B_GW:2 * B_GW, :]) + _dot(pn / den, new[:, B_GW:])
        om = jnp.where(hmask, o, 0.0)
        lm = jnp.where(hmask, m + jnp.log(den), 0.0)
        ob = om[0:R]
        lb = lm[0:R]
        for h in range(1, B_HEADS):
            ob = ob + om[h * R:(h + 1) * R]
            lb = lb + lm[h * R:(h + 1) * R]
        for t in range(T):
            o_ref[t, pl.ds(b, 1), :] = ob[t:t + 1]
            l_ref[t, pl.ds(b, 1), :] = lb[t:t + 1]


def _attn_sample_call(q, kvn, cache_t, layer, g, dil):
    T, B, _ = q.shape
    Lb = cache_t.shape[3]
    qb = min(AS_QB, B)
    cb = max(1, min(qb, AS_CACHE_BYTES // (2 * B_GW * Lb * 4)))
    sub_blocks = qb // cb
    out = jax.ShapeDtypeStruct((T, B, B_GW), F32)
    row_spec = lambda w: pl.BlockSpec((T, qb, w), lambda i: (0, i // sub_blocks, 0))
    return pl.pallas_call(
        functools.partial(_attn_sample_kernel, dil=dil, g=g, sub_blocks=sub_blocks),
        grid=(B // cb,),
        in_specs=[row_spec(B_W), row_spec(2 * B_GW),
                  pl.BlockSpec((1, cb, 2 * B_GW, Lb), lambda i: (layer, i, 0, 0))],
        out_specs=[row_spec(B_GW), row_spec(B_GW)],
        out_shape=[out, out],
        compiler_params=_cparams(("arbitrary",)),
        name="attn_sample_d%d" % dil,
    )(q, kvn, cache_t)


def _shift_rows(x, s, fill):
    n = x.shape[0]
    rolled = pltpu.roll(x, s, 0)
    r = lax.broadcasted_iota(jnp.int32, x.shape, 0)
    return jnp.where(r < s, fill, rolled)


def _conv_init(xx_ref, prev, hp):
    halo = prev.shape[0]
    if halo < hp:
        xx_ref[0:hp, :] = jnp.zeros((hp, xx_ref.shape[1]), F32)
    xx_ref[hp - halo:hp, :] = prev


def _conv_taps(xx_ref, x, w_ref, b_ref, n_taps, unit, hp, c0, c1):
    tm = x.shape[0]
    halo = (n_taps - 1) * unit
    y = b_ref[:, c0:c1] + x * w_ref[n_taps - 1:n_taps, c0:c1]
    if unit % 8 == 0:
        xx_ref[hp:hp + tm, c0:c1] = x
        for j in range(n_taps - 1):
            y = y + xx_ref[hp - halo + j * unit:hp - halo + j * unit + tm, c0:c1] * w_ref[j:j + 1, c0:c1]
        tail = xx_ref[hp + tm - halo:hp + tm, c0:c1]
        xx_ref[hp - halo:hp, c0:c1] = tail
    else:
        xcat = jnp.concatenate([xx_ref[0:hp, c0:c1], x], axis=0)
        for j in range(n_taps - 1):
            y = y + pltpu.roll(xcat, halo - j * unit, 0)[hp:hp + tm, :] * w_ref[j:j + 1, c0:c1]
        tail = x[tm - halo:tm, :]
        xx_ref[0:hp, c0:c1] = x[tm - hp:tm, :]
    return y, tail


def _rglru_kernel(x_ref, g_ref, prev_ref, h0_ref, cw_ref, cb_ref, wr_ref, br_ref, wi_ref, bi_ref, lam_ref,
                  o_ref, hl_ref, xx_ref, hc_ref, *, unit, hp):
    i = pl.program_id(1)
    tm = x_ref.shape[1]

    @pl.when(i == 0)
    def _():
        _conv_init(xx_ref, prev_ref[0], hp)
        hc_ref[...] = h0_ref[0]

    y, _ = _conv_taps(xx_ref, x_ref[0].astype(F32), cw_ref, cb_ref, C_CONV, unit, hp, 0, C_WIDTH)

    r = _sigmoid(_dot(y, wr_ref[...]) + br_ref[...])
    ig = _sigmoid(_dot(y, wi_ref[...]) + bi_ref[...])
    log_a = -C_POW * r * _softplus(-lam_ref[...])
    a = jnp.exp(log_a)
    th = jnp.tanh(log_a)
    bx = jnp.sqrt(-2.0 * th / (1.0 - th)) * (ig * y)
    hc = hc_ref[...]
    if unit == 1:
        rr = lax.broadcasted_iota(jnp.int32, bx.shape, 0)
        bx = jnp.where(rr < 1, a * hc + bx, bx)
    elif unit == tm:
        bx = a * hc + bx
    else:
        bx = jnp.concatenate([a[0:unit] * hc + bx[0:unit], bx[unit:]], axis=0)
    s = unit
    while s < tm:
        if s % 8 == 0:
            bx = jnp.concatenate([bx[0:s], a[s:] * bx[0:tm - s] + bx[s:]], axis=0)
            a = jnp.concatenate([a[0:s], a[s:] * a[0:tm - s]], axis=0)
        else:
            a_sh = _shift_rows(a, s, 1.0)
            b_sh = _shift_rows(bx, s, 0.0)
            bx = a * b_sh + bx
            a = a * a_sh
        s *= 2
    h = bx
    hl = h[tm - unit:tm, :]
    hc_ref[...] = hl
    hl_ref[0] = hl
    o_ref[0] = h * jax.nn.gelu(g_ref[0].astype(F32))


def _rglru_call(proj3, prev, h0, cw, cb, wr, br, wi, bi, lam, unit, tm):
    nseq, rows, _ = proj3.shape
    halo = (C_CONV - 1) * unit
    hp = -(-halo // 8) * 8
    assert unit == 1 or unit % 8 == 0
    vec = lambda: pl.BlockSpec((1, C_WIDTH), lambda n, i: (0, 0))
    return pl.pallas_call(
        functools.partial(_rglru_kernel, unit=unit, hp=hp),
        grid=(nseq, rows // tm),
        in_specs=[pl.BlockSpec((1, tm, C_WIDTH), lambda n, i: (n, i, OFF_XC // C_WIDTH)),
                  pl.BlockSpec((1, tm, C_WIDTH), lambda n, i: (n, i, OFF_GC // C_WIDTH)),
                  pl.BlockSpec((1, halo, C_WIDTH), lambda n, i: (n, 0, 0)),
                  pl.BlockSpec((1, unit, C_WIDTH), lambda n, i: (n, 0, 0)),
                  pl.BlockSpec((C_CONV, C_WIDTH), lambda n, i: (0, 0)), vec(),
                  pl.BlockSpec((C_WIDTH, C_WIDTH), lambda n, i: (0, 0)), vec(),
                  pl.BlockSpec((C_WIDTH, C_WIDTH), lambda n, i: (0, 0)), vec(), vec()],
        out_specs=[pl.BlockSpec((1, tm, C_WIDTH), lambda n, i: (n, i, 0)),
                   pl.BlockSpec((1, unit, C_WIDTH), lambda n, i: (n, 0, 0))],
        out_shape=[jax.ShapeDtypeStruct((nseq, rows, C_WIDTH), F32),
                   jax.ShapeDtypeStruct((nseq, unit, C_WIDTH), F32)],
        scratch_shapes=[pltpu.VMEM((hp + tm, C_WIDTH), F32), pltpu.VMEM((unit, C_WIDTH), F32)],
        compiler_params=_cparams(("parallel", "arbitrary")),
        name="rglru_u%d" % unit,
    )(proj3, proj3, prev, h0, cw, cb, wr, br, wi, bi, lam)


def _merge_kernel(x_ref, ga_ref, gb_ref, gc_ref, oa_ref, oc_ref,
                  o0_ref, o1_ref, o2_ref, l0_ref, l1_ref, l2_ref,
                  wpa_ref, wpb_ref, wpc_ref, wo_ref, g_ref, b_ref, out_ref, il_ref, *, dils):
    tm = x_ref.shape[1]

    def rows_in_order(ref, dil):
        if dil == 1:
            return ref[0, 0]
        for r in range(dil):
            for j in range(il_ref.shape[0]):
                il_ref[j, pl.ds(r, tm // dil, stride=dil), :] = ref[0, r, :, j * 128:(j + 1) * 128]
        return jnp.concatenate([il_ref[j] for j in range(il_ref.shape[0])], axis=1)

    o0, o1, o2 = (rows_in_order(r, d) for r, d in zip((o0_ref, o1_ref, o2_ref), dils))
    l0, l1, l2 = (rows_in_order(r, d) for r, d in zip((l0_ref, l1_ref, l2_ref), dils))
    m = jnp.maximum(jnp.maximum(l0, l1), l2)
    e0, e1, e2 = jnp.exp(l0 - m), jnp.exp(l1 - m), jnp.exp(l2 - m)
    den = e0 + e1 + e2
    ob = (e0 / den) * o0 + (e1 / den) * o1 + (e2 / den) * o2
    merged = (_sigmoid(ga_ref[0].astype(F32)) * _dot(oa_ref[0], wpa_ref[...])
              + _sigmoid(gb_ref[0].astype(F32)) * _dot(ob, wpb_ref[...])
              + _sigmoid(gc_ref[0].astype(F32)) * _dot(oc_ref[0], wpc_ref[...]))
    mix = _dot(merged, wo_ref[...])
    out_ref[0] = _layer_norm(DN_ALPHA * x_ref[0] + mix, g_ref[...], b_ref[...])


def _merge_call(x3, proj3, oa, oc, obs, lses, wpa, wpb, wpc, wo, g, b, tm):
    nseq, rows, _ = x3.shape
    dils = tuple(o.shape[1] for o in obs)
    row_spec = lambda w, j=0: pl.BlockSpec((1, tm, w), lambda n, i: (n, i, j))
    grp_spec = lambda d: pl.BlockSpec((1, d, tm // d, B_GW), lambda n, i: (n, 0, i, 0))
    full = lambda a: pl.BlockSpec(a.shape, lambda n, i: (0, 0))
    return pl.pallas_call(
        functools.partial(_merge_kernel, dils=dils),
        grid=(nseq, rows // tm),
        in_specs=[row_spec(D_MODEL), row_spec(D_MODEL, 0), row_spec(D_MODEL, 1), row_spec(D_MODEL, 2),
                  row_spec(A_VW), row_spec(C_WIDTH)]
                 + [grp_spec(d) for d in dils] * 2
                 + [full(wpa), full(wpb), full(wpc), full(wo), full(g), full(b)],
        out_specs=row_spec(D_MODEL),
        out_shape=jax.ShapeDtypeStruct((nseq, rows, D_MODEL), F32),
        scratch_shapes=[pltpu.VMEM((B_GW // 128, tm, 128), F32)],
        compiler_params=_cparams(("parallel", "parallel")),
        name="merge_ln",
    )(x3, proj3, proj3, proj3, oa, oc, *obs, *lses, wpa, wpb, wpc, wo, g, b)


def _ffn_kernel(x_ref, prev_ref, wu_ref, cw_ref, cb_ref, wd_ref, g_ref, b_ref,
                o_ref, st_ref, xx_ref, *, unit, hp):
    i = pl.program_id(1)
    tm = x_ref.shape[1]

    @pl.when(i == 0)
    def _():
        _conv_init(xx_ref, prev_ref[0], hp)

    x = x_ref[0]
    xb = x.astype(BF16)
    f = jnp.zeros((tm, D_MODEL), F32)
    for c0 in range(0, D_FF, FF_CHUNK):
        c1 = c0 + FF_CHUNK
        gate = jnp.dot(xb, wu_ref[:, c0:c1], preferred_element_type=F32)
        up = jnp.dot(xb, wu_ref[:, D_FF + c0:D_FF + c1], preferred_element_type=F32)
        y, tail = _conv_taps(xx_ref, gate, cw_ref, cb_ref, F_CONV, unit, hp, c0, c1)
        st_ref[0, :, c0:c1] = tail
        f = f + _dot(_silu(y) * up, wd_ref[c0:c1, :])
    o_ref[0] = _layer_norm(DN_ALPHA * x + f, g_ref[...], b_ref[...])


def _ffn_call(x3, prev, wu, cw, cb, wd, g, b, unit, tm):
    nseq, rows, _ = x3.shape
    halo = (F_CONV - 1) * unit
    hp = -(-halo // 8) * 8
    const = lambda a: pl.BlockSpec(a.shape, lambda n, i: (0, 0), pipeline_mode=pl.Buffered(1))
    return pl.pallas_call(
        functools.partial(_ffn_kernel, unit=unit, hp=hp),
        grid=(nseq, rows // tm),
        in_specs=[pl.BlockSpec((1, tm, D_MODEL), lambda n, i: (n, i, 0)),
                  pl.BlockSpec((1, halo, D_FF), lambda n, i: (n, 0, 0)),
                  const(wu), const(cw), const(cb), const(wd), const(g), const(b)],
        out_specs=[pl.BlockSpec((1, tm, D_MODEL), lambda n, i: (n, i, 0)),
                   pl.BlockSpec((1, halo, D_FF), lambda n, i: (n, 0, 0))],
        out_shape=[jax.ShapeDtypeStruct((nseq, rows, D_MODEL), F32),
                   jax.ShapeDtypeStruct((nseq, halo, D_FF), F32)],
        scratch_shapes=[pltpu.VMEM((hp + tm, D_FF), F32)],
        compiler_params=_cparams(("parallel", "arbitrary")),
        name="conv_ffn_u%d" % unit,
    )(x3, prev, wu, cw, cb, wd, g, b)


def _block_diag(w):
    nblk, bw, _ = w.shape
    eye = jnp.eye(nblk, dtype=w.dtype)
    return (eye[:, None, :, None] * w[:, :, None, :]).reshape(nblk * bw, nblk * bw)


def _layer_params(l, w_in, a_conv_w, a_A_log, a_dt_bias, a_norm_w, c_conv_w, c_conv_b, c_w_r, c_b_r,
                  c_w_i, c_b_i, c_lam, w_pa, w_pb, w_pc, w_o, ln1_g, ln1_b, f_up, f_conv_w, f_conv_b,
                  f_down, ln2_g, ln2_b):
    w = w_in[l]
    o_b = A_QKV
    o_a = o_b + A_HEADS
    o_z = o_a + A_HEADS
    o_qb = o_z + A_VW
    o_xc = o_qb + 3 * B_W
    o_gc = o_xc + C_WIDTH
    o_gt = o_gc + C_WIDTH
    pad = jnp.zeros((D_MODEL, N_PROJ - OFF_BA - 2 * A_HEADS), w.dtype)
    wp = jnp.concatenate([w[:, o_gt:o_gt + 3 * D_MODEL], w[:, o_xc:o_gc], w[:, o_gc:o_gt], w[:, o_z:o_qb],
                          w[:, 0:o_b], w[:, o_qb:o_xc], w[:, o_b:o_z], pad], axis=1).astype(BF16)
    gp = jnp.zeros((2, 128), F32)
    gp = gp.at[0, A_HEADS:2 * A_HEADS].set(a_A_log[l]).at[1, A_HEADS:2 * A_HEADS].set(a_dt_bias[l])
    row = lambda v: v.reshape(1, -1)
    return dict(
        wp=wp, a_cw=a_conv_w[l], gp=gp, a_nw=row(a_norm_w[l]),
        c_cw=c_conv_w[l], c_cb=row(c_conv_b[l]),
        c_wr=_block_diag(c_w_r[l]).astype(BF16), c_br=row(c_b_r[l]),
        c_wi=_block_diag(c_w_i[l]).astype(BF16), c_bi=row(c_b_i[l]), c_lam=row(c_lam[l]),
        wpa=w_pa[l].astype(BF16), wpb=w_pb[l].astype(BF16), wpc=w_pc[l].astype(BF16), wo=w_o[l].astype(BF16),
        ln1_g=row(ln1_g[l]), ln1_b=row(ln1_b[l]),
        f_up=f_up[l].astype(BF16), f_cw=f_conv_w[l], f_cb=row(f_conv_b[l]), f_down=f_down[l].astype(BF16),
        ln2_g=row(ln2_g[l]), ln2_b=row(ln2_b[l]))


def _layer_tail(x3, proj3, o_a, obs, lses, unit, p, c_prev, c_h0, f_prev, tm_merge, tm_seq, tm_ffn):
    o_c, c_hl = _rglru_call(proj3, c_prev, c_h0, p["c_cw"], p["c_cb"], p["c_wr"], p["c_br"], p["c_wi"],
                            p["c_bi"], p["c_lam"], unit, tm_seq)
    x1 = _merge_call(x3, proj3, o_a, o_c, obs, lses, p["wpa"], p["wpb"], p["wpc"], p["wo"],
                     p["ln1_g"], p["ln1_b"], tm_merge)
    x_out, f_state = _ffn_call(x1, f_prev, p["f_up"], p["f_cw"], p["f_cb"], p["f_down"],
                               p["ln2_g"], p["ln2_b"], unit, tm_ffn)
    return x_out, c_hl, f_state


def kernel(x_prompt, x_sample, state_a_conv, state_a_rec, cache_b_w128, cache_b_w512, cache_b_w2048,
           state_c_conv, state_c_h, state_f_conv, ln_in_g, ln_in_b, w_in, a_conv_w, a_A_log, a_dt_bias,
           a_norm_w, c_conv_w, c_conv_b, c_w_r, c_b_r, c_w_i, c_b_i, c_lam, w_pa, w_pb, w_pc, w_o,
           ln1_g, ln1_b, f_up, f_conv_w, f_conv_b, f_down, ln2_g, ln2_b):
    Bp, Sp, _ = x_prompt.shape
    Bs, Ts, _ = x_sample.shape
    caches = (cache_b_w128, cache_b_w512, cache_b_w2048)
    for (win, dil), cache in zip(B_GROUPS, caches):
        assert win // dil == B_NK and cache.shape[2] == win and Sp % (dil * B_NK) == 0

    caches_t = [jnp.transpose(c, (0, 1, 3, 4, 5, 2)).reshape(DEPTH, Bs, 2 * B_GW, c.shape[2]) for c in caches]
    Rs = Ts * Bs

    lng, lnb = ln_in_g.reshape(1, -1), ln_in_b.reshape(1, -1)
    hp = _ln_call(x_prompt.reshape(Bp * Sp, D_MODEL), lng, lnb, 512)
    xs_tm = jnp.swapaxes(x_sample, 0, 1).reshape(Rs, D_MODEL)
    hs = _ln_call(xs_tm, lng, lnb, Rs)

    tabs_p = _rope_tables(jnp.arange(Sp))
    tabs_s = _rope_tables(jnp.repeat(PAST_LEN + jnp.arange(Ts), Bs))

    zeros_c = jnp.zeros((Bp, C_CONV - 1, C_WIDTH), F32)
    zeros_h = jnp.zeros((Bp, 1, C_WIDTH), F32)
    zeros_f = jnp.zeros((Bp, F_CONV - 1, D_FF), F32)

    outs_p, outs_s = [], []
    for l in range(DEPTH):
        p = _layer_params(l, w_in, a_conv_w, a_A_log, a_dt_bias, a_norm_w, c_conv_w, c_conv_b, c_w_r, c_b_r,
                          c_w_i, c_b_i, c_lam, w_pa, w_pb, w_pc, w_o, ln1_g, ln1_b, f_up, f_conv_w,
                          f_conv_b, f_down, ln2_g, ln2_b)

        proj3 = _proj_call(hp, p["wp"], 1024, BF16).reshape(Bp, Sp, N_PROJ)
        hp_tail = hp.reshape(Bp, Sp, D_MODEL)[:, Sp - TAIL_ROWS:].reshape(Bp * TAIL_ROWS, D_MODEL)
        proj_tail = _proj_call(hp_tail, p["wp"], Bp * TAIL_ROWS, F32).reshape(Bp, TAIL_ROWS, N_PROJ)
        o_a, a_rec = _gdn_prompt_call(proj3, p["a_cw"], p["gp"], p["a_nw"])
        rp = _rope_prompt_call(proj3, tabs_p, 256)
        res = [_attn_prompt_call(rp[g], rp[3 + g]) for g in range(len(B_GROUPS))]
        hp3, c_hl, f_st = _layer_tail(hp.reshape(Bp, Sp, D_MODEL), proj3, o_a, [r[0] for r in res],
                                      [r[1] for r in res], 1, p, zeros_c, zeros_h, zeros_f, 512, 256, 512)
        hp = hp3.reshape(Bp * Sp, D_MODEL)
        kv_rows_p = [t.reshape(Bp, 2, B_HEADS, B_HD, t.shape[-1]).transpose(0, 4, 1, 2, 3) for t in rp[6:9]]
        outs_p.append((
            proj_tail[:, TAIL_ROWS - (A_CONV - 1):, OFF_QKVA:OFF_QKVA + A_QKV],
            a_rec, kv_rows_p[0], kv_rows_p[1], kv_rows_p[2],
            proj_tail[:, TAIL_ROWS - (C_CONV - 1):, OFF_XC:OFF_XC + C_WIDTH],
            c_hl[:, 0],
            f_st))

        proj3 = _proj_call(hs, p["wp"], Rs, F32).reshape(1, Rs, N_PROJ)
        pt = proj3.reshape(Ts, Bs, N_PROJ)
        a_st = jnp.swapaxes(state_a_conv[l], 0, 1)
        o_a, a_rec = _gdn_sample_call(pt, a_st, state_a_rec, l, p["a_cw"], p["gp"], p["a_nw"])
        rs = _rope_sample_call(proj3, tabs_s, Bs)
        q3 = rs[0].reshape(Ts, Bs, B_W)
        res = [_attn_sample_call(q3, rs[1 + g].reshape(Ts, Bs, 2 * B_GW), caches_t[g], l, g, dil)
               for g, (_, dil) in enumerate(B_GROUPS)]
        c_prev = jnp.swapaxes(state_c_conv[l], 0, 1).reshape(1, (C_CONV - 1) * Bs, C_WIDTH)
        f_prev = jnp.swapaxes(state_f_conv[l], 0, 1).reshape(1, (F_CONV - 1) * Bs, D_FF)
        hs3, c_hl, f_st = _layer_tail(hs.reshape(1, Rs, D_MODEL), proj3, o_a.reshape(1, Rs, A_VW),
                                      [r[0].reshape(1, 1, Rs, B_GW) for r in res],
                                      [r[1].reshape(1, 1, Rs, B_GW) for r in res],
                                      Bs, p, c_prev, state_c_h[l][None], f_prev, min(256, Rs), Rs, Rs)
        hs = hs3.reshape(Rs, D_MODEL)
        tm2bm = lambda a: jnp.swapaxes(a, 0, 1)
        kv_rows_s = [t.reshape(Ts, 2, B_HEADS, B_HD, Bs).transpose(4, 0, 1, 2, 3) for t in rs[4:7]]
        outs_s.append((
            tm2bm(pt[Ts - (A_CONV - 1):, :, OFF_QKVA:OFF_QKVA + A_QKV]),
            a_rec, kv_rows_s[0], kv_rows_s[1], kv_rows_s[2],
            tm2bm(pt[Ts - (C_CONV - 1):, :, OFF_XC:OFF_XC + C_WIDTH]),
            c_hl[0],
            tm2bm(f_st.reshape(F_CONV - 1, Bs, D_FF))))

    stack = lambda outs: [jnp.stack(s, 0) for s in zip(*outs)]
    y_p = hp.reshape(Bp, Sp, D_MODEL)
    y_s = jnp.swapaxes(hs.reshape(Ts, Bs, D_MODEL), 0, 1)
    return (y_p, y_s, *stack(outs_p), *stack(outs_s))
```

```python
import functools

import jax
import jax.numpy as jnp
from jax import lax
from jax.experimental import pallas as pl
from jax.experimental.pallas import tpu as pltpu

F32 = jnp.float32
BF16 = jnp.bfloat16

D_MODEL = 1024
DEPTH = 4
PAST_LEN = 2048
A_HEADS, A_DK, A_DV, A_CONV, A_CHUNK = 4, 128, 128, 4, 64
B_GROUPS = ((128, 1), (512, 4), (2048, 16))
B_HEADS, B_HD = 4, 64
B_ROT = B_HD // 4
ROPE_THETA = 500000.0
B_NK = 128
C_WIDTH, C_CONV, C_POW = 512, 4, 8.0
D_FF = ((8 * D_MODEL // 3 + 255) // 256) * 256
F_CONV = 3
FF_CHUNK = D_FF
DN_ALPHA = (2 * DEPTH) ** 0.25
LN_EPS = 1e-5
NORM_EPS = 1e-6

A_QK = A_HEADS * A_DK
A_VW = A_HEADS * A_DV
A_QKV = 2 * A_QK + A_VW
B_GW = B_HEADS * B_HD
B_W = len(B_GROUPS) * B_GW

OFF_GATES = 0
OFF_XC = 3072
OFF_GC = 3584
OFF_Z = 4096
OFF_QKVA = 4608
OFF_QB = 6144
OFF_BA = 8448
N_PROJ = 8704
PROJ_TN = 2176
TAIL_ROWS = 8

VMEM_LIMIT = 56 * 1024 * 1024


def _cparams(sem, **kw):
    return pltpu.CompilerParams(dimension_semantics=sem, vmem_limit_bytes=VMEM_LIMIT, **kw)


def _sigmoid(x):
    return 1.0 / (1.0 + jnp.exp(-x))


def _silu(x):
    return x * _sigmoid(x)


def _softplus(x):
    return jnp.maximum(x, 0.0) + jnp.log1p(jnp.exp(-jnp.abs(x)))


def _layer_norm(x, g, b):
    mu = jnp.mean(x, axis=-1, keepdims=True)
    xc = x - mu
    var = jnp.mean(xc * xc, axis=-1, keepdims=True)
    return xc * lax.rsqrt(var + LN_EPS) * g + b


def _dot(a, b):
    return jnp.dot(a.astype(BF16), b.astype(BF16), preferred_element_type=F32)


def _dot_nt(a, b):
    return lax.dot_general(a.astype(BF16), b.astype(BF16), (((1,), (1,)), ((), ())),
                           preferred_element_type=F32)


def _dot_tn(a, b):
    return lax.dot_general(a.astype(BF16), b.astype(BF16), (((0,), (0,)), ((), ())),
                           preferred_element_type=F32)


def _split2(a):
    hi = a.astype(BF16)
    lo = (a - hi.astype(F32)).astype(BF16)
    return hi, lo


def _dot3(a, b):
    ah, al = _split2(a)
    bh, bl = _split2(b)
    d = lambda x, y: jnp.dot(x, y, preferred_element_type=F32)
    return d(ah, bh) + (d(ah, bl) + d(al, bh))


def _ln_kernel(x_ref, g_ref, b_ref, o_ref):
    o_ref[...] = _layer_norm(x_ref[...], g_ref[...], b_ref[...])


def _ln_call(x, g, b, tm):
    rows = x.shape[0]
    return pl.pallas_call(
        _ln_kernel,
        grid=(rows // tm,),
        in_specs=[pl.BlockSpec((tm, D_MODEL), lambda i: (i, 0)),
                  pl.BlockSpec((1, D_MODEL), lambda i: (0, 0)),
                  pl.BlockSpec((1, D_MODEL), lambda i: (0, 0))],
        out_specs=pl.BlockSpec((tm, D_MODEL), lambda i: (i, 0)),
        out_shape=jax.ShapeDtypeStruct((rows, D_MODEL), F32),
        compiler_params=_cparams(("parallel",)),
        name="entry_ln",
    )(x, g, b)


def _proj_kernel(x_ref, w_ref, o_ref, xb_ref):
    @pl.when(pl.program_id(1) == 0)
    def _():
        xb_ref[...] = x_ref[...].astype(BF16)

    o_ref[...] = jnp.dot(xb_ref[...], w_ref[...], preferred_element_type=F32).astype(o_ref.dtype)


def _proj_call(x, w, tm, out_dtype):
    rows = x.shape[0]
    return pl.pallas_call(
        _proj_kernel,
        grid=(rows // tm, N_PROJ // PROJ_TN),
        in_specs=[pl.BlockSpec((tm, D_MODEL), lambda i, j: (i, 0)),
                  pl.BlockSpec((D_MODEL, PROJ_TN), lambda i, j: (0, j))],
        out_specs=pl.BlockSpec((tm, PROJ_TN), lambda i, j: (i, j)),
        out_shape=jax.ShapeDtypeStruct((rows, N_PROJ), out_dtype),
        scratch_shapes=[pltpu.VMEM((tm, D_MODEL), BF16)],
        compiler_params=_cparams(("parallel", "arbitrary")),
        name="in_proj",
    )(x, w)


def _gdn_prompt_kernel(x_ref, z_ref, ba_ref, cw_ref, gp_ref, nw_ref,
                       o_ref, sfin_ref, xx_ref, s_ref):
    c = pl.program_id(0)
    C = A_CHUNK
    nb = x_ref.shape[0]

    @pl.when(c == 0)
    def _():
        xx_ref[...] = jnp.zeros((nb, 8, A_QKV), F32)
        s_ref[...] = jnp.zeros(s_ref.shape, F32)

    ri = lax.broadcasted_iota(jnp.int32, (C, C), 0)
    ci = lax.broadcasted_iota(jnp.int32, (C, C), 1)
    tril = ri >= ci
    strict = ri > ci
    eye = (ri == ci).astype(F32)
    trilf = tril.astype(F32)
    nw = nw_ref[...]

    ch = []
    for b in range(nb):
        x = x_ref[b].astype(F32)
        xcat = jnp.concatenate([xx_ref[b], x], axis=0)
        y = pltpu.roll(xcat, A_CONV - 1, 0)[8:8 + C, :] * cw_ref[0:1, :]
        for j in range(1, A_CONV - 1):
            y = y + pltpu.roll(xcat, A_CONV - 1 - j, 0)[8:8 + C, :] * cw_ref[j:j + 1, :]
        y = y + x * cw_ref[A_CONV - 1:A_CONV, :]
        xx_ref[b] = x[C - 8:C, :]
        y = _silu(y)
        ba = ba_ref[b].astype(F32)
        beta_all = _sigmoid(ba)
        g_all = -jnp.exp(gp_ref[0:1, :]) * _softplus(ba + gp_ref[1:2, :])
        gc = _dot3(trilf, g_all)
        gct = jnp.concatenate([gc, jnp.zeros((128 - C, 128), F32)], axis=0).T
        for h in range(A_HEADS):
            q = y[:, h * A_DK:(h + 1) * A_DK]
            k = y[:, A_QK + h * A_DK:A_QK + (h + 1) * A_DK]
            v = y[:, 2 * A_QK + h * A_DV:2 * A_QK + (h + 1) * A_DV]
            q = q * lax.rsqrt(jnp.sum(q * q, axis=-1, keepdims=True) + NORM_EPS) * (A_DK ** -0.5)
            k = k * lax.rsqrt(jnp.sum(k * k, axis=-1, keepdims=True) + NORM_EPS)
            beta = beta_all[:, h:h + 1]
            gcol = gc[:, 4 + h:5 + h]
            grow = gct[4 + h:5 + h, 0:C]
            decay = jnp.where(tril, jnp.exp(jnp.where(tril, gcol - grow, 0.0)), 0.0)
            eg = jnp.exp(gcol)
            glast = gcol[C - 1:C, :]
            ch.append(dict(b=b, h=h, q=q, k=k, kb=k * beta, vb=v * beta, decay=decay, eg=eg,
                           k_dec=k * jnp.exp(glast - gcol), q_dec=q * eg, eglast=jnp.exp(glast)))
    for d in ch:
        d["m"] = -jnp.where(strict, _dot_nt(d["kb"], d["k"]) * d["decay"], 0.0)
        d["t"] = eye + d["m"]
    for d in ch:
        d["qk"] = jnp.where(tril, _dot_nt(d["q"], d["k"]) * d["decay"], 0.0)
    for _ in range(5):
        for d in ch:
            d["m"] = _dot(d["m"], d["m"])
        for d in ch:
            d["t"] = d["t"] + _dot(d["t"], d["m"])
    for d in ch:
        d["u"] = _dot(d["t"], d["vb"])
        d["w"] = _dot(d["t"], d["kb"] * d["eg"])
    for d in ch:
        d["s"] = s_ref[d["b"], d["h"]]
        d["v_new"] = d["u"] - _dot(d["w"], d["s"])
    for d in ch:
        d["o"] = _dot(d["q_dec"], d["s"]) + _dot(d["qk"], d["v_new"])
        s_ref[d["b"], d["h"]] = d["s"] * d["eglast"] + _dot_tn(d["k_dec"], d["v_new"])
    for d in ch:
        b, h, o = d["b"], d["h"], d["o"]
        o = o * lax.rsqrt(jnp.mean(o * o, axis=-1, keepdims=True) + NORM_EPS) * nw
        o_ref[b, :, h * A_DV:(h + 1) * A_DV] = o * _silu(z_ref[b, :, h * A_DV:(h + 1) * A_DV].astype(F32))

    @pl.when(c == pl.num_programs(0) - 1)
    def _():
        sfin_ref[...] = s_ref[...]


def _gdn_prompt_call(proj3, cw, gp, nw):
    B, S, _ = proj3.shape
    C = A_CHUNK
    return pl.pallas_call(
        _gdn_prompt_kernel,
        grid=(S // C,),
        in_specs=[pl.BlockSpec((B, C, A_QKV), lambda c: (0, c, OFF_QKVA // A_QKV)),
                  pl.BlockSpec((B, C, A_VW), lambda c: (0, c, OFF_Z // A_VW)),
                  pl.BlockSpec((B, C, 128), lambda c: (0, c, OFF_BA // 128)),
                  pl.BlockSpec((A_CONV, A_QKV), lambda c: (0, 0)),
                  pl.BlockSpec((2, 128), lambda c: (0, 0)),
                  pl.BlockSpec((1, A_DV), lambda c: (0, 0))],
        out_specs=[pl.BlockSpec((B, C, A_VW), lambda c: (0, c, 0)),
                   pl.BlockSpec((B, A_HEADS, A_DK, A_DV), lambda c: (0, 0, 0, 0))],
        out_shape=[jax.ShapeDtypeStruct((B, S, A_VW), F32),
                   jax.ShapeDtypeStruct((B, A_HEADS, A_DK, A_DV), F32)],
        scratch_shapes=[pltpu.VMEM((B, 8, A_QKV), F32),
                        pltpu.VMEM((B, A_HEADS, A_DK, A_DV), F32)],
        compiler_params=_cparams(("arbitrary",)),
        name="gdn_prompt",
    )(proj3, proj3, proj3, cw, gp, nw)


GS_BB = 16
GS_C = 16


def _gdn_sample_kernel(x_ref, z_ref, ba_ref, st_ref, s0_ref, cw_ref, gp_ref, nw_ref,
                       o_ref, snew_ref, oscr_ref):
    T = x_ref.shape[0]
    bb = GS_BB
    xx = [st_ref[j] for j in range(A_CONV - 1)] + [x_ref[t].astype(F32) for t in range(T)]
    C = GS_C
    assert T <= C
    qs, ks, kbs, vbs, gcs = [], [], [], [], []
    for t in range(T):
        y = xx[t] * cw_ref[0:1, :]
        for j in range(1, A_CONV):
            y = y + xx[t + j] * cw_ref[j:j + 1, :]
        y = _silu(y)
        ba = ba_ref[t].astype(F32)
        beta_all = _sigmoid(ba)
        g_all = -jnp.exp(gp_ref[0:1, :]) * _softplus(ba + gp_ref[1:2, :])
        gcs.append(g_all if t == 0 else gcs[-1] + g_all)
        qs.append([]), ks.append([]), kbs.append([]), vbs.append([])
        for h in range(A_HEADS):
            q = y[:, h * A_DK:(h + 1) * A_DK]
            k = y[:, A_QK + h * A_DK:A_QK + (h + 1) * A_DK]
            v = y[:, 2 * A_QK + h * A_DV:2 * A_QK + (h + 1) * A_DV]
            q = q * lax.rsqrt(jnp.sum(q * q, axis=-1, keepdims=True) + NORM_EPS) * (A_DK ** -0.5)
            k = k * lax.rsqrt(jnp.sum(k * k, axis=-1, keepdims=True) + NORM_EPS)
            beta = beta_all[:, h:h + 1]
            qs[t].append(q), ks[t].append(k), kbs[t].append(k * beta), vbs[t].append(v * beta)

    ri = lax.broadcasted_iota(jnp.int32, (C, C), 0)
    ci = lax.broadcasted_iota(jnp.int32, (C, C), 1)
    tril = ri >= ci
    strict = ri > ci
    eye = (ri == ci).astype(F32)
    zrows = jnp.zeros((C - T, A_DK), F32)
    rows_of = lambda slabs, b, h: jnp.concatenate([slabs[t][h][b:b + 1, :] for t in range(T)] + [zrows], axis=0)
    ch = []
    for h in range(A_HEADS):
        for b in range(bb):
            gcol = jnp.concatenate([gcs[t][b:b + 1, 4 + h:5 + h] for t in range(T)]
                                   + [gcs[T - 1][b:b + 1, 4 + h:5 + h]] * (C - T), axis=0)
            grow = jnp.sum(eye * gcol, axis=0, keepdims=True)
            decay = jnp.where(tril, jnp.exp(jnp.where(tril, gcol - grow, 0.0)), 0.0)
            eg = jnp.exp(gcol)
            glast = gcol[C - 1:C, :]
            q, k, kb, vb = (rows_of(s, b, h) for s in (qs, ks, kbs, vbs))
            ch.append(dict(b=b, h=h, q=q, k=k, kb=kb, vb=vb, decay=decay, eg=eg,
                           k_dec=k * jnp.exp(glast - gcol), q_dec=q * eg, eglast=jnp.exp(glast)))
    for d in ch:
        d["m"] = -jnp.where(strict, _dot_nt(d["kb"], d["k"]) * d["decay"], 0.0)
        d["qk"] = jnp.where(tril, _dot_nt(d["q"], d["k"]) * d["decay"], 0.0)
    for d in ch:
        d["t"] = eye + d["m"]
    for _ in range(GS_C.bit_length() - 2):
        for d in ch:
            d["m"] = _dot(d["m"], d["m"])
        for d in ch:
            d["t"] = d["t"] + _dot(d["t"], d["m"])
    for d in ch:
        d["u"] = _dot(d["t"], d["vb"])
        d["w"] = _dot(d["t"], d["kb"] * d["eg"])
    for d in ch:
        d["s"] = s0_ref[0, d["b"], d["h"]]
        ws = _dot(jnp.concatenate([d["w"], d["q_dec"]], axis=0), d["s"])
        d["v_new"] = d["u"] - ws[0:C]
        d["qs"] = ws[C:2 * C]
    for d in ch:
        b, h = d["b"], d["h"]
        o = d["qs"] + _dot(d["qk"], d["v_new"])
        snew_ref[b, h] = d["s"] * d["eglast"] + _dot_tn(d["k_dec"], d["v_new"])
        for t in range(T):
            oscr_ref[t, b:b + 1, h * A_DV:(h + 1) * A_DV] = o[t:t + 1, :]
    nw = nw_ref[...]
    for t in range(T):
        z = z_ref[t].astype(F32)
        for h in range(A_HEADS):
            o = oscr_ref[t, :, h * A_DV:(h + 1) * A_DV]
            o = o * lax.rsqrt(jnp.mean(o * o, axis=-1, keepdims=True) + NORM_EPS) * nw
            o_ref[t, :, h * A_DV:(h + 1) * A_DV] = o * _silu(z[:, h * A_DV:(h + 1) * A_DV])


def _gdn_sample_call(proj3, st, s0, layer, cw, gp, nw):
    T, B, _ = proj3.shape
    bb = GS_BB
    return pl.pallas_call(
        _gdn_sample_kernel,
        grid=(B // bb,),
        in_specs=[pl.BlockSpec((T, bb, A_QKV), lambda i: (0, i, OFF_QKVA // A_QKV)),
                  pl.BlockSpec((T, bb, A_VW), lambda i: (0, i, OFF_Z // A_VW)),
                  pl.BlockSpec((T, bb, 128), lambda i: (0, i, OFF_BA // 128)),
                  pl.BlockSpec((A_CONV - 1, bb, A_QKV), lambda i: (0, i, 0)),
                  pl.BlockSpec((1, bb, A_HEADS, A_DK, A_DV), lambda i: (layer, i, 0, 0, 0)),
                  pl.BlockSpec((A_CONV, A_QKV), lambda i: (0, 0)),
                  pl.BlockSpec((2, 128), lambda i: (0, 0)),
                  pl.BlockSpec((1, A_DV), lambda i: (0, 0))],
        out_specs=[pl.BlockSpec((T, bb, A_VW), lambda i: (0, i, 0)),
                   pl.BlockSpec((bb, A_HEADS, A_DK, A_DV), lambda i: (i, 0, 0, 0))],
        out_shape=[jax.ShapeDtypeStruct((T, B, A_VW), F32),
                   jax.ShapeDtypeStruct(s0.shape[1:], F32)],
        scratch_shapes=[pltpu.VMEM((T, bb, A_VW), F32)],
        compiler_params=_cparams(("parallel",)),
        name="gdn_sample",
    )(proj3, proj3, proj3, st, s0, cw, gp, nw)


def _rope_qkv(q_ref, k_ref, v_ref, c_ref, s1_ref, s2_ref):
    reps = B_W // 128
    c = jnp.concatenate([c_ref[0]] * reps, axis=1)
    s1 = jnp.concatenate([s1_ref[0]] * reps, axis=1)
    s2 = jnp.concatenate([s2_ref[0]] * reps, axis=1)

    def rot(x):
        return x * c + pltpu.roll(x, B_W - B_ROT // 2, 1) * s1 + pltpu.roll(x, B_ROT // 2, 1) * s2

    return rot(q_ref[0].astype(F32)) * (B_HD ** -0.5), rot(k_ref[0].astype(F32)), v_ref[0].astype(F32)


def _rope_in_specs(tm):
    qb = OFF_QB // B_W
    tab_spec = pl.BlockSpec((1, tm, 128), lambda n, i: (0, i, 0))
    return [pl.BlockSpec((1, tm, B_W), lambda n, i: (n, i, qb)),
            pl.BlockSpec((1, tm, B_W), lambda n, i: (n, i, qb + 1)),
            pl.BlockSpec((1, tm, B_W), lambda n, i: (n, i, qb + 2)),
            tab_spec, tab_spec, tab_spec]


def _store_chunks(buf_ref, x):
    for j in range(buf_ref.shape[0]):
        buf_ref[j] = x[:, j * 128:(j + 1) * 128]


def _load_chunks_strided(buf_ref, start, size, stride):
    return jnp.concatenate([buf_ref[j, pl.ds(start, size, stride=stride), :] for j in range(buf_ref.shape[0])],
                           axis=1)


def _rope_prompt_kernel(q_ref, k_ref, v_ref, c_ref, s1_ref, s2_ref,
                        q0_ref, q1_ref, q2_ref, kv0_ref, kv1_ref, kv2_ref, t0_ref, t1_ref, t2_ref,
                        qs_ref, kvs_ref, *, tails):
    i = pl.program_id(1)
    tm = q_ref.shape[1]
    q, k, v = _rope_qkv(q_ref, k_ref, v_ref, c_ref, s1_ref, s2_ref)
    outs = ((q0_ref, kv0_ref, t0_ref), (q1_ref, kv1_ref, t1_ref), (q2_ref, kv2_ref, t2_ref))
    for g, ((_, dil), (qo_ref, kvo_ref, to_ref)) in enumerate(zip(B_GROUPS, outs)):
        lo, hi = g * B_GW, (g + 1) * B_GW
        qg = q[:, lo:hi]
        kv = jnp.concatenate([k[:, lo:hi], v[:, lo:hi]], axis=1)
        if dil == 1:
            qo_ref[0, 0] = qg.astype(BF16)
            kvo_ref[0, 0] = kv
        else:
            _store_chunks(qs_ref, qg)
            _store_chunks(kvs_ref, kv)
            for r in range(dil):
                qo_ref[0, r] = _load_chunks_strided(qs_ref, r, tm // dil, dil).astype(BF16)
                kvo_ref[0, r] = _load_chunks_strided(kvs_ref, r, tm // dil, dil)
        first, w = tails[g]

        @pl.when(i >= first)
        def _(to_ref=to_ref, w=w, kv=kv):
            to_ref[0] = kv[tm - w:tm, :].T


def _rope_prompt_call(proj3, tabs, tm):
    B, S, _ = proj3.shape
    nt = S // tm
    out_shape, out_specs, tails = [], [], []
    for width, dtype in ((B_GW, BF16), (2 * B_GW, F32)):
        for _, dil in B_GROUPS:
            out_shape.append(jax.ShapeDtypeStruct((B, dil, S // dil, width), dtype))
            out_specs.append(pl.BlockSpec((1, dil, tm // dil, width), lambda n, i: (n, 0, i, 0)))
    for win, _ in B_GROUPS:
        win = min(win, S)
        w = min(tm, win)
        first = nt - win // w
        tails.append((first, w))
        out_shape.append(jax.ShapeDtypeStruct((B, 2 * B_GW, win), F32))
        out_specs.append(pl.BlockSpec((1, 2 * B_GW, w),
                                      lambda n, i, first=first: (n, 0, jnp.maximum(i - first, 0))))
    return pl.pallas_call(
        functools.partial(_rope_prompt_kernel, tails=tuple(tails)),
        grid=(B, nt),
        in_specs=_rope_in_specs(tm),
        out_specs=out_specs,
        out_shape=out_shape,
        scratch_shapes=[pltpu.VMEM((B_GW // 128, tm, 128), F32), pltpu.VMEM((2 * B_GW // 128, tm, 128), F32)],
        compiler_params=_cparams(("parallel", "arbitrary")),
        name="rope_prompt",
    )(proj3, proj3, proj3, *tabs)


def _rope_sample_kernel(q_ref, k_ref, v_ref, c_ref, s1_ref, s2_ref,
                        qo_ref, kv0_ref, kv1_ref, kv2_ref, t0_ref, t1_ref, t2_ref, *, unit):
    q, k, v = _rope_qkv(q_ref, k_ref, v_ref, c_ref, s1_ref, s2_ref)
    qo_ref[0] = q
    for g, (kvo_ref, to_ref) in enumerate(((kv0_ref, t0_ref), (kv1_ref, t1_ref), (kv2_ref, t2_ref))):
        kv = jnp.concatenate([k[:, g * B_GW:(g + 1) * B_GW], v[:, g * B_GW:(g + 1) * B_GW]], axis=1)
        kvo_ref[0] = kv
        for t in range(q.shape[0] // unit):
            to_ref[t] = kv[t * unit:(t + 1) * unit, :].T


def _rope_sample_call(proj3, tabs, unit):
    _, rows, _ = proj3.shape
    T = rows // unit
    kv_shape = jax.ShapeDtypeStruct((1, rows, 2 * B_GW), F32)
    kv_spec = pl.BlockSpec((1, rows, 2 * B_GW), lambda n, i: (0, 0, 0))
    t_shape = jax.ShapeDtypeStruct((T, 2 * B_GW, unit), F32)
    t_spec = pl.BlockSpec((T, 2 * B_GW, unit), lambda n, i: (0, 0, 0))
    return pl.pallas_call(
        functools.partial(_rope_sample_kernel, unit=unit),
        grid=(1, 1),
        in_specs=_rope_in_specs(rows),
        out_specs=[pl.BlockSpec((1, rows, B_W), lambda n, i: (0, 0, 0))] + [kv_spec] * 3 + [t_spec] * 3,
        out_shape=[jax.ShapeDtypeStruct((1, rows, B_W), F32)] + [kv_shape] * 3 + [t_shape] * 3,
        compiler_params=_cparams(("arbitrary", "arbitrary")),
        name="rope_sample",
    )(proj3, proj3, proj3, *tabs)


def _rope_tables(pos):
    half = B_ROT // 2
    inv = ROPE_THETA ** (-jnp.arange(half, dtype=F32) / half)
    ang = pos.astype(F32)[:, None] * inv
    cos, sin = jnp.cos(ang), jnp.sin(ang)
    rows = pos.shape[0]
    one = jnp.ones((rows, B_HD - B_ROT), F32)
    zero = jnp.zeros((rows, half), F32)
    zrest = jnp.zeros((rows, B_HD - B_ROT), F32)
    c = jnp.concatenate([cos, cos, one], axis=1)
    s1 = jnp.concatenate([-sin, zero, zrest], axis=1)
    s2 = jnp.concatenate([zero, sin, zrest], axis=1)
    tile = lambda t: jnp.concatenate([t, t], axis=1)[None]
    return tile(c), tile(s1), tile(s2)


def _head_lane_mask(rows):
    lane = lax.broadcasted_iota(jnp.int32, (rows, B_GW), 1)
    return [(lane >= h * B_HD) & (lane < (h + 1) * B_HD) for h in range(B_HEADS)]


def _attn_prompt_kernel(q_ref, kvp_ref, kvc_ref, o_ref, l_ref):
    i = pl.program_id(2)
    nk = B_NK
    nr, nblk = q_ref.shape[1], q_ref.shape[2] // nk
    qi = lax.broadcasted_iota(jnp.int32, (nk, 2 * nk), 0)
    kj = lax.broadcasted_iota(jnp.int32, (nk, 2 * nk), 1)
    band = (kj >= qi) & (kj <= qi + nk)
    band_first = band & ((kj >= nk) | (i > 0))
    masks = _head_lane_mask(nk)
    zq = jnp.zeros((nk, B_GW), BF16)
    q, kk, vv = [], [], []
    for r in range(nr):
        q.append(q_ref[0, r])
        kk.append(jnp.concatenate([kvp_ref[0, r, :, 0:B_GW], kvc_ref[0, r, :, 0:B_GW]], axis=0).astype(BF16))
        vv.append(jnp.concatenate([kvp_ref[0, r, :, B_GW:], kvc_ref[0, r, :, B_GW:]], axis=0).astype(BF16))
    chains = [(r, j, h) for r in range(nr) for j in range(nblk) for h in range(B_HEADS)]
    s, pn, lse, oh = {}, {}, {}, {}
    for r, j, h in chains:
        s[r, j, h] = _dot_nt(jnp.where(masks[h], q[r][j * nk:(j + 1) * nk], zq), kk[r][j * nk:(j + 2) * nk])
    for r, j, h in chains:
        sm = jnp.where(band_first if j == 0 else band, s[r, j, h], -jnp.inf)
        m = jnp.max(sm, axis=-1, keepdims=True)
        p = jnp.exp(sm - m)
        den = jnp.sum(p, axis=-1, keepdims=True)
        lse[r, j, h] = m + jnp.log(den)
        pn[r, j, h] = (p / den).astype(BF16)
    for r, j, h in chains:
        oh[r, j, h] = jnp.dot(pn[r, j, h], vv[r][j * nk:(j + 2) * nk], preferred_element_type=F32)
    for r in range(nr):
        for j in range(nblk):
            o_acc = jnp.zeros((nk, B_GW), F32)
            l_acc = jnp.zeros((nk, B_GW), F32)
            for h in range(B_HEADS):
                o_acc = jnp.where(masks[h], oh[r, j, h], o_acc)
                l_acc = jnp.where(masks[h], lse[r, j, h], l_acc)
            o_ref[0, r, j * nk:(j + 1) * nk, :] = o_acc
            l_ref[0, r, j * nk:(j + 1) * nk, :] = l_acc


AP_ROWS = 1024


def _attn_prompt_call(q, kv):
    B, dil, n, _ = q.shape
    tq = min(AP_ROWS, n)
    nr = min(dil, AP_ROWS // tq)
    per = tq // B_NK
    blk = lambda w: pl.BlockSpec((1, nr, tq, w), lambda b, r, i: (b, r, i, 0))
    out = jax.ShapeDtypeStruct((B, dil, n, B_GW), F32)
    return pl.pallas_call(
        _attn_prompt_kernel,
        grid=(B, dil // nr, n // tq),
        in_specs=[blk(B_GW),
                  pl.BlockSpec((1, nr, B_NK, 2 * B_GW), lambda b, r, i: (b, r, jnp.maximum(i * per - 1, 0), 0)),
                  blk(2 * B_GW)],
        out_specs=[blk(B_GW), blk(B_GW)],
        out_shape=[out, out],
        compiler_params=_cparams(("parallel", "parallel", "arbitrary")),
        name="attn_prompt_d%d" % dil,
    )(q, kv, kv)


AS_QB = 16
AS_CACHE_BYTES = 16 * 1024 * 1024
AS_R = 8
AS_NEW = 16


def _attn_sample_kernel(q_ref, kvn_ref, c_ref, o_ref, l_ref, *, dil, g, sub_blocks):
    T = q_ref.shape[0]
    cb, Lb = c_ref.shape[1], c_ref.shape[3]
    R, NR = AS_R, B_HEADS * AS_R
    sub = pl.program_id(0) % sub_blocks
    row = lax.broadcasted_iota(jnp.int32, (NR, B_GW), 0)
    lane = lax.broadcasted_iota(jnp.int32, (NR, B_GW), 1)
    hmask = (lane // B_HD) == (row // R)

    def query_of(shape):
        t = lax.broadcasted_iota(jnp.int32, shape, 0) % R
        return jnp.where(t < T, t, 0)

    pos = lax.broadcasted_iota(jnp.int32, (NR, Lb), 1)
    tn = lax.broadcasted_iota(jnp.int32, (NR, AS_NEW), 1)
    if dil == 1:
        valid_c = pos >= query_of((NR, Lb))
        valid_n = tn <= query_of((NR, AS_NEW))
    else:
        valid_c = (pos % dil) == query_of((NR, Lb))
        valid_n = tn == query_of((NR, AS_NEW))

    for bi in range(cb):
        b = sub * cb + bi
        qrows = jnp.concatenate([q_ref[t, pl.ds(b, 1), g * B_GW:(g + 1) * B_GW] for t in range(T)]
                                + [jnp.zeros((R - T, B_GW), F32)], axis=0)
        qbd = jnp.where(hmask, jnp.concatenate([qrows] * B_HEADS, axis=0), 0.0)
        new = jnp.concatenate([kvn_ref[t, pl.ds(b, 1), :] for t in range(T)]
                              + [jnp.zeros((AS_NEW - T, 2 * B_GW), F32)], axis=0)
        s = jnp.where(valid_c, _dot(qbd, c_ref[0, bi, 0:B_GW, :]), -jnp.inf)
        sn = jnp.where(valid_n, _dot_nt(qbd, new[:, 0:B_GW]), -jnp.inf)
        m = jnp.maximum(jnp.max(s, axis=-1, keepdims=True), jnp.max(sn, axis=-1, keepdims=True))
        p = jnp.exp(s - m)
        pn = jnp.exp(sn - m)
        den = jnp.sum(p, axis=-1, keepdims=True) + jnp.sum(pn, axis=-1, keepdims=True)
        o = _dot_nt(p / den, c_ref[0, bi, B_GW:2 * B_GW, :]) + _dot(pn / den, new[:, B_GW:])
        om = jnp.where(hmask, o, 0.0)
        lm = jnp.where(hmask, m + jnp.log(den), 0.0)
        ob = om[0:R]
        lb = lm[0:R]
        for h in range(1, B_HEADS):
            ob = ob + om[h * R:(h + 1) * R]
            lb = lb + lm[h * R:(h + 1) * R]
        for t in range(T):
            o_ref[t, pl.ds(b, 1), :] = ob[t:t + 1]
            l_ref[t, pl.ds(b, 1), :] = lb[t:t + 1]


def _attn_sample_call(q, kvn, cache_t, layer, g, dil):
    T, B, _ = q.shape
    Lb = cache_t.shape[3]
    qb = min(AS_QB, B)
    cb = max(1, min(qb, AS_CACHE_BYTES // (2 * B_GW * Lb * 4)))
    sub_blocks = qb // cb
    out = jax.ShapeDtypeStruct((T, B, B_GW), F32)
    row_spec = lambda w: pl.BlockSpec((T, qb, w), lambda i: (0, i // sub_blocks, 0))
    return pl.pallas_call(
        functools.partial(_attn_sample_kernel, dil=dil, g=g, sub_blocks=sub_blocks),
        grid=(B // cb,),
        in_specs=[row_spec(B_W), row_spec(2 * B_GW),
                  pl.BlockSpec((1, cb, 2 * B_GW, Lb), lambda i: (layer, i, 0, 0))],
        out_specs=[row_spec(B_GW), row_spec(B_GW)],
        out_shape=[out, out],
        compiler_params=_cparams(("arbitrary",)),
        name="attn_sample_d%d" % dil,
    )(q, kvn, cache_t)


def _shift_rows(x, s, fill):
    rolled = pltpu.roll(x, s, 0)
    r = lax.broadcasted_iota(jnp.int32, x.shape, 0)
    return jnp.where(r < s, fill, rolled)


def _conv_init(xx_ref, prev, hp):
    halo = prev.shape[0]
    if halo < hp:
        xx_ref[0:hp, :] = jnp.zeros((hp, xx_ref.shape[1]), F32)
    xx_ref[hp - halo:hp, :] = prev


def _conv_taps(xx_ref, x, w_ref, b_ref, n_taps, unit, hp, c0, c1):
    tm = x.shape[0]
    halo = (n_taps - 1) * unit
    y = b_ref[:, c0:c1] + x * w_ref[n_taps - 1:n_taps, c0:c1]
    if unit % 8 == 0:
        xx_ref[hp:hp + tm, c0:c1] = x
        for j in range(n_taps - 1):
            y = y + xx_ref[hp - halo + j * unit:hp - halo + j * unit + tm, c0:c1] * w_ref[j:j + 1, c0:c1]
        tail = xx_ref[hp + tm - halo:hp + tm, c0:c1]
        xx_ref[hp - halo:hp, c0:c1] = tail
    else:
        xcat = jnp.concatenate([xx_ref[0:hp, c0:c1], x], axis=0)
        for j in range(n_taps - 1):
            y = y + pltpu.roll(xcat, halo - j * unit, 0)[hp:hp + tm, :] * w_ref[j:j + 1, c0:c1]
        tail = x[tm - halo:tm, :]
        xx_ref[0:hp, c0:c1] = x[tm - hp:tm, :]
    return y, tail


def _rglru_kernel(x_ref, g_ref, prev_ref, h0_ref, cw_ref, cb_ref, wr_ref, br_ref, wi_ref, bi_ref, lam_ref,
                  o_ref, hl_ref, xx_ref, hc_ref, *, unit, hp):
    i = pl.program_id(1)
    tm = x_ref.shape[1]

    @pl.when(i == 0)
    def _():
        _conv_init(xx_ref, prev_ref[0], hp)
        hc_ref[...] = h0_ref[0]

    y, _ = _conv_taps(xx_ref, x_ref[0].astype(F32), cw_ref, cb_ref, C_CONV, unit, hp, 0, C_WIDTH)

    r = _sigmoid(_dot(y, wr_ref[...]) + br_ref[...])
    ig = _sigmoid(_dot(y, wi_ref[...]) + bi_ref[...])
    log_a = -C_POW * r * _softplus(-lam_ref[...])
    a = jnp.exp(log_a)
    th = jnp.tanh(log_a)
    bx = jnp.sqrt(-2.0 * th / (1.0 - th)) * (ig * y)
    hc = hc_ref[...]
    if unit == 1:
        rr = lax.broadcasted_iota(jnp.int32, bx.shape, 0)
        bx = jnp.where(rr < 1, a * hc + bx, bx)
    elif unit == tm:
        bx = a * hc + bx
    else:
        bx = jnp.concatenate([a[0:unit] * hc + bx[0:unit], bx[unit:]], axis=0)
    s = unit
    while s < tm:
        if s % 8 == 0:
            bx = jnp.concatenate([bx[0:s], a[s:] * bx[0:tm - s] + bx[s:]], axis=0)
            a = jnp.concatenate([a[0:s], a[s:] * a[0:tm - s]], axis=0)
        else:
            a_sh = _shift_rows(a, s, 1.0)
            b_sh = _shift_rows(bx, s, 0.0)
            bx = a * b_sh + bx
            a = a * a_sh
        s *= 2
    h = bx
    hl = h[tm - unit:tm, :]
    hc_ref[...] = hl
    hl_ref[0] = hl
    o_ref[0] = h * jax.nn.gelu(g_ref[0].astype(F32))


def _rglru_call(proj3, prev, h0, cw, cb, wr, br, wi, bi, lam, unit, tm):
    nseq, rows, _ = proj3.shape
    halo = (C_CONV - 1) * unit
    hp = -(-halo // 8) * 8
    assert unit == 1 or unit % 8 == 0
    vec = lambda: pl.BlockSpec((1, C_WIDTH), lambda n, i: (0, 0))
    return pl.pallas_call(
        functools.partial(_rglru_kernel, unit=unit, hp=hp),
        grid=(nseq, rows // tm),
        in_specs=[pl.BlockSpec((1, tm, C_WIDTH), lambda n, i: (n, i, OFF_XC // C_WIDTH)),
                  pl.BlockSpec((1, tm, C_WIDTH), lambda n, i: (n, i, OFF_GC // C_WIDTH)),
                  pl.BlockSpec((1, halo, C_WIDTH), lambda n, i: (n, 0, 0)),
                  pl.BlockSpec((1, unit, C_WIDTH), lambda n, i: (n, 0, 0)),
                  pl.BlockSpec((C_CONV, C_WIDTH), lambda n, i: (0, 0)), vec(),
                  pl.BlockSpec((C_WIDTH, C_WIDTH), lambda n, i: (0, 0)), vec(),
                  pl.BlockSpec((C_WIDTH, C_WIDTH), lambda n, i: (0, 0)), vec(), vec()],
        out_specs=[pl.BlockSpec((1, tm, C_WIDTH), lambda n, i: (n, i, 0)),
                   pl.BlockSpec((1, unit, C_WIDTH), lambda n, i: (n, 0, 0))],
        out_shape=[jax.ShapeDtypeStruct((nseq, rows, C_WIDTH), F32),
                   jax.ShapeDtypeStruct((nseq, unit, C_WIDTH), F32)],
        scratch_shapes=[pltpu.VMEM((hp + tm, C_WIDTH), F32), pltpu.VMEM((unit, C_WIDTH), F32)],
        compiler_params=_cparams(("parallel", "arbitrary")),
        name="rglru_u%d" % unit,
    )(proj3, proj3, prev, h0, cw, cb, wr, br, wi, bi, lam)


def _merge_kernel(x_ref, ga_ref, gb_ref, gc_ref, oa_ref, oc_ref,
                  o0_ref, o1_ref, o2_ref, l0_ref, l1_ref, l2_ref,
                  wpa_ref, wpb_ref, wpc_ref, wo_ref, g_ref, b_ref, out_ref, il_ref, *, dils):
    tm = x_ref.shape[1]

    def rows_in_order(ref, dil):
        if dil == 1:
            return ref[0, 0]
        for r in range(dil):
            for j in range(il_ref.shape[0]):
                il_ref[j, pl.ds(r, tm // dil, stride=dil), :] = ref[0, r, :, j * 128:(j + 1) * 128]
        return jnp.concatenate([il_ref[j] for j in range(il_ref.shape[0])], axis=1)

    o0, o1, o2 = (rows_in_order(r, d) for r, d in zip((o0_ref, o1_ref, o2_ref), dils))
    l0, l1, l2 = (rows_in_order(r, d) for r, d in zip((l0_ref, l1_ref, l2_ref), dils))
    m = jnp.maximum(jnp.maximum(l0, l1), l2)
    e0, e1, e2 = jnp.exp(l0 - m), jnp.exp(l1 - m), jnp.exp(l2 - m)
    den = e0 + e1 + e2
    ob = (e0 / den) * o0 + (e1 / den) * o1 + (e2 / den) * o2
    merged = (_sigmoid(ga_ref[0].astype(F32)) * _dot(oa_ref[0], wpa_ref[...])
              + _sigmoid(gb_ref[0].astype(F32)) * _dot(ob, wpb_ref[...])
              + _sigmoid(gc_ref[0].astype(F32)) * _dot(oc_ref[0], wpc_ref[...]))
    mix = _dot(merged, wo_ref[...])
    out_ref[0] = _layer_norm(DN_ALPHA * x_ref[0] + mix, g_ref[...], b_ref[...])


def _merge_call(x3, proj3, oa, oc, obs, lses, wpa, wpb, wpc, wo, g, b, tm):
    nseq, rows, _ = x3.shape
    dils = tuple(o.shape[1] for o in obs)
    row_spec = lambda w, j=0: pl.BlockSpec((1, tm, w), lambda n, i: (n, i, j))
    grp_spec = lambda d: pl.BlockSpec((1, d, tm // d, B_GW), lambda n, i: (n, 0, i, 0))
    full = lambda a: pl.BlockSpec(a.shape, lambda n, i: (0, 0))
    return pl.pallas_call(
        functools.partial(_merge_kernel, dils=dils),
        grid=(nseq, rows // tm),
        in_specs=[row_spec(D_MODEL)] + [row_spec(D_MODEL, OFF_GATES // D_MODEL + j) for j in range(3)]
                 + [row_spec(A_VW), row_spec(C_WIDTH)]
                 + [grp_spec(d) for d in dils] * 2
                 + [full(wpa), full(wpb), full(wpc), full(wo), full(g), full(b)],
        out_specs=row_spec(D_MODEL),
        out_shape=jax.ShapeDtypeStruct((nseq, rows, D_MODEL), F32),
        scratch_shapes=[pltpu.VMEM((B_GW // 128, tm, 128), F32)],
        compiler_params=_cparams(("parallel", "parallel")),
        name="merge_ln",
    )(x3, proj3, proj3, proj3, oa, oc, *obs, *lses, wpa, wpb, wpc, wo, g, b)


def _ffn_kernel(x_ref, prev_ref, wu_ref, cw_ref, cb_ref, wd_ref, g_ref, b_ref,
                o_ref, st_ref, xx_ref, *, unit, hp):
    i = pl.program_id(1)
    tm = x_ref.shape[1]

    @pl.when(i == 0)
    def _():
        _conv_init(xx_ref, prev_ref[0], hp)

    x = x_ref[0]
    xb = x.astype(BF16)
    f = jnp.zeros((tm, D_MODEL), F32)
    for c0 in range(0, D_FF, FF_CHUNK):
        c1 = c0 + FF_CHUNK
        gate = jnp.dot(xb, wu_ref[:, c0:c1], preferred_element_type=F32)
        up = jnp.dot(xb, wu_ref[:, D_FF + c0:D_FF + c1], preferred_element_type=F32)
        y, tail = _conv_taps(xx_ref, gate, cw_ref, cb_ref, F_CONV, unit, hp, c0, c1)
        st_ref[0, :, c0:c1] = tail
        f = f + _dot(_silu(y) * up, wd_ref[c0:c1, :])
    o_ref[0] = _layer_norm(DN_ALPHA * x + f, g_ref[...], b_ref[...])


def _ffn_call(x3, prev, wu, cw, cb, wd, g, b, unit, tm):
    nseq, rows, _ = x3.shape
    halo = (F_CONV - 1) * unit
    hp = -(-halo // 8) * 8
    const = lambda a: pl.BlockSpec(a.shape, lambda n, i: (0, 0), pipeline_mode=pl.Buffered(1))
    return pl.pallas_call(
        functools.partial(_ffn_kernel, unit=unit, hp=hp),
        grid=(nseq, rows // tm),
        in_specs=[pl.BlockSpec((1, tm, D_MODEL), lambda n, i: (n, i, 0)),
                  pl.BlockSpec((1, halo, D_FF), lambda n, i: (n, 0, 0)),
                  const(wu), const(cw), const(cb), const(wd), const(g), const(b)],
        out_specs=[pl.BlockSpec((1, tm, D_MODEL), lambda n, i: (n, i, 0)),
                   pl.BlockSpec((1, halo, D_FF), lambda n, i: (n, 0, 0))],
        out_shape=[jax.ShapeDtypeStruct((nseq, rows, D_MODEL), F32),
                   jax.ShapeDtypeStruct((nseq, halo, D_FF), F32)],
        scratch_shapes=[pltpu.VMEM((hp + tm, D_FF), F32)],
        compiler_params=_cparams(("parallel", "arbitrary")),
        name="conv_ffn_u%d" % unit,
    )(x3, prev, wu, cw, cb, wd, g, b)


def _block_diag(w):
    nblk, bw, _ = w.shape
    eye = jnp.eye(nblk, dtype=w.dtype)
    return (eye[:, None, :, None] * w[:, :, None, :]).reshape(nblk * bw, nblk * bw)


def _layer_params(l, w_in, a_conv_w, a_A_log, a_dt_bias, a_norm_w, c_conv_w, c_conv_b, c_w_r, c_b_r,
                  c_w_i, c_b_i, c_lam, w_pa, w_pb, w_pc, w_o, ln1_g, ln1_b, f_up, f_conv_w, f_conv_b,
                  f_down, ln2_g, ln2_b):
    w = w_in[l]
    o_b = A_QKV
    o_a = o_b + A_HEADS
    o_z = o_a + A_HEADS
    o_qb = o_z + A_VW
    o_xc = o_qb + 3 * B_W
    o_gc = o_xc + C_WIDTH
    o_gt = o_gc + C_WIDTH
    pad = jnp.zeros((D_MODEL, N_PROJ - OFF_BA - 2 * A_HEADS), w.dtype)
    wp = jnp.concatenate([w[:, o_gt:o_gt + 3 * D_MODEL], w[:, o_xc:o_gc], w[:, o_gc:o_gt], w[:, o_z:o_qb],
                          w[:, 0:o_b], w[:, o_qb:o_xc], w[:, o_b:o_z], pad], axis=1).astype(BF16)
    gp = jnp.zeros((2, 128), F32)
    gp = gp.at[0, A_HEADS:2 * A_HEADS].set(a_A_log[l]).at[1, A_HEADS:2 * A_HEADS].set(a_dt_bias[l])
    row = lambda v: v.reshape(1, -1)
    return dict(
        wp=wp, a_cw=a_conv_w[l], gp=gp, a_nw=row(a_norm_w[l]),
        c_cw=c_conv_w[l], c_cb=row(c_conv_b[l]),
        c_wr=_block_diag(c_w_r[l]).astype(BF16), c_br=row(c_b_r[l]),
        c_wi=_block_diag(c_w_i[l]).astype(BF16), c_bi=row(c_b_i[l]), c_lam=row(c_lam[l]),
        wpa=w_pa[l].astype(BF16), wpb=w_pb[l].astype(BF16), wpc=w_pc[l].astype(BF16), wo=w_o[l].astype(BF16),
        ln1_g=row(ln1_g[l]), ln1_b=row(ln1_b[l]),
        f_up=f_up[l].astype(BF16), f_cw=f_conv_w[l], f_cb=row(f_conv_b[l]), f_down=f_down[l].astype(BF16),
        ln2_g=row(ln2_g[l]), ln2_b=row(ln2_b[l]))


def _layer_tail(x3, proj3, o_a, obs, lses, unit, p, c_prev, c_h0, f_prev, tm_merge, tm_seq, tm_ffn):
    o_c, c_hl = _rglru_call(proj3, c_prev, c_h0, p["c_cw"], p["c_cb"], p["c_wr"], p["c_br"], p["c_wi"],
                            p["c_bi"], p["c_lam"], unit, tm_seq)
    x1 = _merge_call(x3, proj3, o_a, o_c, obs, lses, p["wpa"], p["wpb"], p["wpc"], p["wo"],
                     p["ln1_g"], p["ln1_b"], tm_merge)
    x_out, f_state = _ffn_call(x1, f_prev, p["f_up"], p["f_cw"], p["f_cb"], p["f_down"],
                               p["ln2_g"], p["ln2_b"], unit, tm_ffn)
    return x_out, c_hl, f_state


def kernel(x_prompt, x_sample, state_a_conv, state_a_rec, cache_b_w128, cache_b_w512, cache_b_w2048,
           state_c_conv, state_c_h, state_f_conv, ln_in_g, ln_in_b, w_in, a_conv_w, a_A_log, a_dt_bias,
           a_norm_w, c_conv_w, c_conv_b, c_w_r, c_b_r, c_w_i, c_b_i, c_lam, w_pa, w_pb, w_pc, w_o,
           ln1_g, ln1_b, f_up, f_conv_w, f_conv_b, f_down, ln2_g, ln2_b):
    Bp, Sp, _ = x_prompt.shape
    Bs, Ts, _ = x_sample.shape
    caches = (cache_b_w128, cache_b_w512, cache_b_w2048)
    for (win, dil), cache in zip(B_GROUPS, caches):
        assert win // dil == B_NK and cache.shape[2] == win and Sp % (dil * B_NK) == 0

    caches_t = [jnp.transpose(c, (0, 1, 3, 4, 5, 2)).reshape(DEPTH, Bs, 2 * B_GW, c.shape[2]) for c in caches]
    Rs = Ts * Bs

    lng, lnb = ln_in_g.reshape(1, -1), ln_in_b.reshape(1, -1)
    hp = _ln_call(x_prompt.reshape(Bp * Sp, D_MODEL), lng, lnb, 512)
    xs_tm = jnp.swapaxes(x_sample, 0, 1).reshape(Rs, D_MODEL)
    hs = _ln_call(xs_tm, lng, lnb, Rs)

    tabs_p = _rope_tables(jnp.arange(Sp))
    tabs_s = _rope_tables(jnp.repeat(PAST_LEN + jnp.arange(Ts), Bs))

    zeros_c = jnp.zeros((Bp, C_CONV - 1, C_WIDTH), F32)
    zeros_h = jnp.zeros((Bp, 1, C_WIDTH), F32)
    zeros_f = jnp.zeros((Bp, F_CONV - 1, D_FF), F32)

    outs_p, outs_s = [], []
    for l in range(DEPTH):
        p = _layer_params(l, w_in, a_conv_w, a_A_log, a_dt_bias, a_norm_w, c_conv_w, c_conv_b, c_w_r, c_b_r,
                          c_w_i, c_b_i, c_lam, w_pa, w_pb, w_pc, w_o, ln1_g, ln1_b, f_up, f_conv_w,
                          f_conv_b, f_down, ln2_g, ln2_b)

        proj3 = _proj_call(hp, p["wp"], 1024, BF16).reshape(Bp, Sp, N_PROJ)
        hp_tail = hp.reshape(Bp, Sp, D_MODEL)[:, Sp - TAIL_ROWS:].reshape(Bp * TAIL_ROWS, D_MODEL)
        proj_tail = _proj_call(hp_tail, p["wp"], Bp * TAIL_ROWS, F32).reshape(Bp, TAIL_ROWS, N_PROJ)
        o_a, a_rec = _gdn_prompt_call(proj3, p["a_cw"], p["gp"], p["a_nw"])
        rp = _rope_prompt_call(proj3, tabs_p, 512)
        res = [_attn_prompt_call(rp[g], rp[3 + g]) for g in range(len(B_GROUPS))]
        hp3, c_hl, f_st = _layer_tail(hp.reshape(Bp, Sp, D_MODEL), proj3, o_a, [r[0] for r in res],
                                      [r[1] for r in res], 1, p, zeros_c, zeros_h, zeros_f, 512, 256, 512)
        hp = hp3.reshape(Bp * Sp, D_MODEL)
        kv_rows_p = [t.reshape(Bp, 2, B_HEADS, B_HD, t.shape[-1]).transpose(0, 4, 1, 2, 3) for t in rp[6:9]]
        outs_p.append((
            proj_tail[:, TAIL_ROWS - (A_CONV - 1):, OFF_QKVA:OFF_QKVA + A_QKV],
            a_rec, kv_rows_p[0], kv_rows_p[1], kv_rows_p[2],
            proj_tail[:, TAIL_ROWS - (C_CONV - 1):, OFF_XC:OFF_XC + C_WIDTH],
            c_hl[:, 0],
            f_st))

        proj3 = _proj_call(hs, p["wp"], Rs, F32).reshape(1, Rs, N_PROJ)
        pt = proj3.reshape(Ts, Bs, N_PROJ)
        a_st = jnp.swapaxes(state_a_conv[l], 0, 1)
        o_a, a_rec = _gdn_sample_call(pt, a_st, state_a_rec, l, p["a_cw"], p["gp"], p["a_nw"])
        rs = _rope_sample_call(proj3, tabs_s, Bs)
        q3 = rs[0].reshape(Ts, Bs, B_W)
        res = [_attn_sample_call(q3, rs[1 + g].reshape(Ts, Bs, 2 * B_GW), caches_t[g], l, g, dil)
               for g, (_, dil) in enumerate(B_GROUPS)]
        c_prev = jnp.swapaxes(state_c_conv[l], 0, 1).reshape(1, (C_CONV - 1) * Bs, C_WIDTH)
        f_prev = jnp.swapaxes(state_f_conv[l], 0, 1).reshape(1, (F_CONV - 1) * Bs, D_FF)
        hs3, c_hl, f_st = _layer_tail(hs.reshape(1, Rs, D_MODEL), proj3, o_a.reshape(1, Rs, A_VW),
                                      [r[0].reshape(1, 1, Rs, B_GW) for r in res],
                                      [r[1].reshape(1, 1, Rs, B_GW) for r in res],
                                      Bs, p, c_prev, state_c_h[l][None], f_prev, min(256, Rs), Rs, Rs)
        hs = hs3.reshape(Rs, D_MODEL)
        tm2bm = lambda a: jnp.swapaxes(a, 0, 1)
        kv_rows_s = [t.reshape(Ts, 2, B_HEADS, B_HD, Bs).transpose(4, 0, 1, 2, 3) for t in rs[4:7]]
        outs_s.append((
            tm2bm(pt[Ts - (A_CONV - 1):, :, OFF_QKVA:OFF_QKVA + A_QKV]),
            a_rec, kv_rows_s[0], kv_rows_s[1], kv_rows_s[2],
            tm2bm(pt[Ts - (C_CONV - 1):, :, OFF_XC:OFF_XC + C_WIDTH]),
            c_hl[0],
            tm2bm(f_st.reshape(F_CONV - 1, Bs, D_FF))))

    stack = lambda outs: [jnp.stack(s, 0) for s in zip(*outs)]
    y_p = hp.reshape(Bp, Sp, D_MODEL)
    y_s = jnp.swapaxes(hs.reshape(Ts, Bs, D_MODEL), 0, 1)
    return (y_p, y_s, *stack(outs_p), *stack(outs_s))
```

```python
import functools

import jax
import jax.numpy as jnp
from jax import lax
from jax.experimental import pallas as pl
from jax.experimental.pallas import tpu as pltpu

F32 = jnp.float32
BF16 = jnp.bfloat16

D_MODEL = 1024
DEPTH = 4
PAST_LEN = 2048
A_HEADS, A_DK, A_DV, A_CONV, A_CHUNK = 4, 128, 128, 4, 64
B_GROUPS = ((128, 1), (512, 4), (2048, 16))
B_HEADS, B_HD = 4, 64
B_ROT = B_HD // 4
ROPE_THETA = 500000.0
B_NK = 128
C_WIDTH, C_CONV, C_POW = 512, 4, 8.0
D_FF = ((8 * D_MODEL // 3 + 255) // 256) * 256
F_CONV = 3
FF_CHUNK = D_FF
DN_ALPHA = (2 * DEPTH) ** 0.25
LN_EPS = 1e-5
NORM_EPS = 1e-6

A_QK = A_HEADS * A_DK
A_VW = A_HEADS * A_DV
A_QKV = 2 * A_QK + A_VW
B_GW = B_HEADS * B_HD
B_W = len(B_GROUPS) * B_GW

OFF_GATES = 0
OFF_XC = 3072
OFF_GC = 3584
OFF_Z = 4096
OFF_QKVA = 4608
OFF_QB = 6144
OFF_BA = 8448
N_PROJ = 8704
PROJ_TN = 2176
TAIL_ROWS = 8

VMEM_LIMIT = 56 * 1024 * 1024


def _cparams(sem, **kw):
    return pltpu.CompilerParams(dimension_semantics=sem, vmem_limit_bytes=VMEM_LIMIT, **kw)


def _sigmoid(x):
    return 0.5 + 0.5 * jnp.tanh(0.5 * x)


def _silu(x):
    h = 0.5 * x
    return h + h * jnp.tanh(h)


def _softplus(x):
    return jnp.maximum(x, 0.0) + jnp.log1p(jnp.exp(-jnp.abs(x)))


def _layer_norm(x, g, b):
    mu = jnp.mean(x, axis=-1, keepdims=True)
    xc = x - mu
    var = jnp.mean(xc * xc, axis=-1, keepdims=True)
    return xc * lax.rsqrt(var + LN_EPS) * g + b


def _dot(a, b):
    return jnp.dot(a.astype(BF16), b.astype(BF16), preferred_element_type=F32)


def _dot_nt(a, b):
    return lax.dot_general(a.astype(BF16), b.astype(BF16), (((1,), (1,)), ((), ())),
                           preferred_element_type=F32)


def _dot_tn(a, b):
    return lax.dot_general(a.astype(BF16), b.astype(BF16), (((0,), (0,)), ((), ())),
                           preferred_element_type=F32)


def _split2(a):
    hi = a.astype(BF16)
    lo = (a - hi.astype(F32)).astype(BF16)
    return hi, lo


def _dot3(a, b):
    ah, al = _split2(a)
    bh, bl = _split2(b)
    d = lambda x, y: jnp.dot(x, y, preferred_element_type=F32)
    return d(ah, bh) + (d(ah, bl) + d(al, bh))


def _ln_kernel(x_ref, g_ref, b_ref, o_ref):
    o_ref[...] = _layer_norm(x_ref[...], g_ref[...], b_ref[...])


def _ln_call(x, g, b, tm):
    rows = x.shape[0]
    return pl.pallas_call(
        _ln_kernel,
        grid=(rows // tm,),
        in_specs=[pl.BlockSpec((tm, D_MODEL), lambda i: (i, 0)),
                  pl.BlockSpec((1, D_MODEL), lambda i: (0, 0)),
                  pl.BlockSpec((1, D_MODEL), lambda i: (0, 0))],
        out_specs=pl.BlockSpec((tm, D_MODEL), lambda i: (i, 0)),
        out_shape=jax.ShapeDtypeStruct((rows, D_MODEL), F32),
        compiler_params=_cparams(("parallel",)),
        name="entry_ln",
    )(x, g, b)


def _proj_kernel(x_ref, w_ref, o_ref, xb_ref):
    @pl.when(pl.program_id(1) == 0)
    def _():
        xb_ref[...] = x_ref[...].astype(BF16)

    o_ref[...] = jnp.dot(xb_ref[...], w_ref[...], preferred_element_type=F32).astype(o_ref.dtype)


def _proj_call(x, w, tm, out_dtype):
    rows = x.shape[0]
    return pl.pallas_call(
        _proj_kernel,
        grid=(rows // tm, N_PROJ // PROJ_TN),
        in_specs=[pl.BlockSpec((tm, D_MODEL), lambda i, j: (i, 0)),
                  pl.BlockSpec((D_MODEL, PROJ_TN), lambda i, j: (0, j))],
        out_specs=pl.BlockSpec((tm, PROJ_TN), lambda i, j: (i, j)),
        out_shape=jax.ShapeDtypeStruct((rows, N_PROJ), out_dtype),
        scratch_shapes=[pltpu.VMEM((tm, D_MODEL), BF16)],
        compiler_params=_cparams(("parallel", "arbitrary")),
        name="in_proj",
    )(x, w)


def _gdn_prompt_kernel(x_ref, z_ref, ba_ref, cw_ref, gp_ref, nw_ref,
                       o_ref, sfin_ref, xx_ref, s_ref):
    c = pl.program_id(0)
    C = A_CHUNK
    nb = x_ref.shape[0]

    @pl.when(c == 0)
    def _():
        xx_ref[...] = jnp.zeros((nb, 8, A_QKV), F32)
        s_ref[...] = jnp.zeros(s_ref.shape, F32)

    ri = lax.broadcasted_iota(jnp.int32, (C, C), 0)
    ci = lax.broadcasted_iota(jnp.int32, (C, C), 1)
    tril = ri >= ci
    strict = ri > ci
    eye = (ri == ci).astype(F32)
    trilf = tril.astype(F32)
    nw = nw_ref[...]

    ch = []
    for b in range(nb):
        x = x_ref[b].astype(F32)
        xcat = jnp.concatenate([xx_ref[b], x], axis=0)
        y = pltpu.roll(xcat, A_CONV - 1, 0)[8:8 + C, :] * cw_ref[0:1, :]
        for j in range(1, A_CONV - 1):
            y = y + pltpu.roll(xcat, A_CONV - 1 - j, 0)[8:8 + C, :] * cw_ref[j:j + 1, :]
        y = y + x * cw_ref[A_CONV - 1:A_CONV, :]
        xx_ref[b] = x[C - 8:C, :]
        y = _silu(y)
        ba = ba_ref[b].astype(F32)
        beta_all = _sigmoid(ba)
        g_all = -jnp.exp(gp_ref[0:1, :]) * _softplus(ba + gp_ref[1:2, :])
        gc = _dot3(trilf, g_all)
        gct = jnp.concatenate([gc, jnp.zeros((128 - C, 128), F32)], axis=0).T
        for h in range(A_HEADS):
            q = y[:, h * A_DK:(h + 1) * A_DK]
            k = y[:, A_QK + h * A_DK:A_QK + (h + 1) * A_DK]
            v = y[:, 2 * A_QK + h * A_DV:2 * A_QK + (h + 1) * A_DV]
            q = q * lax.rsqrt(jnp.sum(q * q, axis=-1, keepdims=True) + NORM_EPS) * (A_DK ** -0.5)
            k = k * lax.rsqrt(jnp.sum(k * k, axis=-1, keepdims=True) + NORM_EPS)
            beta = beta_all[:, h:h + 1]
            gcol = gc[:, 4 + h:5 + h]
            grow = gct[4 + h:5 + h, 0:C]
            decay = jnp.where(tril, jnp.exp(jnp.where(tril, gcol - grow, 0.0)), 0.0)
            eg = jnp.exp(gcol)
            glast = gcol[C - 1:C, :]
            ch.append(dict(b=b, h=h, q=q, k=k, kb=k * beta, vb=v * beta, decay=decay, eg=eg,
                           k_dec=k * jnp.exp(glast - gcol), q_dec=q * eg, eglast=jnp.exp(glast)))
    for d in ch:
        d["m"] = -jnp.where(strict, _dot_nt(d["kb"], d["k"]) * d["decay"], 0.0)
        d["t"] = eye + d["m"]
    for d in ch:
        d["qk"] = jnp.where(tril, _dot_nt(d["q"], d["k"]) * d["decay"], 0.0)
    for _ in range(5):
        for d in ch:
            d["m"] = _dot(d["m"], d["m"])
        for d in ch:
            d["t"] = d["t"] + _dot(d["t"], d["m"])
    for d in ch:
        d["u"] = _dot(d["t"], d["vb"])
        d["w"] = _dot(d["t"], d["kb"] * d["eg"])
    for d in ch:
        d["s"] = s_ref[d["b"], d["h"]]
        d["v_new"] = d["u"] - _dot(d["w"], d["s"])
    for d in ch:
        d["o"] = _dot(d["q_dec"], d["s"]) + _dot(d["qk"], d["v_new"])
        s_ref[d["b"], d["h"]] = d["s"] * d["eglast"] + _dot_tn(d["k_dec"], d["v_new"])
    for d in ch:
        b, h, o = d["b"], d["h"], d["o"]
        o = o * lax.rsqrt(jnp.mean(o * o, axis=-1, keepdims=True) + NORM_EPS) * nw
        o_ref[b, :, h * A_DV:(h + 1) * A_DV] = o * _silu(z_ref[b, :, h * A_DV:(h + 1) * A_DV].astype(F32))

    @pl.when(c == pl.num_programs(0) - 1)
    def _():
        sfin_ref[...] = s_ref[...]


def _gdn_prompt_call(proj3, cw, gp, nw):
    B, S, _ = proj3.shape
    C = A_CHUNK
    return pl.pallas_call(
        _gdn_prompt_kernel,
        grid=(S // C,),
        in_specs=[pl.BlockSpec((B, C, A_QKV), lambda c: (0, c, OFF_QKVA // A_QKV)),
                  pl.BlockSpec((B, C, A_VW), lambda c: (0, c, OFF_Z // A_VW)),
                  pl.BlockSpec((B, C, 128), lambda c: (0, c, OFF_BA // 128)),
                  pl.BlockSpec((A_CONV, A_QKV), lambda c: (0, 0)),
                  pl.BlockSpec((2, 128), lambda c: (0, 0)),
                  pl.BlockSpec((1, A_DV), lambda c: (0, 0))],
        out_specs=[pl.BlockSpec((B, C, A_VW), lambda c: (0, c, 0)),
                   pl.BlockSpec((B, A_HEADS, A_DK, A_DV), lambda c: (0, 0, 0, 0))],
        out_shape=[jax.ShapeDtypeStruct((B, S, A_VW), F32),
                   jax.ShapeDtypeStruct((B, A_HEADS, A_DK, A_DV), F32)],
        scratch_shapes=[pltpu.VMEM((B, 8, A_QKV), F32),
                        pltpu.VMEM((B, A_HEADS, A_DK, A_DV), F32)],
        compiler_params=_cparams(("arbitrary",)),
        name="gdn_prompt",
    )(proj3, proj3, proj3, cw, gp, nw)


GS_BB = 16
GS_C = 16


def _gdn_sample_kernel(x_ref, z_ref, ba_ref, st_ref, s0_ref, cw_ref, gp_ref, nw_ref,
                       o_ref, snew_ref, oscr_ref):
    T = x_ref.shape[0]
    bb = GS_BB
    xx = [st_ref[j] for j in range(A_CONV - 1)] + [x_ref[t].astype(F32) for t in range(T)]
    C = GS_C
    assert T <= C
    qs, ks, kbs, vbs, gcs = [], [], [], [], []
    for t in range(T):
        y = xx[t] * cw_ref[0:1, :]
        for j in range(1, A_CONV):
            y = y + xx[t + j] * cw_ref[j:j + 1, :]
        y = _silu(y)
        ba = ba_ref[t].astype(F32)
        beta_all = _sigmoid(ba)
        g_all = -jnp.exp(gp_ref[0:1, :]) * _softplus(ba + gp_ref[1:2, :])
        gcs.append(g_all if t == 0 else gcs[-1] + g_all)
        qs.append([]), ks.append([]), kbs.append([]), vbs.append([])
        for h in range(A_HEADS):
            q = y[:, h * A_DK:(h + 1) * A_DK]
            k = y[:, A_QK + h * A_DK:A_QK + (h + 1) * A_DK]
            v = y[:, 2 * A_QK + h * A_DV:2 * A_QK + (h + 1) * A_DV]
            q = q * lax.rsqrt(jnp.sum(q * q, axis=-1, keepdims=True) + NORM_EPS) * (A_DK ** -0.5)
            k = k * lax.rsqrt(jnp.sum(k * k, axis=-1, keepdims=True) + NORM_EPS)
            beta = beta_all[:, h:h + 1]
            qs[t].append(q), ks[t].append(k), kbs[t].append(k * beta), vbs[t].append(v * beta)

    ri = lax.broadcasted_iota(jnp.int32, (C, C), 0)
    ci = lax.broadcasted_iota(jnp.int32, (C, C), 1)
    tril = ri >= ci
    strict = ri > ci
    eye = (ri == ci).astype(F32)
    zrows = jnp.zeros((C - T, A_DK), F32)
    rows_of = lambda slabs, b, h: jnp.concatenate([slabs[t][h][b:b + 1, :] for t in range(T)] + [zrows], axis=0)
    ch = []
    for h in range(A_HEADS):
        for b in range(bb):
            gcol = jnp.concatenate([gcs[t][b:b + 1, 4 + h:5 + h] for t in range(T)]
                                   + [gcs[T - 1][b:b + 1, 4 + h:5 + h]] * (C - T), axis=0)
            grow = jnp.sum(eye * gcol, axis=0, keepdims=True)
            decay = jnp.where(tril, jnp.exp(jnp.where(tril, gcol - grow, 0.0)), 0.0)
            eg = jnp.exp(gcol)
            glast = gcol[C - 1:C, :]
            q, k, kb, vb = (rows_of(s, b, h) for s in (qs, ks, kbs, vbs))
            ch.append(dict(b=b, h=h, q=q, k=k, kb=kb, vb=vb, decay=decay, eg=eg,
                           k_dec=k * jnp.exp(glast - gcol), q_dec=q * eg, eglast=jnp.exp(glast)))
    for d in ch:
        d["m"] = -jnp.where(strict, _dot_nt(d["kb"], d["k"]) * d["decay"], 0.0)
        d["qk"] = jnp.where(tril, _dot_nt(d["q"], d["k"]) * d["decay"], 0.0)
    for d in ch:
        d["t"] = eye + d["m"]
    for _ in range(GS_C.bit_length() - 2):
        for d in ch:
            d["m"] = _dot(d["m"], d["m"])
        for d in ch:
            d["t"] = d["t"] + _dot(d["t"], d["m"])
    for d in ch:
        d["u"] = _dot(d["t"], d["vb"])
        d["w"] = _dot(d["t"], d["kb"] * d["eg"])
    for d in ch:
        d["s"] = s0_ref[0, d["b"], d["h"]]
        ws = _dot(jnp.concatenate([d["w"], d["q_dec"]], axis=0), d["s"])
        d["v_new"] = d["u"] - ws[0:C]
        d["qs"] = ws[C:2 * C]
    for d in ch:
        b, h = d["b"], d["h"]
        o = d["qs"] + _dot(d["qk"], d["v_new"])
        snew_ref[b, h] = d["s"] * d["eglast"] + _dot_tn(d["k_dec"], d["v_new"])
        for t in range(T):
            oscr_ref[t, b:b + 1, h * A_DV:(h + 1) * A_DV] = o[t:t + 1, :]
    nw = nw_ref[...]
    for t in range(T):
        z = z_ref[t].astype(F32)
        for h in range(A_HEADS):
            o = oscr_ref[t, :, h * A_DV:(h + 1) * A_DV]
            o = o * lax.rsqrt(jnp.mean(o * o, axis=-1, keepdims=True) + NORM_EPS) * nw
            o_ref[t, :, h * A_DV:(h + 1) * A_DV] = o * _silu(z[:, h * A_DV:(h + 1) * A_DV])


def _gdn_sample_call(proj3, st, s0, layer, cw, gp, nw):
    T, B, _ = proj3.shape
    bb = GS_BB
    return pl.pallas_call(
        _gdn_sample_kernel,
        grid=(B // bb,),
        in_specs=[pl.BlockSpec((T, bb, A_QKV), lambda i: (0, i, OFF_QKVA // A_QKV)),
                  pl.BlockSpec((T, bb, A_VW), lambda i: (0, i, OFF_Z // A_VW)),
                  pl.BlockSpec((T, bb, 128), lambda i: (0, i, OFF_BA // 128)),
                  pl.BlockSpec((A_CONV - 1, bb, A_QKV), lambda i: (0, i, 0)),
                  pl.BlockSpec((1, bb, A_HEADS, A_DK, A_DV), lambda i: (layer, i, 0, 0, 0)),
                  pl.BlockSpec((A_CONV, A_QKV), lambda i: (0, 0)),
                  pl.BlockSpec((2, 128), lambda i: (0, 0)),
                  pl.BlockSpec((1, A_DV), lambda i: (0, 0))],
        out_specs=[pl.BlockSpec((T, bb, A_VW), lambda i: (0, i, 0)),
                   pl.BlockSpec((bb, A_HEADS, A_DK, A_DV), lambda i: (i, 0, 0, 0))],
        out_shape=[jax.ShapeDtypeStruct((T, B, A_VW), F32),
                   jax.ShapeDtypeStruct(s0.shape[1:], F32)],
        scratch_shapes=[pltpu.VMEM((T, bb, A_VW), F32)],
        compiler_params=_cparams(("parallel",)),
        name="gdn_sample",
    )(proj3, proj3, proj3, st, s0, cw, gp, nw)


def _rope_qkv(q_ref, k_ref, v_ref, c_ref, s1_ref, s2_ref):
    reps = B_W // 128
    c = jnp.concatenate([c_ref[0]] * reps, axis=1)
    s1 = jnp.concatenate([s1_ref[0]] * reps, axis=1)
    s2 = jnp.concatenate([s2_ref[0]] * reps, axis=1)

    def rot(x):
        return x * c + pltpu.roll(x, B_W - B_ROT // 2, 1) * s1 + pltpu.roll(x, B_ROT // 2, 1) * s2

    return rot(q_ref[0].astype(F32)) * (B_HD ** -0.5), rot(k_ref[0].astype(F32)), v_ref[0].astype(F32)


def _rope_in_specs(tm):
    qb = OFF_QB // B_W
    tab_spec = pl.BlockSpec((1, tm, 128), lambda n, i: (0, i, 0))
    return [pl.BlockSpec((1, tm, B_W), lambda n, i: (n, i, qb)),
            pl.BlockSpec((1, tm, B_W), lambda n, i: (n, i, qb + 1)),
            pl.BlockSpec((1, tm, B_W), lambda n, i: (n, i, qb + 2)),
            tab_spec, tab_spec, tab_spec]


def _store_chunks(buf_ref, x):
    for j in range(buf_ref.shape[0]):
        buf_ref[j] = x[:, j * 128:(j + 1) * 128]


def _load_chunks_strided(buf_ref, start, size, stride):
    return jnp.concatenate([buf_ref[j, pl.ds(start, size, stride=stride), :] for j in range(buf_ref.shape[0])],
                           axis=1)


def _rope_prompt_kernel(q_ref, k_ref, v_ref, c_ref, s1_ref, s2_ref,
                        q0_ref, q1_ref, q2_ref, kv0_ref, kv1_ref, kv2_ref, t0_ref, t1_ref, t2_ref,
                        qs_ref, kvs_ref, *, tails):
    i = pl.program_id(1)
    tm = q_ref.shape[1]
    q, k, v = _rope_qkv(q_ref, k_ref, v_ref, c_ref, s1_ref, s2_ref)
    outs = ((q0_ref, kv0_ref, t0_ref), (q1_ref, kv1_ref, t1_ref), (q2_ref, kv2_ref, t2_ref))
    for g, ((_, dil), (qo_ref, kvo_ref, to_ref)) in enumerate(zip(B_GROUPS, outs)):
        lo, hi = g * B_GW, (g + 1) * B_GW
        qg = q[:, lo:hi]
        kv = jnp.concatenate([k[:, lo:hi], v[:, lo:hi]], axis=1)
        if dil == 1:
            qo_ref[0, 0] = qg.astype(BF16)
            kvo_ref[0, 0] = kv
        else:
            _store_chunks(qs_ref, qg)
            _store_chunks(kvs_ref, kv)
            for r in range(dil):
                qo_ref[0, r] = _load_chunks_strided(qs_ref, r, tm // dil, dil).astype(BF16)
                kvo_ref[0, r] = _load_chunks_strided(kvs_ref, r, tm // dil, dil)
        first, w = tails[g]

        @pl.when(i >= first)
        def _(to_ref=to_ref, w=w, kv=kv):
            to_ref[0] = kv[tm - w:tm, :].T


def _rope_prompt_call(proj3, tabs, tm):
    B, S, _ = proj3.shape
    nt = S // tm
    out_shape, out_specs, tails = [], [], []
    for width, dtype in ((B_GW, BF16), (2 * B_GW, F32)):
        for _, dil in B_GROUPS:
            out_shape.append(jax.ShapeDtypeStruct((B, dil, S // dil, width), dtype))
            out_specs.append(pl.BlockSpec((1, dil, tm // dil, width), lambda n, i: (n, 0, i, 0)))
    for win, _ in B_GROUPS:
        win = min(win, S)
        w = min(tm, win)
        first = nt - win // w
        tails.append((first, w))
        out_shape.append(jax.ShapeDtypeStruct((B, 2 * B_GW, win), F32))
        out_specs.append(pl.BlockSpec((1, 2 * B_GW, w),
                                      lambda n, i, first=first: (n, 0, jnp.maximum(i - first, 0))))
    return pl.pallas_call(
        functools.partial(_rope_prompt_kernel, tails=tuple(tails)),
        grid=(B, nt),
        in_specs=_rope_in_specs(tm),
        out_specs=out_specs,
        out_shape=out_shape,
        scratch_shapes=[pltpu.VMEM((B_GW // 128, tm, 128), F32), pltpu.VMEM((2 * B_GW // 128, tm, 128), F32)],
        compiler_params=_cparams(("parallel", "arbitrary")),
        name="rope_prompt",
    )(proj3, proj3, proj3, *tabs)


def _rope_sample_kernel(q_ref, k_ref, v_ref, c_ref, s1_ref, s2_ref,
                        qo_ref, kv0_ref, kv1_ref, kv2_ref, t0_ref, t1_ref, t2_ref, *, unit):
    q, k, v = _rope_qkv(q_ref, k_ref, v_ref, c_ref, s1_ref, s2_ref)
    qo_ref[0] = q
    for g, (kvo_ref, to_ref) in enumerate(((kv0_ref, t0_ref), (kv1_ref, t1_ref), (kv2_ref, t2_ref))):
        kv = jnp.concatenate([k[:, g * B_GW:(g + 1) * B_GW], v[:, g * B_GW:(g + 1) * B_GW]], axis=1)
        kvo_ref[0] = kv
        for t in range(q.shape[0] // unit):
            to_ref[t] = kv[t * unit:(t + 1) * unit, :].T


def _rope_sample_call(proj3, tabs, unit):
    _, rows, _ = proj3.shape
    T = rows // unit
    kv_shape = jax.ShapeDtypeStruct((1, rows, 2 * B_GW), F32)
    kv_spec = pl.BlockSpec((1, rows, 2 * B_GW), lambda n, i: (0, 0, 0))
    t_shape = jax.ShapeDtypeStruct((T, 2 * B_GW, unit), F32)
    t_spec = pl.BlockSpec((T, 2 * B_GW, unit), lambda n, i: (0, 0, 0))
    return pl.pallas_call(
        functools.partial(_rope_sample_kernel, unit=unit),
        grid=(1, 1),
        in_specs=_rope_in_specs(rows),
        out_specs=[pl.BlockSpec((1, rows, B_W), lambda n, i: (0, 0, 0))] + [kv_spec] * 3 + [t_spec] * 3,
        out_shape=[jax.ShapeDtypeStruct((1, rows, B_W), F32)] + [kv_shape] * 3 + [t_shape] * 3,
        compiler_params=_cparams(("arbitrary", "arbitrary")),
        name="rope_sample",
    )(proj3, proj3, proj3, *tabs)


def _rope_tables(pos):
    half = B_ROT // 2
    inv = ROPE_THETA ** (-jnp.arange(half, dtype=F32) / half)
    ang = pos.astype(F32)[:, None] * inv
    cos, sin = jnp.cos(ang), jnp.sin(ang)
    rows = pos.shape[0]
    one = jnp.ones((rows, B_HD - B_ROT), F32)
    zero = jnp.zeros((rows, half), F32)
    zrest = jnp.zeros((rows, B_HD - B_ROT), F32)
    c = jnp.concatenate([cos, cos, one], axis=1)
    s1 = jnp.concatenate([-sin, zero, zrest], axis=1)
    s2 = jnp.concatenate([zero, sin, zrest], axis=1)
    tile = lambda t: jnp.concatenate([t, t], axis=1)[None]
    return tile(c), tile(s1), tile(s2)


def _head_lane_mask(rows):
    lane = lax.broadcasted_iota(jnp.int32, (rows, B_GW), 1)
    return [(lane >= h * B_HD) & (lane < (h + 1) * B_HD) for h in range(B_HEADS)]


def _attn_prompt_kernel(q_ref, kvp_ref, kvc_ref, o_ref, l_ref):
    i = pl.program_id(2)
    nk = B_NK
    nr, nblk = q_ref.shape[1], q_ref.shape[2] // nk
    qi = lax.broadcasted_iota(jnp.int32, (nk, 2 * nk), 0)
    kj = lax.broadcasted_iota(jnp.int32, (nk, 2 * nk), 1)
    band = (kj >= qi) & (kj <= qi + nk)
    band_first = band & ((kj >= nk) | (i > 0))
    masks = _head_lane_mask(nk)
    zq = jnp.zeros((nk, B_GW), BF16)
    q, kk, vv = [], [], []
    for r in range(nr):
        q.append(q_ref[0, r])
        kk.append(jnp.concatenate([kvp_ref[0, r, :, 0:B_GW], kvc_ref[0, r, :, 0:B_GW]], axis=0).astype(BF16))
        vv.append(jnp.concatenate([kvp_ref[0, r, :, B_GW:], kvc_ref[0, r, :, B_GW:]], axis=0).astype(BF16))
    chains = [(r, j, h) for r in range(nr) for j in range(nblk) for h in range(B_HEADS)]
    s, pn, lse, oh = {}, {}, {}, {}
    for r, j, h in chains:
        s[r, j, h] = _dot_nt(jnp.where(masks[h], q[r][j * nk:(j + 1) * nk], zq), kk[r][j * nk:(j + 2) * nk])
    for r, j, h in chains:
        sm = jnp.where(band_first if j == 0 else band, s[r, j, h], -jnp.inf)
        m = jnp.max(sm, axis=-1, keepdims=True)
        p = jnp.exp(sm - m)
        den = jnp.sum(p, axis=-1, keepdims=True)
        lse[r, j, h] = m + jnp.log(den)
        pn[r, j, h] = (p / den).astype(BF16)
    for r, j, h in chains:
        oh[r, j, h] = jnp.dot(pn[r, j, h], vv[r][j * nk:(j + 2) * nk], preferred_element_type=F32)
    for r in range(nr):
        for j in range(nblk):
            o_acc = jnp.zeros((nk, B_GW), F32)
            l_acc = jnp.zeros((nk, B_GW), F32)
            for h in range(B_HEADS):
                o_acc = jnp.where(masks[h], oh[r, j, h], o_acc)
                l_acc = jnp.where(masks[h], lse[r, j, h], l_acc)
            o_ref[0, r, j * nk:(j + 1) * nk, :] = o_acc
            l_ref[0, r, j * nk:(j + 1) * nk, :] = l_acc


AP_ROWS = 1024


def _attn_prompt_call(q, kv):
    B, dil, n, _ = q.shape
    tq = min(AP_ROWS, n)
    nr = min(dil, AP_ROWS // tq)
    per = tq // B_NK
    blk = lambda w: pl.BlockSpec((1, nr, tq, w), lambda b, r, i: (b, r, i, 0))
    out = jax.ShapeDtypeStruct((B, dil, n, B_GW), F32)
    return pl.pallas_call(
        _attn_prompt_kernel,
        grid=(B, dil // nr, n // tq),
        in_specs=[blk(B_GW),
                  pl.BlockSpec((1, nr, B_NK, 2 * B_GW), lambda b, r, i: (b, r, jnp.maximum(i * per - 1, 0), 0)),
                  blk(2 * B_GW)],
        out_specs=[blk(B_GW), blk(B_GW)],
        out_shape=[out, out],
        compiler_params=_cparams(("parallel", "parallel", "arbitrary")),
        name="attn_prompt_d%d" % dil,
    )(q, kv, kv)


AS_QB = 16
AS_CACHE_BYTES = 16 * 1024 * 1024
AS_R = 8
AS_NEW = 16


def _attn_sample_kernel(q_ref, kvn_ref, c_ref, o_ref, l_ref, *, dil, g, sub_blocks):
    T = q_ref.shape[0]
    cb, Lb = c_ref.shape[1], c_ref.shape[3]
    R, NR = AS_R, B_HEADS * AS_R
    sub = pl.program_id(0) % sub_blocks
    row = lax.broadcasted_iota(jnp.int32, (NR, B_GW), 0)
    lane = lax.broadcasted_iota(jnp.int32, (NR, B_GW), 1)
    hmask = (lane // B_HD) == (row // R)

    def query_of(shape):
        t = lax.broadcasted_iota(jnp.int32, shape, 0) % R
        return jnp.where(t < T, t, 0)

    pos = lax.broadcasted_iota(jnp.int32, (NR, Lb), 1)
    tn = lax.broadcasted_iota(jnp.int32, (NR, AS_NEW), 1)
    if dil == 1:
        valid_c = pos >= query_of((NR, Lb))
        valid_n = tn <= query_of((NR, AS_NEW))
    else:
        valid_c = (pos % dil) == query_of((NR, Lb))
        valid_n = tn == query_of((NR, AS_NEW))

    for bi in range(cb):
        b = sub * cb + bi
        qrows = jnp.concatenate([q_ref[t, pl.ds(b, 1), g * B_GW:(g + 1) * B_GW] for t in range(T)]
                                + [jnp.zeros((R - T, B_GW), F32)], axis=0)
        qbd = jnp.where(hmask, jnp.concatenate([qrows] * B_HEADS, axis=0), 0.0)
        new = jnp.concatenate([kvn_ref[t, pl.ds(b, 1), :] for t in range(T)]
                              + [jnp.zeros((AS_NEW - T, 2 * B_GW), F32)], axis=0)
        s = jnp.where(valid_c, _dot(qbd, c_ref[0, bi, 0:B_GW, :]), -jnp.inf)
        sn = jnp.where(valid_n, _dot_nt(qbd, new[:, 0:B_GW]), -jnp.inf)
        m = jnp.maximum(jnp.max(s, axis=-1, keepdims=True), jnp.max(sn, axis=-1, keepdims=True))
        p = jnp.exp(s - m)
        pn = jnp.exp(sn - m)
        den = jnp.sum(p, axis=-1, keepdims=True) + jnp.sum(pn, axis=-1, keepdims=True)
        o = _dot_nt(p / den, c_ref[0, bi, B_GW:2 * B_GW, :]) + _dot(pn / den, new[:, B_GW:])
        om = jnp.where(hmask, o, 0.0)
        lm = jnp.where(hmask, m + jnp.log(den), 0.0)
        ob = om[0:R]
        lb = lm[0:R]
        for h in range(1, B_HEADS):
            ob = ob + om[h * R:(h + 1) * R]
            lb = lb + lm[h * R:(h + 1) * R]
        for t in range(T):
            o_ref[t, pl.ds(b, 1), :] = ob[t:t + 1]
            l_ref[t, pl.ds(b, 1), :] = lb[t:t + 1]


def _attn_sample_call(q, kvn, cache_t, layer, g, dil):
    T, B, _ = q.shape
    Lb = cache_t.shape[3]
    qb = min(AS_QB, B)
    cb = max(1, min(qb, AS_CACHE_BYTES // (2 * B_GW * Lb * 4)))
    sub_blocks = qb // cb
    out = jax.ShapeDtypeStruct((T, B, B_GW), F32)
    row_spec = lambda w: pl.BlockSpec((T, qb, w), lambda i: (0, i // sub_blocks, 0))
    return pl.pallas_call(
        functools.partial(_attn_sample_kernel, dil=dil, g=g, sub_blocks=sub_blocks),
        grid=(B // cb,),
        in_specs=[row_spec(B_W), row_spec(2 * B_GW),
                  pl.BlockSpec((1, cb, 2 * B_GW, Lb), lambda i: (layer, i, 0, 0))],
        out_specs=[row_spec(B_GW), row_spec(B_GW)],
        out_shape=[out, out],
        compiler_params=_cparams(("arbitrary",)),
        name="attn_sample_d%d" % dil,
    )(q, kvn, cache_t)


def _shift_rows(x, s, fill):
    rolled = pltpu.roll(x, s, 0)
    r = lax.broadcasted_iota(jnp.int32, x.shape, 0)
    return jnp.where(r < s, fill, rolled)


def _conv_init(xx_ref, prev, hp):
    halo = prev.shape[0]
    if halo < hp:
        xx_ref[0:hp, :] = jnp.zeros((hp, xx_ref.shape[1]), F32)
    xx_ref[hp - halo:hp, :] = prev


def _conv_taps(xx_ref, x, w_ref, b_ref, n_taps, unit, hp, c0, c1):
    tm = x.shape[0]
    halo = (n_taps - 1) * unit
    y = b_ref[:, c0:c1] + x * w_ref[n_taps - 1:n_taps, c0:c1]
    if unit % 8 == 0:
        xx_ref[hp:hp + tm, c0:c1] = x
        for j in range(n_taps - 1):
            y = y + xx_ref[hp - halo + j * unit:hp - halo + j * unit + tm, c0:c1] * w_ref[j:j + 1, c0:c1]
        tail = xx_ref[hp + tm - halo:hp + tm, c0:c1]
        xx_ref[hp - halo:hp, c0:c1] = tail
    else:
        xcat = jnp.concatenate([xx_ref[0:hp, c0:c1], x], axis=0)
        for j in range(n_taps - 1):
            y = y + pltpu.roll(xcat, halo - j * unit, 0)[hp:hp + tm, :] * w_ref[j:j + 1, c0:c1]
        tail = x[tm - halo:tm, :]
        xx_ref[0:hp, c0:c1] = x[tm - hp:tm, :]
    return y, tail


def _rglru_kernel(x_ref, g_ref, prev_ref, h0_ref, cw_ref, cb_ref, wr_ref, br_ref, wi_ref, bi_ref, lam_ref,
                  o_ref, hl_ref, xx_ref, hc_ref, *, unit, hp):
    i = pl.program_id(1)
    tm = x_ref.shape[1]

    @pl.when(i == 0)
    def _():
        _conv_init(xx_ref, prev_ref[0], hp)
        hc_ref[...] = h0_ref[0]

    y, _ = _conv_taps(xx_ref, x_ref[0].astype(F32), cw_ref, cb_ref, C_CONV, unit, hp, 0, C_WIDTH)

    r = _sigmoid(_dot(y, wr_ref[...]) + br_ref[...])
    ig = _sigmoid(_dot(y, wi_ref[...]) + bi_ref[...])
    log_a = -C_POW * r * _softplus(-lam_ref[...])
    a = jnp.exp(log_a)
    th = jnp.tanh(log_a)
    bx = jnp.sqrt(-2.0 * th / (1.0 - th)) * (ig * y)
    hc = hc_ref[...]
    if unit == 1:
        rr = lax.broadcasted_iota(jnp.int32, bx.shape, 0)
        bx = jnp.where(rr < 1, a * hc + bx, bx)
    elif unit == tm:
        bx = a * hc + bx
    else:
        bx = jnp.concatenate([a[0:unit] * hc + bx[0:unit], bx[unit:]], axis=0)
    s = unit
    while s < tm:
        if s % 8 == 0:
            bx = jnp.concatenate([bx[0:s], a[s:] * bx[0:tm - s] + bx[s:]], axis=0)
            a = jnp.concatenate([a[0:s], a[s:] * a[0:tm - s]], axis=0)
        else:
            a_sh = _shift_rows(a, s, 1.0)
            b_sh = _shift_rows(bx, s, 0.0)
            bx = a * b_sh + bx
            a = a * a_sh
        s *= 2
    h = bx
    hl = h[tm - unit:tm, :]
    hc_ref[...] = hl
    hl_ref[0] = hl
    o_ref[0] = h * jax.nn.gelu(g_ref[0].astype(F32))


def _rglru_call(proj3, prev, h0, cw, cb, wr, br, wi, bi, lam, unit, tm):
    nseq, rows, _ = proj3.shape
    halo = (C_CONV - 1) * unit
    hp = -(-halo // 8) * 8
    assert unit == 1 or unit % 8 == 0
    vec = lambda: pl.BlockSpec((1, C_WIDTH), lambda n, i: (0, 0))
    return pl.pallas_call(
        functools.partial(_rglru_kernel, unit=unit, hp=hp),
        grid=(nseq, rows // tm),
        in_specs=[pl.BlockSpec((1, tm, C_WIDTH), lambda n, i: (n, i, OFF_XC // C_WIDTH)),
                  pl.BlockSpec((1, tm, C_WIDTH), lambda n, i: (n, i, OFF_GC // C_WIDTH)),
                  pl.BlockSpec((1, halo, C_WIDTH), lambda n, i: (n, 0, 0)),
                  pl.BlockSpec((1, unit, C_WIDTH), lambda n, i: (n, 0, 0)),
                  pl.BlockSpec((C_CONV, C_WIDTH), lambda n, i: (0, 0)), vec(),
                  pl.BlockSpec((C_WIDTH, C_WIDTH), lambda n, i: (0, 0)), vec(),
                  pl.BlockSpec((C_WIDTH, C_WIDTH), lambda n, i: (0, 0)), vec(), vec()],
        out_specs=[pl.BlockSpec((1, tm, C_WIDTH), lambda n, i: (n, i, 0)),
                   pl.BlockSpec((1, unit, C_WIDTH), lambda n, i: (n, 0, 0))],
        out_shape=[jax.ShapeDtypeStruct((nseq, rows, C_WIDTH), F32),
                   jax.ShapeDtypeStruct((nseq, unit, C_WIDTH), F32)],
        scratch_shapes=[pltpu.VMEM((hp + tm, C_WIDTH), F32), pltpu.VMEM((unit, C_WIDTH), F32)],
        compiler_params=_cparams(("parallel", "arbitrary")),
        name="rglru_u%d" % unit,
    )(proj3, proj3, prev, h0, cw, cb, wr, br, wi, bi, lam)


def _merge_kernel(x_ref, ga_ref, gb_ref, gc_ref, oa_ref, oc_ref,
                  o0_ref, o1_ref, o2_ref, l0_ref, l1_ref, l2_ref,
                  wpa_ref, wpb_ref, wpc_ref, wo_ref, g_ref, b_ref, out_ref, il_ref, *, dils):
    tm = x_ref.shape[1]

    def rows_in_order(ref, dil):
        if dil == 1:
            return ref[0, 0]
        for r in range(dil):
            for j in range(il_ref.shape[0]):
                il_ref[j, pl.ds(r, tm // dil, stride=dil), :] = ref[0, r, :, j * 128:(j + 1) * 128]
        return jnp.concatenate([il_ref[j] for j in range(il_ref.shape[0])], axis=1)

    o0, o1, o2 = (rows_in_order(r, d) for r, d in zip((o0_ref, o1_ref, o2_ref), dils))
    l0, l1, l2 = (rows_in_order(r, d) for r, d in zip((l0_ref, l1_ref, l2_ref), dils))
    m = jnp.maximum(jnp.maximum(l0, l1), l2)
    e0, e1, e2 = jnp.exp(l0 - m), jnp.exp(l1 - m), jnp.exp(l2 - m)
    ob = (e0 * o0 + e1 * o1 + e2 * o2) / (e0 + e1 + e2)
    merged = (_sigmoid(ga_ref[0].astype(F32)) * _dot(oa_ref[0], wpa_ref[...])
              + _sigmoid(gb_ref[0].astype(F32)) * _dot(ob, wpb_ref[...])
              + _sigmoid(gc_ref[0].astype(F32)) * _dot(oc_ref[0], wpc_ref[...]))
    mix = _dot(merged, wo_ref[...])
    out_ref[0] = _layer_norm(DN_ALPHA * x_ref[0] + mix, g_ref[...], b_ref[...])


def _merge_call(x3, proj3, oa, oc, obs, lses, wpa, wpb, wpc, wo, g, b, tm):
    nseq, rows, _ = x3.shape
    dils = tuple(o.shape[1] for o in obs)
    row_spec = lambda w, j=0: pl.BlockSpec((1, tm, w), lambda n, i: (n, i, j))
    grp_spec = lambda d: pl.BlockSpec((1, d, tm // d, B_GW), lambda n, i: (n, 0, i, 0))
    full = lambda a: pl.BlockSpec(a.shape, lambda n, i: (0, 0))
    return pl.pallas_call(
        functools.partial(_merge_kernel, dils=dils),
        grid=(nseq, rows // tm),
        in_specs=[row_spec(D_MODEL)] + [row_spec(D_MODEL, OFF_GATES // D_MODEL + j) for j in range(3)]
                 + [row_spec(A_VW), row_spec(C_WIDTH)]
                 + [grp_spec(d) for d in dils] * 2
                 + [full(wpa), full(wpb), full(wpc), full(wo), full(g), full(b)],
        out_specs=row_spec(D_MODEL),
        out_shape=jax.ShapeDtypeStruct((nseq, rows, D_MODEL), F32),
        scratch_shapes=[pltpu.VMEM((B_GW // 128, tm, 128), F32)],
        compiler_params=_cparams(("parallel", "parallel")),
        name="merge_ln",
    )(x3, proj3, proj3, proj3, oa, oc, *obs, *lses, wpa, wpb, wpc, wo, g, b)


def _ffn_kernel(x_ref, prev_ref, wu_ref, cw_ref, cb_ref, wd_ref, g_ref, b_ref,
                o_ref, st_ref, xx_ref, *, unit, hp):
    i = pl.program_id(1)
    tm = x_ref.shape[1]

    @pl.when(i == 0)
    def _():
        _conv_init(xx_ref, prev_ref[0], hp)

    x = x_ref[0]
    xb = x.astype(BF16)
    f = jnp.zeros((tm, D_MODEL), F32)
    for c0 in range(0, D_FF, FF_CHUNK):
        c1 = c0 + FF_CHUNK
        gate = jnp.dot(xb, wu_ref[:, c0:c1], preferred_element_type=F32)
        up = jnp.dot(xb, wu_ref[:, D_FF + c0:D_FF + c1], preferred_element_type=F32)
        y, tail = _conv_taps(xx_ref, gate, cw_ref, cb_ref, F_CONV, unit, hp, c0, c1)
        st_ref[0, :, c0:c1] = tail
        f = f + _dot(_silu(y) * up, wd_ref[c0:c1, :])
    o_ref[0] = _layer_norm(DN_ALPHA * x + f, g_ref[...], b_ref[...])


def _ffn_call(x3, prev, wu, cw, cb, wd, g, b, unit, tm):
    nseq, rows, _ = x3.shape
    halo = (F_CONV - 1) * unit
    hp = -(-halo // 8) * 8
    const = lambda a: pl.BlockSpec(a.shape, lambda n, i: (0, 0), pipeline_mode=pl.Buffered(1))
    return pl.pallas_call(
        functools.partial(_ffn_kernel, unit=unit, hp=hp),
        grid=(nseq, rows // tm),
        in_specs=[pl.BlockSpec((1, tm, D_MODEL), lambda n, i: (n, i, 0)),
                  pl.BlockSpec((1, halo, D_FF), lambda n, i: (n, 0, 0)),
                  const(wu), const(cw), const(cb), const(wd), const(g), const(b)],
        out_specs=[pl.BlockSpec((1, tm, D_MODEL), lambda n, i: (n, i, 0)),
                   pl.BlockSpec((1, halo, D_FF), lambda n, i: (n, 0, 0))],
        out_shape=[jax.ShapeDtypeStruct((nseq, rows, D_MODEL), F32),
                   jax.ShapeDtypeStruct((nseq, halo, D_FF), F32)],
        scratch_shapes=[pltpu.VMEM((hp + tm, D_FF), F32)],
        compiler_params=_cparams(("parallel", "arbitrary")),
        name="conv_ffn_u%d" % unit,
    )(x3, prev, wu, cw, cb, wd, g, b)


def _block_diag(w):
    nblk, bw, _ = w.shape
    eye = jnp.eye(nblk, dtype=w.dtype)
    return (eye[:, None, :, None] * w[:, :, None, :]).reshape(nblk * bw, nblk * bw)


def _layer_params(l, w_in, a_conv_w, a_A_log, a_dt_bias, a_norm_w, c_conv_w, c_conv_b, c_w_r, c_b_r,
                  c_w_i, c_b_i, c_lam, w_pa, w_pb, w_pc, w_o, ln1_g, ln1_b, f_up, f_conv_w, f_conv_b,
                  f_down, ln2_g, ln2_b):
    w = w_in[l]
    o_b = A_QKV
    o_a = o_b + A_HEADS
    o_z = o_a + A_HEADS
    o_qb = o_z + A_VW
    o_xc = o_qb + 3 * B_W
    o_gc = o_xc + C_WIDTH
    o_gt = o_gc + C_WIDTH
    pad = jnp.zeros((D_MODEL, N_PROJ - OFF_BA - 2 * A_HEADS), w.dtype)
    wp = jnp.concatenate([w[:, o_gt:o_gt + 3 * D_MODEL], w[:, o_xc:o_gc], w[:, o_gc:o_gt], w[:, o_z:o_qb],
                          w[:, 0:o_b], w[:, o_qb:o_xc], w[:, o_b:o_z], pad], axis=1).astype(BF16)
    gp = jnp.zeros((2, 128), F32)
    gp = gp.at[0, A_HEADS:2 * A_HEADS].set(a_A_log[l]).at[1, A_HEADS:2 * A_HEADS].set(a_dt_bias[l])
    row = lambda v: v.reshape(1, -1)
    return dict(
        wp=wp, a_cw=a_conv_w[l], gp=gp, a_nw=row(a_norm_w[l]),
        c_cw=c_conv_w[l], c_cb=row(c_conv_b[l]),
        c_wr=_block_diag(c_w_r[l]).astype(BF16), c_br=row(c_b_r[l]),
        c_wi=_block_diag(c_w_i[l]).astype(BF16), c_bi=row(c_b_i[l]), c_lam=row(c_lam[l]),
        wpa=w_pa[l].astype(BF16), wpb=w_pb[l].astype(BF16), wpc=w_pc[l].astype(BF16), wo=w_o[l].astype(BF16),
        ln1_g=row(ln1_g[l]), ln1_b=row(ln1_b[l]),
        f_up=f_up[l].astype(BF16), f_cw=f_conv_w[l], f_cb=row(f_conv_b[l]), f_down=f_down[l].astype(BF16),
        ln2_g=row(ln2_g[l]), ln2_b=row(ln2_b[l]))


def _layer_tail(x3, proj3, o_a, obs, lses, unit, p, c_prev, c_h0, f_prev, tm_merge, tm_seq, tm_ffn):
    o_c, c_hl = _rglru_call(proj3, c_prev, c_h0, p["c_cw"], p["c_cb"], p["c_wr"], p["c_br"], p["c_wi"],
                            p["c_bi"], p["c_lam"], unit, tm_seq)
    x1 = _merge_call(x3, proj3, o_a, o_c, obs, lses, p["wpa"], p["wpb"], p["wpc"], p["wo"],
                     p["ln1_g"], p["ln1_b"], tm_merge)
    x_out, f_state = _ffn_call(x1, f_prev, p["f_up"], p["f_cw"], p["f_cb"], p["f_down"],
                               p["ln2_g"], p["ln2_b"], unit, tm_ffn)
    return x_out, c_hl, f_state


def kernel(x_prompt, x_sample, state_a_conv, state_a_rec, cache_b_w128, cache_b_w512, cache_b_w2048,
           state_c_conv, state_c_h, state_f_conv, ln_in_g, ln_in_b, w_in, a_conv_w, a_A_log, a_dt_bias,
           a_norm_w, c_conv_w, c_conv_b, c_w_r, c_b_r, c_w_i, c_b_i, c_lam, w_pa, w_pb, w_pc, w_o,
           ln1_g, ln1_b, f_up, f_conv_w, f_conv_b, f_down, ln2_g, ln2_b):
    Bp, Sp, _ = x_prompt.shape
    Bs, Ts, _ = x_sample.shape
    caches = (cache_b_w128, cache_b_w512, cache_b_w2048)
    for (win, dil), cache in zip(B_GROUPS, caches):
        assert win // dil == B_NK and cache.shape[2] == win and Sp % (dil * B_NK) == 0

    caches_t = [jnp.transpose(c, (0, 1, 3, 4, 5, 2)).reshape(DEPTH, Bs, 2 * B_GW, c.shape[2]) for c in caches]
    Rs = Ts * Bs

    lng, lnb = ln_in_g.reshape(1, -1), ln_in_b.reshape(1, -1)
    hp = _ln_call(x_prompt.reshape(Bp * Sp, D_MODEL), lng, lnb, 512)
    xs_tm = jnp.swapaxes(x_sample, 0, 1).reshape(Rs, D_MODEL)
    hs = _ln_call(xs_tm, lng, lnb, Rs)

    tabs_p = _rope_tables(jnp.arange(Sp))
    tabs_s = _rope_tables(jnp.repeat(PAST_LEN + jnp.arange(Ts), Bs))

    zeros_c = jnp.zeros((Bp, C_CONV - 1, C_WIDTH), F32)
    zeros_h = jnp.zeros((Bp, 1, C_WIDTH), F32)
    zeros_f = jnp.zeros((Bp, F_CONV - 1, D_FF), F32)

    outs_p, outs_s = [], []
    for l in range(DEPTH):
        p = _layer_params(l, w_in, a_conv_w, a_A_log, a_dt_bias, a_norm_w, c_conv_w, c_conv_b, c_w_r, c_b_r,
                          c_w_i, c_b_i, c_lam, w_pa, w_pb, w_pc, w_o, ln1_g, ln1_b, f_up, f_conv_w,
                          f_conv_b, f_down, ln2_g, ln2_b)

        proj3 = _proj_call(hp, p["wp"], 1024, BF16).reshape(Bp, Sp, N_PROJ)
        hp_tail = hp.reshape(Bp, Sp, D_MODEL)[:, Sp - TAIL_ROWS:].reshape(Bp * TAIL_ROWS, D_MODEL)
        proj_tail = _proj_call(hp_tail, p["wp"], Bp * TAIL_ROWS, F32).reshape(Bp, TAIL_ROWS, N_PROJ)
        o_a, a_rec = _gdn_prompt_call(proj3, p["a_cw"], p["gp"], p["a_nw"])
        rp = _rope_prompt_call(proj3, tabs_p, 512)
        res = [_attn_prompt_call(rp[g], rp[3 + g]) for g in range(len(B_GROUPS))]
        hp3, c_hl, f_st = _layer_tail(hp.reshape(Bp, Sp, D_MODEL), proj3, o_a, [r[0] for r in res],
                                      [r[1] for r in res], 1, p, zeros_c, zeros_h, zeros_f, 512, 256, 512)
        hp = hp3.reshape(Bp * Sp, D_MODEL)
        kv_rows_p = [t.reshape(Bp, 2, B_HEADS, B_HD, t.shape[-1]).transpose(0, 4, 1, 2, 3) for t in rp[6:9]]
        outs_p.append((
            proj_tail[:, TAIL_ROWS - (A_CONV - 1):, OFF_QKVA:OFF_QKVA + A_QKV],
            a_rec, kv_rows_p[0], kv_rows_p[1], kv_rows_p[2],
            proj_tail[:, TAIL_ROWS - (C_CONV - 1):, OFF_XC:OFF_XC + C_WIDTH],
            c_hl[:, 0],
            f_st))

        proj3 = _proj_call(hs, p["wp"], Rs, F32).reshape(1, Rs, N_PROJ)
        pt = proj3.reshape(Ts, Bs, N_PROJ)
        a_st = jnp.swapaxes(state_a_conv[l], 0, 1)
        o_a, a_rec = _gdn_sample_call(pt, a_st, state_a_rec, l, p["a_cw"], p["gp"], p["a_nw"])
        rs = _rope_sample_call(proj3, tabs_s, Bs)
        q3 = rs[0].reshape(Ts, Bs, B_W)
        res = [_attn_sample_call(q3, rs[1 + g].reshape(Ts, Bs, 2 * B_GW), caches_t[g], l, g, dil)
               for g, (_, dil) in enumerate(B_GROUPS)]
        c_prev = jnp.swapaxes(state_c_conv[l], 0, 1).reshape(1, (C_CONV - 1) * Bs, C_WIDTH)
        f_prev = jnp.swapaxes(state_f_conv[l], 0, 1).reshape(1, (F_CONV - 1) * Bs, D_FF)
        hs3, c_hl, f_st = _layer_tail(hs.reshape(1, Rs, D_MODEL), proj3, o_a.reshape(1, Rs, A_VW),
                                      [r[0].reshape(1, 1, Rs, B_GW) for r in res],
                                      [r[1].reshape(1, 1, Rs, B_GW) for r in res],
                                      Bs, p, c_prev, state_c_h[l][None], f_prev, min(256, Rs), Rs, Rs)
        hs = hs3.reshape(Rs, D_MODEL)
        tm2bm = lambda a: jnp.swapaxes(a, 0, 1)
        kv_rows_s = [t.reshape(Ts, 2, B_HEADS, B_HD, Bs).transpose(4, 0, 1, 2, 3) for t in rs[4:7]]
        outs_s.append((
            tm2bm(pt[Ts - (A_CONV - 1):, :, OFF_QKVA:OFF_QKVA + A_QKV]),
            a_rec, kv_rows_s[0], kv_rows_s[1], kv_rows_s[2],
            tm2bm(pt[Ts - (C_CONV - 1):, :, OFF_XC:OFF_XC + C_WIDTH]),
            c_hl[0],
            tm2bm(f_st.reshape(F_CONV - 1, Bs, D_FF))))

    stack = lambda outs: [jnp.stack(s, 0) for s in zip(*outs)]
    y_p = hp.reshape(Bp, Sp, D_MODEL)
    y_s = jnp.swapaxes(hs.reshape(Ts, Bs, D_MODEL), 0, 1)
    return (y_p, y_s, *stack(outs_p), *stack(outs_s))
```

```python
import functools

import jax
import jax.numpy as jnp
from jax import lax
from jax.experimental import pallas as pl
from jax.experimental.pallas import tpu as pltpu

F32 = jnp.float32
BF16 = jnp.bfloat16

D_MODEL = 1024
DEPTH = 4
PAST_LEN = 2048
A_HEADS, A_DK, A_DV, A_CONV, A_CHUNK = 4, 128, 128, 4, 64
B_GROUPS = ((128, 1), (512, 4), (2048, 16))
B_HEADS, B_HD = 4, 64
B_ROT = B_HD // 4
ROPE_THETA = 500000.0
B_NK = 128
C_WIDTH, C_CONV, C_POW = 512, 4, 8.0
D_FF = ((8 * D_MODEL // 3 + 255) // 256) * 256
F_CONV = 3
FF_CHUNK = D_FF
DN_ALPHA = (2 * DEPTH) ** 0.25
LN_EPS = 1e-5
NORM_EPS = 1e-6

A_QK = A_HEADS * A_DK
A_VW = A_HEADS * A_DV
A_QKV = 2 * A_QK + A_VW
B_GW = B_HEADS * B_HD
B_W = len(B_GROUPS) * B_GW

OFF_GATES = 0
OFF_XC = 3072
OFF_GC = 3584
OFF_Z = 4096
OFF_QKVA = 4608
OFF_QB = 6144
OFF_BA = 8448
N_PROJ = 8704
PROJ_TN = 2176
TAIL_ROWS = 8

VMEM_LIMIT = 56 * 1024 * 1024


def _cparams(sem, **kw):
    return pltpu.CompilerParams(dimension_semantics=sem, vmem_limit_bytes=VMEM_LIMIT, **kw)


def _sigmoid(x):
    return 0.5 + 0.5 * jnp.tanh(0.5 * x)


def _silu(x):
    h = 0.5 * x
    return h + h * jnp.tanh(h)


def _softplus(x):
    return jnp.maximum(x, 0.0) + jnp.log1p(jnp.exp(-jnp.abs(x)))


def _layer_norm(x, g, b):
    mu = jnp.mean(x, axis=-1, keepdims=True)
    xc = x - mu
    var = jnp.mean(xc * xc, axis=-1, keepdims=True)
    return xc * lax.rsqrt(var + LN_EPS) * g + b


def _dot(a, b):
    return jnp.dot(a.astype(BF16), b.astype(BF16), preferred_element_type=F32)


def _dot_nt(a, b):
    return lax.dot_general(a.astype(BF16), b.astype(BF16), (((1,), (1,)), ((), ())),
                           preferred_element_type=F32)


def _dot_tn(a, b):
    return lax.dot_general(a.astype(BF16), b.astype(BF16), (((0,), (0,)), ((), ())),
                           preferred_element_type=F32)


def _split2(a):
    hi = a.astype(BF16)
    lo = (a - hi.astype(F32)).astype(BF16)
    return hi, lo


def _dot3(a, b):
    ah, al = _split2(a)
    bh, bl = _split2(b)
    d = lambda x, y: jnp.dot(x, y, preferred_element_type=F32)
    return d(ah, bh) + (d(ah, bl) + d(al, bh))


def _ln_kernel(x_ref, g_ref, b_ref, o_ref):
    o_ref[...] = _layer_norm(x_ref[...], g_ref[...], b_ref[...])


def _ln_call(x, g, b, tm):
    rows = x.shape[0]
    return pl.pallas_call(
        _ln_kernel,
        grid=(rows // tm,),
        in_specs=[pl.BlockSpec((tm, D_MODEL), lambda i: (i, 0)),
                  pl.BlockSpec((1, D_MODEL), lambda i: (0, 0)),
                  pl.BlockSpec((1, D_MODEL), lambda i: (0, 0))],
        out_specs=pl.BlockSpec((tm, D_MODEL), lambda i: (i, 0)),
        out_shape=jax.ShapeDtypeStruct((rows, D_MODEL), F32),
        compiler_params=_cparams(("parallel",)),
        name="entry_ln",
    )(x, g, b)


def _proj_kernel(x_ref, w_ref, o_ref, xb_ref):
    @pl.when(pl.program_id(1) == 0)
    def _():
        xb_ref[...] = x_ref[...].astype(BF16)

    o_ref[...] = jnp.dot(xb_ref[...], w_ref[...], preferred_element_type=F32).astype(o_ref.dtype)


def _proj_call(x, w, tm, out_dtype):
    rows = x.shape[0]
    return pl.pallas_call(
        _proj_kernel,
        grid=(rows // tm, N_PROJ // PROJ_TN),
        in_specs=[pl.BlockSpec((tm, D_MODEL), lambda i, j: (i, 0)),
                  pl.BlockSpec((D_MODEL, PROJ_TN), lambda i, j: (0, j))],
        out_specs=pl.BlockSpec((tm, PROJ_TN), lambda i, j: (i, j)),
        out_shape=jax.ShapeDtypeStruct((rows, N_PROJ), out_dtype),
        scratch_shapes=[pltpu.VMEM((tm, D_MODEL), BF16)],
        compiler_params=_cparams(("parallel", "arbitrary")),
        name="in_proj",
    )(x, w)


def _gdn_prompt_kernel(x_ref, z_ref, ba_ref, cw_ref, gp_ref, nw_ref,
                       o_ref, sfin_ref, xx_ref, s_ref):
    c = pl.program_id(0)
    C = A_CHUNK
    nb = x_ref.shape[0]

    @pl.when(c == 0)
    def _():
        xx_ref[...] = jnp.zeros((nb, 8, A_QKV), F32)
        s_ref[...] = jnp.zeros(s_ref.shape, F32)

    ri = lax.broadcasted_iota(jnp.int32, (C, C), 0)
    ci = lax.broadcasted_iota(jnp.int32, (C, C), 1)
    tril = ri >= ci
    strict = ri > ci
    eye = (ri == ci).astype(F32)
    trilf = tril.astype(F32)
    nw = nw_ref[...]

    ch = []
    for b in range(nb):
        x = x_ref[b].astype(F32)
        xcat = jnp.concatenate([xx_ref[b], x], axis=0)
        y = pltpu.roll(xcat, A_CONV - 1, 0)[8:8 + C, :] * cw_ref[0:1, :]
        for j in range(1, A_CONV - 1):
            y = y + pltpu.roll(xcat, A_CONV - 1 - j, 0)[8:8 + C, :] * cw_ref[j:j + 1, :]
        y = y + x * cw_ref[A_CONV - 1:A_CONV, :]
        xx_ref[b] = x[C - 8:C, :]
        y = _silu(y)
        ba = ba_ref[b].astype(F32)
        beta_all = _sigmoid(ba)
        g_all = -jnp.exp(gp_ref[0:1, :]) * _softplus(ba + gp_ref[1:2, :])
        gc = _dot3(trilf, g_all)
        gct = jnp.concatenate([gc, jnp.zeros((128 - C, 128), F32)], axis=0).T
        for h in range(A_HEADS):
            q = y[:, h * A_DK:(h + 1) * A_DK]
            k = y[:, A_QK + h * A_DK:A_QK + (h + 1) * A_DK]
            v = y[:, 2 * A_QK + h * A_DV:2 * A_QK + (h + 1) * A_DV]
            q = q * lax.rsqrt(jnp.sum(q * q, axis=-1, keepdims=True) + NORM_EPS) * (A_DK ** -0.5)
            k = k * lax.rsqrt(jnp.sum(k * k, axis=-1, keepdims=True) + NORM_EPS)
            beta = beta_all[:, h:h + 1]
            gcol = gc[:, 4 + h:5 + h]
            grow = gct[4 + h:5 + h, 0:C]
            decay = jnp.where(tril, jnp.exp(jnp.where(tril, gcol - grow, 0.0)), 0.0)
            eg = jnp.exp(gcol)
            glast = gcol[C - 1:C, :]
            ch.append(dict(b=b, h=h, q=q, k=k, kb=k * beta, vb=v * beta, decay=decay, eg=eg,
                           k_dec=k * jnp.exp(glast - gcol), q_dec=q * eg, eglast=jnp.exp(glast)))
    for d in ch:
        d["m"] = -jnp.where(strict, _dot_nt(d["kb"], d["k"]) * d["decay"], 0.0)
        d["t"] = eye + d["m"]
    for d in ch:
        d["qk"] = jnp.where(tril, _dot_nt(d["q"], d["k"]) * d["decay"], 0.0)
    for _ in range(5):
        for d in ch:
            d["m"] = _dot(d["m"], d["m"])
        for d in ch:
            d["t"] = d["t"] + _dot(d["t"], d["m"])
    for d in ch:
        d["u"] = _dot(d["t"], d["vb"])
        d["w"] = _dot(d["t"], d["kb"] * d["eg"])
    for d in ch:
        d["s"] = s_ref[d["b"], d["h"]]
        d["v_new"] = d["u"] - _dot(d["w"], d["s"])
    for d in ch:
        d["o"] = _dot(d["q_dec"], d["s"]) + _dot(d["qk"], d["v_new"])
        s_ref[d["b"], d["h"]] = d["s"] * d["eglast"] + _dot_tn(d["k_dec"], d["v_new"])
    for d in ch:
        b, h, o = d["b"], d["h"], d["o"]
        o = o * lax.rsqrt(jnp.mean(o * o, axis=-1, keepdims=True) + NORM_EPS) * nw
        o_ref[b, :, h * A_DV:(h + 1) * A_DV] = o * _silu(z_ref[b, :, h * A_DV:(h + 1) * A_DV].astype(F32))

    @pl.when(c == pl.num_programs(0) - 1)
    def _():
        sfin_ref[...] = s_ref[...]


def _gdn_prompt_call(proj3, cw, gp, nw):
    B, S, _ = proj3.shape
    C = A_CHUNK
    return pl.pallas_call(
        _gdn_prompt_kernel,
        grid=(S // C,),
        in_specs=[pl.BlockSpec((B, C, A_QKV), lambda c: (0, c, OFF_QKVA // A_QKV)),
                  pl.BlockSpec((B, C, A_VW), lambda c: (0, c, OFF_Z // A_VW)),
                  pl.BlockSpec((B, C, 128), lambda c: (0, c, OFF_BA // 128)),
                  pl.BlockSpec((A_CONV, A_QKV), lambda c: (0, 0)),
                  pl.BlockSpec((2, 128), lambda c: (0, 0)),
                  pl.BlockSpec((1, A_DV), lambda c: (0, 0))],
        out_specs=[pl.BlockSpec((B, C, A_VW), lambda c: (0, c, 0)),
                   pl.BlockSpec((B, A_HEADS, A_DK, A_DV), lambda c: (0, 0, 0, 0))],
        out_shape=[jax.ShapeDtypeStruct((B, S, A_VW), F32),
                   jax.ShapeDtypeStruct((B, A_HEADS, A_DK, A_DV), F32)],
        scratch_shapes=[pltpu.VMEM((B, 8, A_QKV), F32),
                        pltpu.VMEM((B, A_HEADS, A_DK, A_DV), F32)],
        compiler_params=_cparams(("arbitrary",)),
        name="gdn_prompt",
    )(proj3, proj3, proj3, cw, gp, nw)


GS_BB = 16
GS_C = 16


def _gdn_sample_kernel(x_ref, z_ref, ba_ref, st_ref, s0_ref, cw_ref, gp_ref, nw_ref,
                       o_ref, snew_ref, oscr_ref):
    T = x_ref.shape[0]
    bb = GS_BB
    xx = [st_ref[j] for j in range(A_CONV - 1)] + [x_ref[t].astype(F32) for t in range(T)]
    C = GS_C
    assert T <= C
    qs, ks, kbs, vbs, gcs = [], [], [], [], []
    for t in range(T):
        y = xx[t] * cw_ref[0:1, :]
        for j in range(1, A_CONV):
            y = y + xx[t + j] * cw_ref[j:j + 1, :]
        y = _silu(y)
        ba = ba_ref[t].astype(F32)
        beta_all = _sigmoid(ba)
        g_all = -jnp.exp(gp_ref[0:1, :]) * _softplus(ba + gp_ref[1:2, :])
        gcs.append(g_all if t == 0 else gcs[-1] + g_all)
        qs.append([]), ks.append([]), kbs.append([]), vbs.append([])
        for h in range(A_HEADS):
            q = y[:, h * A_DK:(h + 1) * A_DK]
            k = y[:, A_QK + h * A_DK:A_QK + (h + 1) * A_DK]
            v = y[:, 2 * A_QK + h * A_DV:2 * A_QK + (h + 1) * A_DV]
            q = q * lax.rsqrt(jnp.sum(q * q, axis=-1, keepdims=True) + NORM_EPS) * (A_DK ** -0.5)
            k = k * lax.rsqrt(jnp.sum(k * k, axis=-1, keepdims=True) + NORM_EPS)
            beta = beta_all[:, h:h + 1]
            qs[t].append(q), ks[t].append(k), kbs[t].append(k * beta), vbs[t].append(v * beta)

    ri = lax.broadcasted_iota(jnp.int32, (C, C), 0)
    ci = lax.broadcasted_iota(jnp.int32, (C, C), 1)
    tril = ri >= ci
    strict = ri > ci
    eye = (ri == ci).astype(F32)
    zrows = jnp.zeros((C - T, A_DK), F32)
    rows_of = lambda slabs, b, h: jnp.concatenate([slabs[t][h][b:b + 1, :] for t in range(T)] + [zrows], axis=0)
    ch = []
    for h in range(A_HEADS):
        for b in range(bb):
            gcol = jnp.concatenate([gcs[t][b:b + 1, 4 + h:5 + h] for t in range(T)]
                                   + [gcs[T - 1][b:b + 1, 4 + h:5 + h]] * (C - T), axis=0)
            grow = jnp.sum(eye * gcol, axis=0, keepdims=True)
            decay = jnp.where(tril, jnp.exp(jnp.where(tril, gcol - grow, 0.0)), 0.0)
            eg = jnp.exp(gcol)
            glast = gcol[C - 1:C, :]
            q, k, kb, vb = (rows_of(s, b, h) for s in (qs, ks, kbs, vbs))
            ch.append(dict(b=b, h=h, q=q, k=k, kb=kb, vb=vb, decay=decay, eg=eg,
                           k_dec=k * jnp.exp(glast - gcol), q_dec=q * eg, eglast=jnp.exp(glast)))
    for d in ch:
        d["m"] = -jnp.where(strict, _dot_nt(d["kb"], d["k"]) * d["decay"], 0.0)
        d["qk"] = jnp.where(tril, _dot_nt(d["q"], d["k"]) * d["decay"], 0.0)
    for d in ch:
        d["t"] = eye + d["m"]
    for _ in range(GS_C.bit_length() - 2):
        for d in ch:
            d["m"] = _dot(d["m"], d["m"])
        for d in ch:
            d["t"] = d["t"] + _dot(d["t"], d["m"])
    for d in ch:
        d["u"] = _dot(d["t"], d["vb"])
        d["w"] = _dot(d["t"], d["kb"] * d["eg"])
    for d in ch:
        d["s"] = s0_ref[0, d["b"], d["h"]]
        ws = _dot(jnp.concatenate([d["w"], d["q_dec"]], axis=0), d["s"])
        d["v_new"] = d["u"] - ws[0:C]
        d["qs"] = ws[C:2 * C]
    for d in ch:
        b, h = d["b"], d["h"]
        o = d["qs"] + _dot(d["qk"], d["v_new"])
        snew_ref[b, h] = d["s"] * d["eglast"] + _dot_tn(d["k_dec"], d["v_new"])
        for t in range(T):
            oscr_ref[t, b:b + 1, h * A_DV:(h + 1) * A_DV] = o[t:t + 1, :]
    nw = nw_ref[...]
    for t in range(T):
        z = z_ref[t].astype(F32)
        for h in range(A_HEADS):
            o = oscr_ref[t, :, h * A_DV:(h + 1) * A_DV]
            o = o * lax.rsqrt(jnp.mean(o * o, axis=-1, keepdims=True) + NORM_EPS) * nw
            o_ref[t, :, h * A_DV:(h + 1) * A_DV] = o * _silu(z[:, h * A_DV:(h + 1) * A_DV])


def _gdn_sample_call(proj3, st, s0, layer, cw, gp, nw):
    T, B, _ = proj3.shape
    bb = GS_BB
    return pl.pallas_call(
        _gdn_sample_kernel,
        grid=(B // bb,),
        in_specs=[pl.BlockSpec((T, bb, A_QKV), lambda i: (0, i, OFF_QKVA // A_QKV)),
                  pl.BlockSpec((T, bb, A_VW), lambda i: (0, i, OFF_Z // A_VW)),
                  pl.BlockSpec((T, bb, 128), lambda i: (0, i, OFF_BA // 128)),
                  pl.BlockSpec((A_CONV - 1, bb, A_QKV), lambda i: (0, i, 0)),
                  pl.BlockSpec((1, bb, A_HEADS, A_DK, A_DV), lambda i: (layer, i, 0, 0, 0)),
                  pl.BlockSpec((A_CONV, A_QKV), lambda i: (0, 0)),
                  pl.BlockSpec((2, 128), lambda i: (0, 0)),
                  pl.BlockSpec((1, A_DV), lambda i: (0, 0))],
        out_specs=[pl.BlockSpec((T, bb, A_VW), lambda i: (0, i, 0)),
                   pl.BlockSpec((bb, A_HEADS, A_DK, A_DV), lambda i: (i, 0, 0, 0))],
        out_shape=[jax.ShapeDtypeStruct((T, B, A_VW), F32),
                   jax.ShapeDtypeStruct(s0.shape[1:], F32)],
        scratch_shapes=[pltpu.VMEM((T, bb, A_VW), F32)],
        compiler_params=_cparams(("parallel",)),
        name="gdn_sample",
    )(proj3, proj3, proj3, st, s0, cw, gp, nw)


def _rope_qkv(q_ref, k_ref, v_ref, c_ref, s1_ref, s2_ref):
    reps = B_W // 128
    c = jnp.concatenate([c_ref[0]] * reps, axis=1)
    s1 = jnp.concatenate([s1_ref[0]] * reps, axis=1)
    s2 = jnp.concatenate([s2_ref[0]] * reps, axis=1)

    def rot(x):
        return x * c + pltpu.roll(x, B_W - B_ROT // 2, 1) * s1 + pltpu.roll(x, B_ROT // 2, 1) * s2

    return rot(q_ref[0].astype(F32)) * (B_HD ** -0.5), rot(k_ref[0].astype(F32)), v_ref[0].astype(F32)


def _rope_in_specs(tm):
    qb = OFF_QB // B_W
    tab_spec = pl.BlockSpec((1, tm, 128), lambda n, i: (0, i, 0))
    return [pl.BlockSpec((1, tm, B_W), lambda n, i: (n, i, qb)),
            pl.BlockSpec((1, tm, B_W), lambda n, i: (n, i, qb + 1)),
            pl.BlockSpec((1, tm, B_W), lambda n, i: (n, i, qb + 2)),
            tab_spec, tab_spec, tab_spec]


def _store_chunks(buf_ref, x):
    for j in range(buf_ref.shape[0]):
        buf_ref[j] = x[:, j * 128:(j + 1) * 128]


def _load_chunks_strided(buf_ref, start, size, stride):
    return jnp.concatenate([buf_ref[j, pl.ds(start, size, stride=stride), :] for j in range(buf_ref.shape[0])],
                           axis=1)


def _rope_prompt_kernel(q_ref, k_ref, v_ref, c_ref, s1_ref, s2_ref,
                        q0_ref, q1_ref, q2_ref, kv0_ref, kv1_ref, kv2_ref, t0_ref, t1_ref, t2_ref,
                        qs_ref, kvs_ref, *, tails):
    i = pl.program_id(1)
    tm = q_ref.shape[1]
    q, k, v = _rope_qkv(q_ref, k_ref, v_ref, c_ref, s1_ref, s2_ref)
    outs = ((q0_ref, kv0_ref, t0_ref), (q1_ref, kv1_ref, t1_ref), (q2_ref, kv2_ref, t2_ref))
    for g, ((_, dil), (qo_ref, kvo_ref, to_ref)) in enumerate(zip(B_GROUPS, outs)):
        lo, hi = g * B_GW, (g + 1) * B_GW
        qg = q[:, lo:hi]
        kv = jnp.concatenate([k[:, lo:hi], v[:, lo:hi]], axis=1)
        if dil == 1:
            qo_ref[0, 0] = qg.astype(BF16)
            kvo_ref[0, 0] = kv
        else:
            _store_chunks(qs_ref, qg)
            _store_chunks(kvs_ref, kv)
            for r in range(dil):
                qo_ref[0, r] = _load_chunks_strided(qs_ref, r, tm // dil, dil).astype(BF16)
                kvo_ref[0, r] = _load_chunks_strided(kvs_ref, r, tm // dil, dil)
        first, w = tails[g]

        @pl.when(i >= first)
        def _(to_ref=to_ref, w=w, kv=kv):
            to_ref[0] = kv[tm - w:tm, :].T


def _rope_prompt_call(proj3, tabs, tm):
    B, S, _ = proj3.shape
    nt = S // tm
    out_shape, out_specs, tails = [], [], []
    for width, dtype in ((B_GW, BF16), (2 * B_GW, F32)):
        for _, dil in B_GROUPS:
            out_shape.append(jax.ShapeDtypeStruct((B, dil, S // dil, width), dtype))
            out_specs.append(pl.BlockSpec((1, dil, tm // dil, width), lambda n, i: (n, 0, i, 0)))
    for win, _ in B_GROUPS:
        win = min(win, S)
        w = min(tm, win)
        first = nt - win // w
        tails.append((first, w))
        out_shape.append(jax.ShapeDtypeStruct((B, 2 * B_GW, win), F32))
        out_specs.append(pl.BlockSpec((1, 2 * B_GW, w),
                                      lambda n, i, first=first: (n, 0, jnp.maximum(i - first, 0))))
    return pl.pallas_call(
        functools.partial(_rope_prompt_kernel, tails=tuple(tails)),
        grid=(B, nt),
        in_specs=_rope_in_specs(tm),
        out_specs=out_specs,
        out_shape=out_shape,
        scratch_shapes=[pltpu.VMEM((B_GW // 128, tm, 128), F32), pltpu.VMEM((2 * B_GW // 128, tm, 128), F32)],
        compiler_params=_cparams(("parallel", "arbitrary")),
        name="rope_prompt",
    )(proj3, proj3, proj3, *tabs)


def _rope_sample_kernel(q_ref, k_ref, v_ref, c_ref, s1_ref, s2_ref,
                        qo_ref, kv0_ref, kv1_ref, kv2_ref, t0_ref, t1_ref, t2_ref, *, unit):
    q, k, v = _rope_qkv(q_ref, k_ref, v_ref, c_ref, s1_ref, s2_ref)
    qo_ref[0] = q
    for g, (kvo_ref, to_ref) in enumerate(((kv0_ref, t0_ref), (kv1_ref, t1_ref), (kv2_ref, t2_ref))):
        kv = jnp.concatenate([k[:, g * B_GW:(g + 1) * B_GW], v[:, g * B_GW:(g + 1) * B_GW]], axis=1)
        kvo_ref[0] = kv
        for t in range(q.shape[0] // unit):
            to_ref[t] = kv[t * unit:(t + 1) * unit, :].T


def _rope_sample_call(proj3, tabs, unit):
    _, rows, _ = proj3.shape
    T = rows // unit
    kv_shape = jax.ShapeDtypeStruct((1, rows, 2 * B_GW), F32)
    kv_spec = pl.BlockSpec((1, rows, 2 * B_GW), lambda n, i: (0, 0, 0))
    t_shape = jax.ShapeDtypeStruct((T, 2 * B_GW, unit), F32)
    t_spec = pl.BlockSpec((T, 2 * B_GW, unit), lambda n, i: (0, 0, 0))
    return pl.pallas_call(
        functools.partial(_rope_sample_kernel, unit=unit),
        grid=(1, 1),
        in_specs=_rope_in_specs(rows),
        out_specs=[pl.BlockSpec((1, rows, B_W), lambda n, i: (0, 0, 0))] + [kv_spec] * 3 + [t_spec] * 3,
        out_shape=[jax.ShapeDtypeStruct((1, rows, B_W), F32)] + [kv_shape] * 3 + [t_shape] * 3,
        compiler_params=_cparams(("arbitrary", "arbitrary")),
        name="rope_sample",
    )(proj3, proj3, proj3, *tabs)


def _rope_tables(pos):
    half = B_ROT // 2
    inv = ROPE_THETA ** (-jnp.arange(half, dtype=F32) / half)
    ang = pos.astype(F32)[:, None] * inv
    cos, sin = jnp.cos(ang), jnp.sin(ang)
    rows = pos.shape[0]
    one = jnp.ones((rows, B_HD - B_ROT), F32)
    zero = jnp.zeros((rows, half), F32)
    zrest = jnp.zeros((rows, B_HD - B_ROT), F32)
    c = jnp.concatenate([cos, cos, one], axis=1)
    s1 = jnp.concatenate([-sin, zero, zrest], axis=1)
    s2 = jnp.concatenate([zero, sin, zrest], axis=1)
    tile = lambda t: jnp.concatenate([t, t], axis=1)[None]
    return tile(c), tile(s1), tile(s2)


def _head_lane_mask(rows):
    lane = lax.broadcasted_iota(jnp.int32, (rows, B_GW), 1)
    return [(lane >= h * B_HD) & (lane < (h + 1) * B_HD) for h in range(B_HEADS)]


def _attn_prompt_kernel(q_ref, kvp_ref, kvc_ref, o_ref, l_ref):
    i = pl.program_id(2)
    nk = B_NK
    nr, nblk = q_ref.shape[1], q_ref.shape[2] // nk
    qi = lax.broadcasted_iota(jnp.int32, (nk, 2 * nk), 0)
    kj = lax.broadcasted_iota(jnp.int32, (nk, 2 * nk), 1)
    band = (kj >= qi) & (kj <= qi + nk)
    band_first = band & ((kj >= nk) | (i > 0))
    masks = _head_lane_mask(nk)
    zq = jnp.zeros((nk, B_GW), BF16)
    q, kk, vv = [], [], []
    for r in range(nr):
        q.append(q_ref[0, r])
        kk.append(jnp.concatenate([kvp_ref[0, r, :, 0:B_GW], kvc_ref[0, r, :, 0:B_GW]], axis=0).astype(BF16))
        vv.append(jnp.concatenate([kvp_ref[0, r, :, B_GW:], kvc_ref[0, r, :, B_GW:]], axis=0).astype(BF16))
    chains = [(r, j, h) for r in range(nr) for j in range(nblk) for h in range(B_HEADS)]
    s, pn, lse, oh = {}, {}, {}, {}
    for r, j, h in chains:
        s[r, j, h] = _dot_nt(jnp.where(masks[h], q[r][j * nk:(j + 1) * nk], zq), kk[r][j * nk:(j + 2) * nk])
    for r, j, h in chains:
        sm = jnp.where(band_first if j == 0 else band, s[r, j, h], -jnp.inf)
        m = jnp.max(sm, axis=-1, keepdims=True)
        p = jnp.exp(sm - m)
        den = jnp.sum(p, axis=-1, keepdims=True)
        lse[r, j, h] = m + jnp.log(den)
        pn[r, j, h] = (p / den).astype(BF16)
    for r, j, h in chains:
        oh[r, j, h] = jnp.dot(pn[r, j, h], vv[r][j * nk:(j + 2) * nk], preferred_element_type=F32)
    for r in range(nr):
        for j in range(nblk):
            o_acc = jnp.zeros((nk, B_GW), F32)
            l_acc = jnp.zeros((nk, B_GW), F32)
            for h in range(B_HEADS):
                o_acc = jnp.where(masks[h], oh[r, j, h], o_acc)
                l_acc = jnp.where(masks[h], lse[r, j, h], l_acc)
            o_ref[0, r, j * nk:(j + 1) * nk, :] = o_acc
            l_ref[0, r, j * nk:(j + 1) * nk, :] = l_acc


AP_ROWS = 1024


def _attn_prompt_call(q, kv):
    B, dil, n, _ = q.shape
    tq = min(AP_ROWS, n)
    nr = min(dil, AP_ROWS // tq)
    per = tq // B_NK
    blk = lambda w: pl.BlockSpec((1, nr, tq, w), lambda b, r, i: (b, r, i, 0))
    out = jax.ShapeDtypeStruct((B, dil, n, B_GW), F32)
    return pl.pallas_call(
        _attn_prompt_kernel,
        grid=(B, dil // nr, n // tq),
        in_specs=[blk(B_GW),
                  pl.BlockSpec((1, nr, B_NK, 2 * B_GW), lambda b, r, i: (b, r, jnp.maximum(i * per - 1, 0), 0)),
                  blk(2 * B_GW)],
        out_specs=[blk(B_GW), blk(B_GW)],
        out_shape=[out, out],
        compiler_params=_cparams(("parallel", "parallel", "arbitrary")),
        name="attn_prompt_d%d" % dil,
    )(q, kv, kv)


AS_QB = 16
AS_CACHE_BYTES = 16 * 1024 * 1024
AS_R = 8
AS_NEW = 16


def _attn_sample_kernel(q_ref, kvn_ref, c_ref, o_ref, l_ref, *, dil, g, sub_blocks):
    T = q_ref.shape[0]
    cb, Lb = c_ref.shape[1], c_ref.shape[3]
    R, NR = AS_R, B_HEADS * AS_R
    sub = pl.program_id(0) % sub_blocks
    row = lax.broadcasted_iota(jnp.int32, (NR, B_GW), 0)
    lane = lax.broadcasted_iota(jnp.int32, (NR, B_GW), 1)
    hmask = (lane // B_HD) == (row // R)

    def query_of(shape):
        t = lax.broadcasted_iota(jnp.int32, shape, 0) % R
        return jnp.where(t < T, t, 0)

    pos = lax.broadcasted_iota(jnp.int32, (NR, Lb), 1)
    tn = lax.broadcasted_iota(jnp.int32, (NR, AS_NEW), 1)
    if dil == 1:
        valid_c = pos >= query_of((NR, Lb))
        valid_n = tn <= query_of((NR, AS_NEW))
    else:
        valid_c = (pos % dil) == query_of((NR, Lb))
        valid_n = tn == query_of((NR, AS_NEW))

    ch = []
    for bi in range(cb):
        b = sub * cb + bi
        qrows = jnp.concatenate([q_ref[t, pl.ds(b, 1), g * B_GW:(g + 1) * B_GW] for t in range(T)]
                                + [jnp.zeros((R - T, B_GW), F32)], axis=0)
        qbd = jnp.where(hmask, jnp.concatenate([qrows] * B_HEADS, axis=0), 0.0)
        new = jnp.concatenate([kvn_ref[t, pl.ds(b, 1), :] for t in range(T)]
                              + [jnp.zeros((AS_NEW - T, 2 * B_GW), F32)], axis=0)
        ch.append(dict(bi=bi, b=b, qbd=qbd, new=new))
    for d in ch:
        d["s"] = jnp.where(valid_c, _dot(d["qbd"], c_ref[0, d["bi"], 0:B_GW, :]), -jnp.inf)
        d["sn"] = jnp.where(valid_n, _dot_nt(d["qbd"], d["new"][:, 0:B_GW]), -jnp.inf)
    for d in ch:
        m = jnp.maximum(jnp.max(d["s"], axis=-1, keepdims=True), jnp.max(d["sn"], axis=-1, keepdims=True))
        p = jnp.exp(d["s"] - m)
        pn = jnp.exp(d["sn"] - m)
        den = jnp.sum(p, axis=-1, keepdims=True) + jnp.sum(pn, axis=-1, keepdims=True)
        d["p"], d["pn"], d["lse"] = p / den, pn / den, m + jnp.log(den)
    for d in ch:
        d["o"] = (_dot_nt(d["p"], c_ref[0, d["bi"], B_GW:2 * B_GW, :])
                  + _dot(d["pn"], d["new"][:, B_GW:]))
    for d in ch:
        om = jnp.where(hmask, d["o"], 0.0)
        lm = jnp.where(hmask, d["lse"], 0.0)
        ob = om[0:R]
        lb = lm[0:R]
        for h in range(1, B_HEADS):
            ob = ob + om[h * R:(h + 1) * R]
            lb = lb + lm[h * R:(h + 1) * R]
        for t in range(T):
            o_ref[t, pl.ds(d["b"], 1), :] = ob[t:t + 1]
            l_ref[t, pl.ds(d["b"], 1), :] = lb[t:t + 1]


def _attn_sample_call(q, kvn, cache_t, layer, g, dil):
    T, B, _ = q.shape
    Lb = cache_t.shape[3]
    qb = min(AS_QB, B)
    cb = max(1, min(qb, AS_CACHE_BYTES // (2 * B_GW * Lb * 4)))
    sub_blocks = qb // cb
    out = jax.ShapeDtypeStruct((T, B, B_GW), F32)
    row_spec = lambda w: pl.BlockSpec((T, qb, w), lambda i: (0, i // sub_blocks, 0))
    return pl.pallas_call(
        functools.partial(_attn_sample_kernel, dil=dil, g=g, sub_blocks=sub_blocks),
        grid=(B // cb,),
        in_specs=[row_spec(B_W), row_spec(2 * B_GW),
                  pl.BlockSpec((1, cb, 2 * B_GW, Lb), lambda i: (layer, i, 0, 0))],
        out_specs=[row_spec(B_GW), row_spec(B_GW)],
        out_shape=[out, out],
        compiler_params=_cparams(("arbitrary",)),
        name="attn_sample_d%d" % dil,
    )(q, kvn, cache_t)


def _shift_rows(x, s, fill):
    rolled = pltpu.roll(x, s, 0)
    r = lax.broadcasted_iota(jnp.int32, x.shape, 0)
    return jnp.where(r < s, fill, rolled)


def _conv_init(xx_ref, prev, hp):
    halo = prev.shape[0]
    if halo < hp:
        xx_ref[0:hp, :] = jnp.zeros((hp, xx_ref.shape[1]), F32)
    xx_ref[hp - halo:hp, :] = prev


def _conv_taps(xx_ref, x, w_ref, b_ref, n_taps, unit, hp, c0, c1):
    tm = x.shape[0]
    halo = (n_taps - 1) * unit
    y = b_ref[:, c0:c1] + x * w_ref[n_taps - 1:n_taps, c0:c1]
    if unit % 8 == 0:
        xx_ref[hp:hp + tm, c0:c1] = x
        for j in range(n_taps - 1):
            y = y + xx_ref[hp - halo + j * unit:hp - halo + j * unit + tm, c0:c1] * w_ref[j:j + 1, c0:c1]
        tail = xx_ref[hp + tm - halo:hp + tm, c0:c1]
        xx_ref[hp - halo:hp, c0:c1] = tail
    else:
        xcat = jnp.concatenate([xx_ref[0:hp, c0:c1], x], axis=0)
        for j in range(n_taps - 1):
            y = y + pltpu.roll(xcat, halo - j * unit, 0)[hp:hp + tm, :] * w_ref[j:j + 1, c0:c1]
        tail = x[tm - halo:tm, :]
        xx_ref[0:hp, c0:c1] = x[tm - hp:tm, :]
    return y, tail


def _rglru_kernel(x_ref, g_ref, prev_ref, h0_ref, cw_ref, cb_ref, wr_ref, br_ref, wi_ref, bi_ref, lam_ref,
                  o_ref, hl_ref, xx_ref, hc_ref, *, unit, hp):
    i = pl.program_id(1)
    tm = x_ref.shape[1]

    @pl.when(i == 0)
    def _():
        _conv_init(xx_ref, prev_ref[0], hp)
        hc_ref[...] = h0_ref[0]

    y, _ = _conv_taps(xx_ref, x_ref[0].astype(F32), cw_ref, cb_ref, C_CONV, unit, hp, 0, C_WIDTH)

    r = _sigmoid(_dot(y, wr_ref[...]) + br_ref[...])
    ig = _sigmoid(_dot(y, wi_ref[...]) + bi_ref[...])
    log_a = -C_POW * r * _softplus(-lam_ref[...])
    a = jnp.exp(log_a)
    th = jnp.tanh(log_a)
    bx = jnp.sqrt(-2.0 * th / (1.0 - th)) * (ig * y)
    hc = hc_ref[...]
    if unit == 1:
        rr = lax.broadcasted_iota(jnp.int32, bx.shape, 0)
        bx = jnp.where(rr < 1, a * hc + bx, bx)
    elif unit == tm:
        bx = a * hc + bx
    else:
        bx = jnp.concatenate([a[0:unit] * hc + bx[0:unit], bx[unit:]], axis=0)
    s = unit
    while s < tm:
        if s % 8 == 0:
            bx = jnp.concatenate([bx[0:s], a[s:] * bx[0:tm - s] + bx[s:]], axis=0)
            a = jnp.concatenate([a[0:s], a[s:] * a[0:tm - s]], axis=0)
        else:
            a_sh = _shift_rows(a, s, 1.0)
            b_sh = _shift_rows(bx, s, 0.0)
            bx = a * b_sh + bx
            a = a * a_sh
        s *= 2
    h = bx
    hl = h[tm - unit:tm, :]
    hc_ref[...] = hl
    hl_ref[0] = hl
    o_ref[0] = h * jax.nn.gelu(g_ref[0].astype(F32))


def _rglru_call(proj3, prev, h0, cw, cb, wr, br, wi, bi, lam, unit, tm):
    nseq, rows, _ = proj3.shape
    halo = (C_CONV - 1) * unit
    hp = -(-halo // 8) * 8
    assert unit == 1 or unit % 8 == 0
    vec = lambda: pl.BlockSpec((1, C_WIDTH), lambda n, i: (0, 0))
    return pl.pallas_call(
        functools.partial(_rglru_kernel, unit=unit, hp=hp),
        grid=(nseq, rows // tm),
        in_specs=[pl.BlockSpec((1, tm, C_WIDTH), lambda n, i: (n, i, OFF_XC // C_WIDTH)),
                  pl.BlockSpec((1, tm, C_WIDTH), lambda n, i: (n, i, OFF_GC // C_WIDTH)),
                  pl.BlockSpec((1, halo, C_WIDTH), lambda n, i: (n, 0, 0)),
                  pl.BlockSpec((1, unit, C_WIDTH), lambda n, i: (n, 0, 0)),
                  pl.BlockSpec((C_CONV, C_WIDTH), lambda n, i: (0, 0)), vec(),
                  pl.BlockSpec((C_WIDTH, C_WIDTH), lambda n, i: (0, 0)), vec(),
                  pl.BlockSpec((C_WIDTH, C_WIDTH), lambda n, i: (0, 0)), vec(), vec()],
        out_specs=[pl.BlockSpec((1, tm, C_WIDTH), lambda n, i: (n, i, 0)),
                   pl.BlockSpec((1, unit, C_WIDTH), lambda n, i: (n, 0, 0))],
        out_shape=[jax.ShapeDtypeStruct((nseq, rows, C_WIDTH), F32),
                   jax.ShapeDtypeStruct((nseq, unit, C_WIDTH), F32)],
        scratch_shapes=[pltpu.VMEM((hp + tm, C_WIDTH), F32), pltpu.VMEM((unit, C_WIDTH), F32)],
        compiler_params=_cparams(("parallel", "arbitrary")),
        name="rglru_u%d" % unit,
    )(proj3, proj3, prev, h0, cw, cb, wr, br, wi, bi, lam)


def _merge_kernel(x_ref, ga_ref, gb_ref, gc_ref, oa_ref, oc_ref,
                  o0_ref, o1_ref, o2_ref, l0_ref, l1_ref, l2_ref,
                  wpa_ref, wpb_ref, wpc_ref, wo_ref, g_ref, b_ref, out_ref, il_ref, *, dils):
    tm = x_ref.shape[1]

    def rows_in_order(ref, dil):
        if dil == 1:
            return ref[0, 0]
        for r in range(dil):
            for j in range(il_ref.shape[0]):
                il_ref[j, pl.ds(r, tm // dil, stride=dil), :] = ref[0, r, :, j * 128:(j + 1) * 128]
        return jnp.concatenate([il_ref[j] for j in range(il_ref.shape[0])], axis=1)

    o0, o1, o2 = (rows_in_order(r, d) for r, d in zip((o0_ref, o1_ref, o2_ref), dils))
    l0, l1, l2 = (rows_in_order(r, d) for r, d in zip((l0_ref, l1_ref, l2_ref), dils))
    m = jnp.maximum(jnp.maximum(l0, l1), l2)
    e0, e1, e2 = jnp.exp(l0 - m), jnp.exp(l1 - m), jnp.exp(l2 - m)
    ob = (e0 * o0 + e1 * o1 + e2 * o2) / (e0 + e1 + e2)
    merged = (_sigmoid(ga_ref[0].astype(F32)) * _dot(oa_ref[0], wpa_ref[...])
              + _sigmoid(gb_ref[0].astype(F32)) * _dot(ob, wpb_ref[...])
              + _sigmoid(gc_ref[0].astype(F32)) * _dot(oc_ref[0], wpc_ref[...]))
    mix = _dot(merged, wo_ref[...])
    out_ref[0] = _layer_norm(DN_ALPHA * x_ref[0] + mix, g_ref[...], b_ref[...])


def _merge_call(x3, proj3, oa, oc, obs, lses, wpa, wpb, wpc, wo, g, b, tm):
    nseq, rows, _ = x3.shape
    dils = tuple(o.shape[1] for o in obs)
    row_spec = lambda w, j=0: pl.BlockSpec((1, tm, w), lambda n, i: (n, i, j))
    grp_spec = lambda d: pl.BlockSpec((1, d, tm // d, B_GW), lambda n, i: (n, 0, i, 0))
    full = lambda a: pl.BlockSpec(a.shape, lambda n, i: (0, 0))
    return pl.pallas_call(
        functools.partial(_merge_kernel, dils=dils),
        grid=(nseq, rows // tm),
        in_specs=[row_spec(D_MODEL)] + [row_spec(D_MODEL, OFF_GATES // D_MODEL + j) for j in range(3)]
                 + [row_spec(A_VW), row_spec(C_WIDTH)]
                 + [grp_spec(d) for d in dils] * 2
                 + [full(wpa), full(wpb), full(wpc), full(wo), full(g), full(b)],
        out_specs=row_spec(D_MODEL),
        out_shape=jax.ShapeDtypeStruct((nseq, rows, D_MODEL), F32),
        scratch_shapes=[pltpu.VMEM((B_GW // 128, tm, 128), F32)],
        compiler_params=_cparams(("parallel", "parallel")),
        name="merge_ln",
    )(x3, proj3, proj3, proj3, oa, oc, *obs, *lses, wpa, wpb, wpc, wo, g, b)


def _ffn_kernel(x_ref, prev_ref, wu_ref, cw_ref, cb_ref, wd_ref, g_ref, b_ref,
                o_ref, st_ref, xx_ref, *, unit, hp):
    i = pl.program_id(1)
    tm = x_ref.shape[1]

    @pl.when(i == 0)
    def _():
        _conv_init(xx_ref, prev_ref[0], hp)

    x = x_ref[0]
    xb = x.astype(BF16)
    f = jnp.zeros((tm, D_MODEL), F32)
    for c0 in range(0, D_FF, FF_CHUNK):
        c1 = c0 + FF_CHUNK
        gate = jnp.dot(xb, wu_ref[:, c0:c1], preferred_element_type=F32)
        up = jnp.dot(xb, wu_ref[:, D_FF + c0:D_FF + c1], preferred_element_type=F32)
        y, tail = _conv_taps(xx_ref, gate, cw_ref, cb_ref, F_CONV, unit, hp, c0, c1)
        st_ref[0, :, c0:c1] = tail
        f = f + _dot(_silu(y) * up, wd_ref[c0:c1, :])
    o_ref[0] = _layer_norm(DN_ALPHA * x + f, g_ref[...], b_ref[...])


def _ffn_call(x3, prev, wu, cw, cb, wd, g, b, unit, tm):
    nseq, rows, _ = x3.shape
    halo = (F_CONV - 1) * unit
    hp = -(-halo // 8) * 8
    const = lambda a: pl.BlockSpec(a.shape, lambda n, i: (0, 0), pipeline_mode=pl.Buffered(1))
    return pl.pallas_call(
        functools.partial(_ffn_kernel, unit=unit, hp=hp),
        grid=(nseq, rows // tm),
        in_specs=[pl.BlockSpec((1, tm, D_MODEL), lambda n, i: (n, i, 0)),
                  pl.BlockSpec((1, halo, D_FF), lambda n, i: (n, 0, 0)),
                  const(wu), const(cw), const(cb), const(wd), const(g), const(b)],
        out_specs=[pl.BlockSpec((1, tm, D_MODEL), lambda n, i: (n, i, 0)),
                   pl.BlockSpec((1, halo, D_FF), lambda n, i: (n, 0, 0))],
        out_shape=[jax.ShapeDtypeStruct((nseq, rows, D_MODEL), F32),
                   jax.ShapeDtypeStruct((nseq, halo, D_FF), F32)],
        scratch_shapes=[pltpu.VMEM((hp + tm, D_FF), F32)],
        compiler_params=_cparams(("parallel", "arbitrary")),
        name="conv_ffn_u%d" % unit,
    )(x3, prev, wu, cw, cb, wd, g, b)


def _block_diag(w):
    nblk, bw, _ = w.shape
    eye = jnp.eye(nblk, dtype=w.dtype)
    return (eye[:, None, :, None] * w[:, :, None, :]).reshape(nblk * bw, nblk * bw)


def _layer_params(l, w_in, a_conv_w, a_A_log, a_dt_bias, a_norm_w, c_conv_w, c_conv_b, c_w_r, c_b_r,
                  c_w_i, c_b_i, c_lam, w_pa, w_pb, w_pc, w_o, ln1_g, ln1_b, f_up, f_conv_w, f_conv_b,
                  f_down, ln2_g, ln2_b):
    w = w_in[l]
    o_b = A_QKV
    o_a = o_b + A_HEADS
    o_z = o_a + A_HEADS
    o_qb = o_z + A_VW
    o_xc = o_qb + 3 * B_W
    o_gc = o_xc + C_WIDTH
    o_gt = o_gc + C_WIDTH
    pad = jnp.zeros((D_MODEL, N_PROJ - OFF_BA - 2 * A_HEADS), w.dtype)
    wp = jnp.concatenate([w[:, o_gt:o_gt + 3 * D_MODEL], w[:, o_xc:o_gc], w[:, o_gc:o_gt], w[:, o_z:o_qb],
                          w[:, 0:o_b], w[:, o_qb:o_xc], w[:, o_b:o_z], pad], axis=1).astype(BF16)
    gp = jnp.zeros((2, 128), F32)
    gp = gp.at[0, A_HEADS:2 * A_HEADS].set(a_A_log[l]).at[1, A_HEADS:2 * A_HEADS].set(a_dt_bias[l])
    row = lambda v: v.reshape(1, -1)
    return dict(
        wp=wp, a_cw=a_conv_w[l], gp=gp, a_nw=row(a_norm_w[l]),
        c_cw=c_conv_w[l], c_cb=row(c_conv_b[l]),
        c_wr=_block_diag(c_w_r[l]).astype(BF16), c_br=row(c_b_r[l]),
        c_wi=_block_diag(c_w_i[l]).astype(BF16), c_bi=row(c_b_i[l]), c_lam=row(c_lam[l]),
        wpa=w_pa[l].astype(BF16), wpb=w_pb[l].astype(BF16), wpc=w_pc[l].astype(BF16), wo=w_o[l].astype(BF16),
        ln1_g=row(ln1_g[l]), ln1_b=row(ln1_b[l]),
        f_up=f_up[l].astype(BF16), f_cw=f_conv_w[l], f_cb=row(f_conv_b[l]), f_down=f_down[l].astype(BF16),
        ln2_g=row(ln2_g[l]), ln2_b=row(ln2_b[l]))


def _layer_tail(x3, proj3, o_a, obs, lses, unit, p, c_prev, c_h0, f_prev, tm_merge, tm_seq, tm_ffn):
    o_c, c_hl = _rglru_call(proj3, c_prev, c_h0, p["c_cw"], p["c_cb"], p["c_wr"], p["c_br"], p["c_wi"],
                            p["c_bi"], p["c_lam"], unit, tm_seq)
    x1 = _merge_call(x3, proj3, o_a, o_c, obs, lses, p["wpa"], p["wpb"], p["wpc"], p["wo"],
                     p["ln1_g"], p["ln1_b"], tm_merge)
    x_out, f_state = _ffn_call(x1, f_prev, p["f_up"], p["f_cw"], p["f_cb"], p["f_down"],
                               p["ln2_g"], p["ln2_b"], unit, tm_ffn)
    return x_out, c_hl, f_state


def kernel(x_prompt, x_sample, state_a_conv, state_a_rec, cache_b_w128, cache_b_w512, cache_b_w2048,
           state_c_conv, state_c_h, state_f_conv, ln_in_g, ln_in_b, w_in, a_conv_w, a_A_log, a_dt_bias,
           a_norm_w, c_conv_w, c_conv_b, c_w_r, c_b_r, c_w_i, c_b_i, c_lam, w_pa, w_pb, w_pc, w_o,
           ln1_g, ln1_b, f_up, f_conv_w, f_conv_b, f_down, ln2_g, ln2_b):
    Bp, Sp, _ = x_prompt.shape
    Bs, Ts, _ = x_sample.shape
    caches = (cache_b_w128, cache_b_w512, cache_b_w2048)
    for (win, dil), cache in zip(B_GROUPS, caches):
        assert win // dil == B_NK and cache.shape[2] == win and Sp % (dil * B_NK) == 0

    caches_t = [jnp.transpose(c, (0, 1, 3, 4, 5, 2)).reshape(DEPTH, Bs, 2 * B_GW, c.shape[2]) for c in caches]
    Rs = Ts * Bs

    lng, lnb = ln_in_g.reshape(1, -1), ln_in_b.reshape(1, -1)
    hp = _ln_call(x_prompt.reshape(Bp * Sp, D_MODEL), lng, lnb, 512)
    xs_tm = jnp.swapaxes(x_sample, 0, 1).reshape(Rs, D_MODEL)
    hs = _ln_call(xs_tm, lng, lnb, Rs)

    tabs_p = _rope_tables(jnp.arange(Sp))
    tabs_s = _rope_tables(jnp.repeat(PAST_LEN + jnp.arange(Ts), Bs))

    zeros_c = jnp.zeros((Bp, C_CONV - 1, C_WIDTH), F32)
    zeros_h = jnp.zeros((Bp, 1, C_WIDTH), F32)
    zeros_f = jnp.zeros((Bp, F_CONV - 1, D_FF), F32)

    outs_p, outs_s = [], []
    for l in range(DEPTH):
        p = _layer_params(l, w_in, a_conv_w, a_A_log, a_dt_bias, a_norm_w, c_conv_w, c_conv_b, c_w_r, c_b_r,
                          c_w_i, c_b_i, c_lam, w_pa, w_pb, w_pc, w_o, ln1_g, ln1_b, f_up, f_conv_w,
                          f_conv_b, f_down, ln2_g, ln2_b)

        proj3 = _proj_call(hp, p["wp"], 1024, BF16).reshape(Bp, Sp, N_PROJ)
        hp_tail = hp.reshape(Bp, Sp, D_MODEL)[:, Sp - TAIL_ROWS:].reshape(Bp * TAIL_ROWS, D_MODEL)
        proj_tail = _proj_call(hp_tail, p["wp"], Bp * TAIL_ROWS, F32).reshape(Bp, TAIL_ROWS, N_PROJ)
        o_a, a_rec = _gdn_prompt_call(proj3, p["a_cw"], p["gp"], p["a_nw"])
        rp = _rope_prompt_call(proj3, tabs_p, 512)
        res = [_attn_prompt_call(rp[g], rp[3 + g]) for g in range(len(B_GROUPS))]
        hp3, c_hl, f_st = _layer_tail(hp.reshape(Bp, Sp, D_MODEL), proj3, o_a, [r[0] for r in res],
                                      [r[1] for r in res], 1, p, zeros_c, zeros_h, zeros_f, 512, 256, 512)
        hp = hp3.reshape(Bp * Sp, D_MODEL)
        kv_rows_p = [t.reshape(Bp, 2, B_HEADS, B_HD, t.shape[-1]).transpose(0, 4, 1, 2, 3) for t in rp[6:9]]
        outs_p.append((
            proj_tail[:, TAIL_ROWS - (A_CONV - 1):, OFF_QKVA:OFF_QKVA + A_QKV],
            a_rec, kv_rows_p[0], kv_rows_p[1], kv_rows_p[2],
            proj_tail[:, TAIL_ROWS - (C_CONV - 1):, OFF_XC:OFF_XC + C_WIDTH],
            c_hl[:, 0],
            f_st))

        proj3 = _proj_call(hs, p["wp"], Rs, F32).reshape(1, Rs, N_PROJ)
        pt = proj3.reshape(Ts, Bs, N_PROJ)
        a_st = jnp.swapaxes(state_a_conv[l], 0, 1)
        o_a, a_rec = _gdn_sample_call(pt, a_st, state_a_rec, l, p["a_cw"], p["gp"], p["a_nw"])
        rs = _rope_sample_call(proj3, tabs_s, Bs)
        q3 = rs[0].reshape(Ts, Bs, B_W)
        res = [_attn_sample_call(q3, rs[1 + g].reshape(Ts, Bs, 2 * B_GW), caches_t[g], l, g, dil)
               for g, (_, dil) in enumerate(B_GROUPS)]
        c_prev = jnp.swapaxes(state_c_conv[l], 0, 1).reshape(1, (C_CONV - 1) * Bs, C_WIDTH)
        f_prev = jnp.swapaxes(state_f_conv[l], 0, 1).reshape(1, (F_CONV - 1) * Bs, D_FF)
        hs3, c_hl, f_st = _layer_tail(hs.reshape(1, Rs, D_MODEL), proj3, o_a.reshape(1, Rs, A_VW),
                                      [r[0].reshape(1, 1, Rs, B_GW) for r in res],
                                      [r[1].reshape(1, 1, Rs, B_GW) for r in res],
                                      Bs, p, c_prev, state_c_h[l][None], f_prev, min(256, Rs), Rs, Rs)
        hs = hs3.reshape(Rs, D_MODEL)
        tm2bm = lambda a: jnp.swapaxes(a, 0, 1)
        kv_rows_s = [t.reshape(Ts, 2, B_HEADS, B_HD, Bs).transpose(4, 0, 1, 2, 3) for t in rs[4:7]]
        outs_s.append((
            tm2bm(pt[Ts - (A_CONV - 1):, :, OFF_QKVA:OFF_QKVA + A_QKV]),
            a_rec, kv_rows_s[0], kv_rows_s[1], kv_rows_s[2],
            tm2bm(pt[Ts - (C_CONV - 1):, :, OFF_XC:OFF_XC + C_WIDTH]),
            c_hl[0],
            tm2bm(f_st.reshape(F_CONV - 1, Bs, D_FF))))

    stack = lambda outs: [jnp.stack(s, 0) for s in zip(*outs)]
    y_p = hp.reshape(Bp, Sp, D_MODEL)
    y_s = jnp.swapaxes(hs.reshape(Ts, Bs, D_MODEL), 0, 1)
    return (y_p, y_s, *stack(outs_p), *stack(outs_s))
```
